```python
import jax, jax.numpy as jnp
from jax import lax
import numpy as np

D_MODEL = 1024
BATCH = 16
SEQ = 2048
DEPTH = 4

CHUNK = 64
Q_BLOCK = 128
ROPE_THETA = 10000.0
LN_EPS = 1e-5
RMS_EPS = 1e-6

A_HEADS = 8
A_HEAD_DIM = 64
IDX_HEADS = 4
IDX_DIM = 64
TOPK_MAX = 256

B_HEADS = 8
B_NOPE = 64
B_ROPE = 32
B_V = 64
B_Q_RANK = 384
B_KV_RANK = 256

C_HEADS = 16
C_HEAD_DIM = D_MODEL // C_HEADS

D_FF = 2816
CONV_WIDTH = 3

EVEN_IN_SIZES = (A_HEADS * A_HEAD_DIM, A_HEAD_DIM, A_HEAD_DIM, IDX_HEADS * IDX_DIM, IDX_DIM, IDX_HEADS, B_Q_RANK, B_KV_RANK, B_ROPE)
EVEN_IN_DIM = sum(EVEN_IN_SIZES)
EVEN_OUT_DIM = A_HEADS * A_HEAD_DIM + B_HEADS * B_V
N_EVEN = (DEPTH + 1) // 2
N_ODD = DEPTH // 2
DEEPNORM_ALPHA = (2 * DEPTH) ** 0.25
DEEPNORM_BETA = (8 * DEPTH) ** -0.25

kernel_name = 'hybrid_dsa_mla_stickbreak_convffn_deepnorm'


def split_sizes(x, sizes):
    idx = np.cumsum(sizes)[:-1].tolist()
    return jnp.split(x, idx, axis=-1)


def layer_norm(x, g, b):
    xf = x.astype(jnp.float32)
    mu = xf.mean(-1, keepdims=True)
    var = jnp.square(xf - mu).mean(-1, keepdims=True)
    y = (xf - mu) * lax.rsqrt(var + LN_EPS)
    return (y * g.astype(jnp.float32) + b.astype(jnp.float32)).astype(x.dtype)


def rms_norm(x, g):
    xf = x.astype(jnp.float32)
    y = xf * lax.rsqrt(jnp.mean(xf * xf, -1, keepdims=True) + RMS_EPS)
    return (y * g.astype(jnp.float32)).astype(x.dtype)


def rope_tables(positions, dim):
    inv_freq = ROPE_THETA ** (-jnp.arange(0, dim, 2, dtype=jnp.float32) / dim)
    ang = positions.astype(jnp.float32)[..., None] * inv_freq
    return jnp.cos(ang), jnp.sin(ang)


def apply_rope(x, cos, sin):
    if x.ndim == 4:
        cos, sin = cos[:, :, None, :], sin[:, :, None, :]
    c = cos.astype(x.dtype)
    s = sin.astype(x.dtype)
    x1, x2 = jnp.split(x, 2, axis=-1)
    return jnp.concatenate([x1 * c - x2 * s, x2 * c + x1 * s], axis=-1)


def to_blocks(x):
    b, s = x.shape[:2]
    return jnp.moveaxis(x.reshape(b, s // Q_BLOCK, Q_BLOCK, *x.shape[2:]), 1, 0)


def from_blocks(y):
    y = jnp.moveaxis(y, 0, 1)
    return y.reshape(y.shape[0], -1, *y.shape[3:])


def dsa_attention(q, k, v, iq, ik, iw, frame):
    seq = k.shape[1]
    topk = min(TOPK_MAX, seq // 4)
    key_chunk = frame // CHUNK

    def block(args):
        qb, iqb, iwb, tb = args
        q_chunk = tb // CHUNK
        admissible = key_chunk[None, :] <= q_chunk[:, None]
        rel = jax.nn.relu(jnp.einsum('bthd,bsd->bths', iqb, ik).astype(jnp.float32) * IDX_DIM ** -0.5)
        score = jnp.einsum('bths,bth->bts', rel, iwb.astype(jnp.float32))
        score = jnp.where(admissible[None], score, -jnp.inf)
        _, idx = lax.top_k(score, topk)
        valid = (idx // CHUNK) <= q_chunk[None, :, None]
        k_sel = jax.vmap(lambda kb, ib: kb[ib])(k, idx)
        v_sel = jax.vmap(lambda vb, ib: vb[ib])(v, idx)
        logits = jnp.einsum('bthd,btkd->bthk', qb, k_sel).astype(jnp.float32) * A_HEAD_DIM ** -0.5
        logits = jnp.where(valid[:, :, None, :], logits, -jnp.inf)
        p = jax.nn.softmax(logits, axis=-1).astype(v.dtype)
        return jnp.einsum('bthk,btkd->bthd', p, v_sel)

    out = lax.map(block, (to_blocks(q), to_blocks(iq), to_blocks(iw), frame.reshape(-1, Q_BLOCK)))
    return from_blocks(out)


def mla_attention(q, k_nope, k_rope, v, frame):
    key_chunk = frame // CHUNK
    scale = (B_NOPE + B_ROPE) ** -0.5

    def block(args):
        qb, tb = args
        logits = (jnp.einsum('bthd,bshd->bhts', qb[..., :B_NOPE], k_nope)
                  + jnp.einsum('bthd,bsd->bhts', qb[..., B_NOPE:], k_rope)).astype(jnp.float32) * scale
        mask = key_chunk[None, :] <= (tb // CHUNK)[:, None]
        logits = jnp.where(mask, logits, -jnp.inf)
        p = jax.nn.softmax(logits, axis=-1).astype(v.dtype)
        return jnp.einsum('bhts,bshd->bthd', p, v)

    return from_blocks(lax.map(block, (to_blocks(q), frame.reshape(-1, Q_BLOCK))))


def stick_breaking_attention(q, k, v, frame):
    scale = C_HEAD_DIM ** -0.5

    def block(args):
        qb, tb = args
        z = jnp.einsum('bthd,bshd->bhts', qb, k).astype(jnp.float32) * scale
        before = frame[None, :] < tb[:, None]
        log_1m_beta = jnp.where(before, -jax.nn.softplus(z), 0.0)
        tail = lax.cumsum(log_1m_beta, axis=3, reverse=True) - log_1m_beta
        a = jnp.where(before, jnp.exp(jax.nn.log_sigmoid(z) + tail), 0.0).astype(v.dtype)
        return jnp.einsum('bhts,bshd->bthd', a, v)

    return from_blocks(lax.map(block, (to_blocks(q), frame.reshape(-1, Q_BLOCK))))


def even_mixer(h, positions, frame, w_in, idx_k_g, idx_k_b, q_norm_g, kv_norm_g, w_uq, w_ukv, w_out):
    b, s, _ = h.shape
    qa, ka, va, iq, ik, iw, cq, ckv, kr = split_sizes(h @ w_in, EVEN_IN_SIZES)
    cos_a, sin_a = rope_tables(positions, A_HEAD_DIM)
    cos_i, sin_i = rope_tables(positions, IDX_DIM)
    qa = apply_rope(qa.reshape(b, s, A_HEADS, A_HEAD_DIM), cos_a, sin_a)
    ka = apply_rope(ka, cos_a, sin_a)
    iq = apply_rope(iq.reshape(b, s, IDX_HEADS, IDX_DIM), cos_i, sin_i)
    ik = apply_rope(layer_norm(ik, idx_k_g, idx_k_b), cos_i, sin_i)
    iw = iw * IDX_HEADS ** -0.5
    out_a = dsa_attention(qa, ka, va, iq, ik, iw, frame)
    cos_b, sin_b = rope_tables(positions, B_ROPE)
    qb = (rms_norm(cq, q_norm_g) @ w_uq).reshape(b, s, B_HEADS, B_NOPE + B_ROPE)
    qb = jnp.concatenate([qb[..., :B_NOPE], apply_rope(qb[..., B_NOPE:], cos_b, sin_b)], axis=-1)
    kv = (rms_norm(ckv, kv_norm_g) @ w_ukv).reshape(b, s, B_HEADS, B_NOPE + B_V)
    k_rope = apply_rope(kr, cos_b, sin_b)
    out_b = mla_attention(qb, kv[..., :B_NOPE], k_rope, kv[..., B_NOPE:], frame)
    y = jnp.concatenate([out_a.reshape(b, s, -1), out_b.reshape(b, s, -1)], axis=-1)
    return y @ w_out


def odd_mixer(h, frame, w_qkv, w_out):
    b, s, _ = h.shape
    q, k, v = [t.reshape(b, s, C_HEADS, C_HEAD_DIM) for t in jnp.split(h @ w_qkv, 3, axis=-1)]
    return stick_breaking_attention(q, k, v, frame).reshape(b, s, -1) @ w_out


def conv_ffn(h, w_up, conv_w, conv_b, w_down):
    a, g = jnp.split(h @ w_up, 2, axis=-1)
    a = lax.conv_general_dilated(a, conv_w[:, None, :], window_strides=(1,),
                                 padding=[(CONV_WIDTH - 1, 0)],
                                 dimension_numbers=('NWC', 'WIO', 'NWC'),
                                 feature_group_count=D_FF) + conv_b
    return (jax.nn.silu(a) * g) @ w_down


def setup_inputs(seed: int = 0) -> dict:
    key = jax.random.key(seed)
    ks = iter(jax.random.split(key, 23))
    D = D_MODEL

    def nrm(shape, scale):
        return jax.random.normal(next(ks), shape, jnp.float32) * scale

    x = nrm((BATCH, SEQ, D), 1.0)
    c = nrm((BATCH, D), 1.0)
    offsets = jax.random.randint(next(ks), (BATCH, 1), 0, 64, dtype=jnp.int32) * CHUNK
    positions = (offsets + jnp.arange(SEQ, dtype=jnp.int32)[None, :]).astype(jnp.int32)
    return {
        'x': x,
        'c': c,
        'positions': positions,
        'mod_w': nrm((DEPTH, D, 6 * D), 0.1 * D ** -0.5),
        'mod_b': nrm((DEPTH, 6 * D), 0.02),
        'ln_mix_g': 1.0 + nrm((DEPTH, D), 0.02),
        'ln_mix_b': nrm((DEPTH, D), 0.02),
        'ln_ffn_g': 1.0 + nrm((DEPTH, D), 0.02),
        'ln_ffn_b': nrm((DEPTH, D), 0.02),
        'ev_w_in': nrm((N_EVEN, D, EVEN_IN_DIM), D ** -0.5),
        'ev_idx_k_g': 1.0 + nrm((N_EVEN, IDX_DIM), 0.02),
        'ev_idx_k_b': nrm((N_EVEN, IDX_DIM), 0.02),
        'ev_q_norm_g': 1.0 + nrm((N_EVEN, B_Q_RANK), 0.02),
        'ev_kv_norm_g': 1.0 + nrm((N_EVEN, B_KV_RANK), 0.02),
        'ev_w_uq': nrm((N_EVEN, B_Q_RANK, B_HEADS * (B_NOPE + B_ROPE)), B_Q_RANK ** -0.5),
        'ev_w_ukv': nrm((N_EVEN, B_KV_RANK, B_HEADS * (B_NOPE + B_V)), B_KV_RANK ** -0.5),
        'ev_w_out': nrm((N_EVEN, EVEN_OUT_DIM, D), DEEPNORM_BETA * EVEN_OUT_DIM ** -0.5),
        'od_w_qkv': nrm((N_ODD, D, 3 * C_HEADS * C_HEAD_DIM), D ** -0.5),
        'od_w_out': nrm((N_ODD, C_HEADS * C_HEAD_DIM, D), DEEPNORM_BETA * (C_HEADS * C_HEAD_DIM) ** -0.5),
        'ffn_w_up': nrm((DEPTH, D, 2 * D_FF), D ** -0.5),
        'ffn_conv_w': nrm((DEPTH, CONV_WIDTH, D_FF), CONV_WIDTH ** -0.5),
        'ffn_conv_b': nrm((DEPTH, D_FF), 0.02),
        'ffn_w_down': nrm((DEPTH, D_FF, D), DEEPNORM_BETA * D_FF ** -0.5),
    }


def reference(x, c, positions, mod_w, mod_b, ln_mix_g, ln_mix_b, ln_ffn_g, ln_ffn_b,
              ev_w_in, ev_idx_k_g, ev_idx_k_b, ev_q_norm_g, ev_kv_norm_g, ev_w_uq, ev_w_ukv, ev_w_out,
              od_w_qkv, od_w_out, ffn_w_up, ffn_conv_w, ffn_conv_b, ffn_w_down):
    seq = x.shape[1]
    frame = jnp.arange(seq, dtype=jnp.int32)
    c_act = jax.nn.silu(c)
    for l in range(DEPTH):
        mod = (c_act @ mod_w[l] + mod_b[l])[:, None, :]
        sh_m, sc_m, g_m, sh_f, sc_f, g_f = jnp.split(mod, 6, axis=-1)
        h = x * (1 + sc_m) + sh_m
        if l % 2 == 0:
            i = l // 2
            y = even_mixer(h, positions, frame, ev_w_in[i], ev_idx_k_g[i], ev_idx_k_b[i],
                           ev_q_norm_g[i], ev_kv_norm_g[i], ev_w_uq[i], ev_w_ukv[i], ev_w_out[i])
        else:
            i = l // 2
            y = odd_mixer(h, frame, od_w_qkv[i], od_w_out[i])
        x = layer_norm(DEEPNORM_ALPHA * x + (1 + g_m) * y, ln_mix_g[l], ln_mix_b[l])
        h = x * (1 + sc_f) + sh_f
        y = conv_ffn(h, ffn_w_up[l], ffn_conv_w[l], ffn_conv_b[l], ffn_w_down[l])
        x = layer_norm(DEEPNORM_ALPHA * x + (1 + g_f) * y, ln_ffn_g[l], ln_ffn_b[l])
    return x
```

```python
import functools

import numpy as np
import jax
import jax.numpy as jnp
from jax import lax
from jax.experimental import pallas as pl
from jax.experimental.pallas import tpu as pltpu

F32 = jnp.float32
BF16 = jnp.bfloat16

D_MODEL = 1024
DEPTH = 4
CHUNK = 64
ROPE_THETA = 10000.0
LN_EPS = 1e-5
RMS_EPS = 1e-6

A_HEADS = 8
A_HEAD_DIM = 64
IDX_HEADS = 4
IDX_DIM = 64
TOPK_MAX = 256

B_HEADS = 8
B_NOPE = 64
B_ROPE = 32
B_V = 64
B_Q_RANK = 384
B_KV_RANK = 256

C_HEADS = 16
C_HEAD_DIM = D_MODEL // C_HEADS

D_FF = 2816
CONV_WIDTH = 3

EVEN_IN_SIZES = (A_HEADS * A_HEAD_DIM, A_HEAD_DIM, A_HEAD_DIM, IDX_HEADS * IDX_DIM, IDX_DIM, IDX_HEADS,
                 B_Q_RANK, B_KV_RANK, B_ROPE)
DEEPNORM_ALPHA = (2 * DEPTH) ** 0.25

LANES = 128
HALF = 64
TOKEN_TILE = 512
Q_TILE = 256
FF_CHUNK = 256
VMEM_LIMIT = 56 * 2 ** 20

_P_QA = 0
_P_IQ = _P_QA + A_HEADS * A_HEAD_DIM
_P_CQ = _P_IQ + IDX_HEADS * IDX_DIM
_P_CKV = _P_CQ + B_Q_RANK
_P_GK = _P_CKV + B_KV_RANK
_P_GV = _P_GK + LANES
_P_GI = _P_GV + LANES
_P_G5 = _P_GI + LANES
_P_END = _P_G5 + LANES
_KR_LO = HALF
_KR_HI = HALF + B_ROPE
_IW_LO = _KR_HI

_NT = (((1,), (1,)), ((), ()))


def _cparams(sem):
    return pltpu.CompilerParams(dimension_semantics=sem, vmem_limit_bytes=VMEM_LIMIT)


def _const_spec(shape):
    nd = len(shape)
    return pl.BlockSpec(shape, lambda *_: (0,) * nd)


def _lane_iota(shape):
    return lax.broadcasted_iota(jnp.int32, shape, len(shape) - 1)


def _layer_norm(v, g, b):
    mu = jnp.mean(v, axis=-1, keepdims=True)
    d = v - mu
    var = jnp.mean(d * d, axis=-1, keepdims=True)
    return d * lax.rsqrt(var + LN_EPS) * g + b


def _mod_kernel(c_ref, w_ref, b_ref, o_ref):
    c = c_ref[...]
    ca = c / (1.0 + jnp.exp(-c))
    o_ref[0] = jnp.dot(ca, w_ref[0], preferred_element_type=F32,
                       precision=lax.Precision.HIGHEST) + b_ref[0]


def _modulation(c, mod_w, mod_b):
    depth, d, d6 = mod_w.shape
    b = c.shape[0]
    nj = d6 // d
    return pl.pallas_call(
        _mod_kernel,
        grid=(depth, nj),
        in_specs=[pl.BlockSpec((b, d), lambda l, j: (0, 0)),
                  pl.BlockSpec((1, d, d), lambda l, j: (l, 0, j)),
                  pl.BlockSpec((1, 1, d), lambda l, j: (l, 0, j))],
        out_specs=pl.BlockSpec((1, b, d), lambda l, j: (l, 0, j)),
        out_shape=jax.ShapeDtypeStruct((depth, b, d6), F32),
        compiler_params=_cparams(("parallel", "parallel")),
    )(c, mod_w, mod_b.reshape(depth, 1, d6))


def _rope_kernel(pos_ref, fa_ref, ga_ref, fb_ref, gb_ref, ca_ref, sa_ref, cb_ref, sb_ref):
    pos = pos_ref[...]
    anga = pos * fa_ref[...]
    ca_ref[...] = jnp.cos(anga)
    sa_ref[...] = jnp.sin(anga) * ga_ref[...]
    angb = pos * fb_ref[...]
    cb_ref[...] = jnp.cos(angb)
    sb_ref[...] = jnp.sin(angb) * gb_ref[...]


def _rope_tables(positions):
    n = positions.size
    lane = np.arange(LANES)
    inv_a = ROPE_THETA ** (-jnp.arange(0, A_HEAD_DIM, 2, dtype=F32) / A_HEAD_DIM)
    inv_b = ROPE_THETA ** (-jnp.arange(0, B_ROPE, 2, dtype=F32) / B_ROPE)
    fa = inv_a[lane % (A_HEAD_DIM // 2)]
    ga = jnp.asarray(np.where(lane % A_HEAD_DIM < A_HEAD_DIM // 2, -1.0, 1.0), F32)
    in_rope = (lane >= _KR_LO) & (lane < _KR_HI)
    fb = jnp.where(in_rope, inv_b[(lane - _KR_LO) % (B_ROPE // 2)], 0.0)
    gb = jnp.asarray(np.where(in_rope, np.where(lane < _KR_LO + B_ROPE // 2, -1.0, 1.0), 0.0), F32)
    tm = 2048 if n % 2048 == 0 else TOKEN_TILE
    vec = lambda v: v.reshape(1, LANES).astype(F32)
    tab = jax.ShapeDtypeStruct((n, LANES), F32)
    return pl.pallas_call(
        _rope_kernel,
        grid=(n // tm,),
        in_specs=[pl.BlockSpec((tm, 1), lambda i: (i, 0))] + [_const_spec((1, LANES))] * 4,
        out_specs=[pl.BlockSpec((tm, LANES), lambda i: (i, 0))] * 4,
        out_shape=[tab] * 4,
        compiler_params=_cparams(("parallel",)),
    )(positions.reshape(n, 1).astype(F32), vec(fa), vec(ga), vec(fb), vec(gb))


def _rope_a(x, c, s):
    lane = _lane_iota(x.shape)
    partner = jnp.where(lane % A_HEAD_DIM < A_HEAD_DIM // 2,
                        pltpu.roll(x, LANES - A_HEAD_DIM // 2, 1), pltpu.roll(x, A_HEAD_DIM // 2, 1))
    return x * c + partner * s


def _rope_b(x, c, s):
    lane = _lane_iota(x.shape)
    partner = jnp.where(lane < _KR_LO + B_ROPE // 2,
                        pltpu.roll(x, LANES - B_ROPE // 2, 1), pltpu.roll(x, B_ROPE // 2, 1))
    return x * c + partner * s


def _even_proj_kernel(x_ref, mod_ref, win_ref, ca_ref, sa_ref, cb_ref, sb_ref, ikg_ref, ikb_ref,
                      qg_ref, kvg_ref, wuq_ref, wkn_ref, wv_ref,
                      qa_ref, iq_ref, ka0_ref, ka1_ref, va_ref, ik0_ref, ik1_ref, misc_ref,
                      qb_ref, kf_ref, vb_ref):
    sh = mod_ref[0, 0:1, :]
    sc = mod_ref[0, 1:2, :]
    h = (x_ref[...] * (1.0 + sc) + sh).astype(BF16)
    p = jnp.dot(h, win_ref[...], preferred_element_type=F32)
    ca, sa, cb, sb = ca_ref[...], sa_ref[...], cb_ref[...], sb_ref[...]
    lane = _lane_iota(ca.shape)
    low = lane < HALF

    def group(off):
        return p[:, off:off + LANES]

    for g in range(A_HEADS * A_HEAD_DIM // LANES):
        qa_ref[:, g * LANES:(g + 1) * LANES] = (
            _rope_a(group(_P_QA + g * LANES), ca, sa) * (A_HEAD_DIM ** -0.5)).astype(BF16)
    for g in range(IDX_HEADS * IDX_DIM // LANES):
        iq_ref[:, g * LANES:(g + 1) * LANES] = (
            _rope_a(group(_P_IQ + g * LANES), ca, sa) * (IDX_DIM ** -0.5)).astype(BF16)

    ka = _rope_a(group(_P_GK), ca, sa)
    ka0_ref[...] = jnp.where(low, ka, 0.0).astype(BF16)
    ka1_ref[...] = jnp.where(low, 0.0, ka).astype(BF16)
    va_ref[...] = group(_P_GV).astype(BF16)

    ik = group(_P_GI)
    mu = jnp.sum(jnp.where(low, ik, 0.0), axis=-1, keepdims=True) * (1.0 / IDX_DIM)
    d = ik - mu
    var = jnp.sum(jnp.where(low, d * d, 0.0), axis=-1, keepdims=True) * (1.0 / IDX_DIM)
    ik = _rope_a(d * lax.rsqrt(var + LN_EPS) * ikg_ref[...] + ikb_ref[...], ca, sa)
    ik0_ref[...] = jnp.where(low, ik, 0.0).astype(BF16)
    ik1_ref[...] = jnp.where(low, 0.0, ik).astype(BF16)

    g5 = _rope_b(group(_P_G5), cb, sb)
    misc_ref[...] = g5 * (IDX_HEADS ** -0.5)
    kr = jnp.where((lane >= _KR_LO) & (lane < _KR_HI), g5, 0.0)

    cq = p[:, _P_CQ:_P_CQ + B_Q_RANK]
    cqn = cq * lax.rsqrt(jnp.mean(cq * cq, axis=-1, keepdims=True) + RMS_EPS) * qg_ref[...]
    qb = jnp.dot(cqn.astype(BF16), wuq_ref[...], preferred_element_type=F32)
    ckv = p[:, _P_CKV:_P_CKV + B_KV_RANK]
    ckvn = (ckv * lax.rsqrt(jnp.mean(ckv * ckv, axis=-1, keepdims=True) + RMS_EPS) * kvg_ref[...]).astype(BF16)
    kn = jnp.dot(ckvn, wkn_ref[...], preferred_element_type=F32)
    scale_b = (B_NOPE + B_ROPE) ** -0.5
    for hd in range(B_HEADS):
        sl = slice(hd * LANES, (hd + 1) * LANES)
        qb_ref[:, sl] = (_rope_b(qb[:, sl], cb, sb) * scale_b).astype(BF16)
        kf_ref[:, sl] = (kn[:, sl] + kr).astype(BF16)
    vb_ref[...] = jnp.dot(ckvn, wv_ref[...], preferred_element_type=F32).astype(BF16)


def _even_weights(w_in, idx_k_g, idx_k_b, q_norm_g, kv_norm_g, w_uq, w_ukv):
    d = w_in.shape[0]
    qa, ka, va, iq, ik, iw, cq, ckv, kr = jnp.split(w_in, np.cumsum(EVEN_IN_SIZES)[:-1].tolist(), axis=1)
    z = lambda n: jnp.zeros((d, n), w_in.dtype)
    win = jnp.concatenate([qa, iq, cq, ckv, ka, ka, va, va, ik, ik,
                           z(_KR_LO), kr, iw, z(LANES - _IW_LO - IDX_HEADS)], axis=1).astype(BF16)
    assert win.shape[1] == _P_END
    pad_head = B_NOPE + B_ROPE
    wuq = jnp.pad(w_uq.reshape(B_Q_RANK, B_HEADS, pad_head), ((0, 0), (0, 0), (0, LANES - pad_head)))
    wuq = wuq.reshape(B_Q_RANK, B_HEADS * LANES).astype(BF16)
    wkv = w_ukv.reshape(B_KV_RANK, B_HEADS, B_NOPE + B_V)
    wkn = jnp.pad(wkv[:, :, :B_NOPE], ((0, 0), (0, 0), (0, LANES - B_NOPE)))
    wkn = wkn.reshape(B_KV_RANK, B_HEADS * LANES).astype(BF16)
    wv = wkv[:, :, B_NOPE:].reshape(B_KV_RANK, B_HEADS * B_V).astype(BF16)
    two = lambda v: jnp.concatenate([v, v]).reshape(1, LANES).astype(F32)
    return (win, two(idx_k_g), two(idx_k_b), q_norm_g.reshape(1, -1).astype(F32),
            kv_norm_g.reshape(1, -1).astype(F32), wuq, wkn, wv)


def _even_proj(x, mod_l, tables, weights, batch, seq):
    n, d = x.shape
    tm = min(TOKEN_TILE, seq)
    nt = seq // tm
    win, ikg, ikb, qg, kvg, wuq, wkn, wv = weights
    row = lambda w: pl.BlockSpec((tm, w), lambda b, t: (b * nt + t, 0))
    bf = lambda w: jax.ShapeDtypeStruct((n, w), BF16)
    out_widths = [A_HEADS * A_HEAD_DIM, IDX_HEADS * IDX_DIM, LANES, LANES, LANES, LANES, LANES]
    out_shape = [bf(w) for w in out_widths] + [jax.ShapeDtypeStruct((n, LANES), F32)] + \
                [bf(B_HEADS * LANES), bf(B_HEADS * LANES), bf(B_HEADS * B_V)]
    out_specs = [row(w) for w in out_widths] + [row(LANES)] + \
                [row(B_HEADS * LANES), row(B_HEADS * LANES), row(B_HEADS * B_V)]
    return pl.pallas_call(
        _even_proj_kernel,
        grid=(batch, nt),
        in_specs=[row(d), pl.BlockSpec((1, 6, d), lambda b, t: (b, 0, 0)), _const_spec(win.shape)]
                 + [row(LANES)] * 4
                 + [_const_spec(a.shape) for a in (ikg, ikb, qg, kvg, wuq, wkn, wv)],
        out_specs=out_specs,
        out_shape=out_shape,
        compiler_params=_cparams(("parallel", "parallel")),
    )(x, mod_l, win, *tables, ikg, ikb, qg, kvg, wuq, wkn, wv)


def _softmax_pv(logits, v):
    m = jnp.max(logits, axis=-1, keepdims=True)
    e = jnp.exp(logits - m)
    den = jnp.sum(e, axis=-1, keepdims=True)
    return jnp.dot(e.astype(BF16), v, preferred_element_type=F32) / den


def _dsa_block(c, tq, topk, idx_bits, iq_ref, misc_ref, qa_ref, ik0_ref, ik1_ref, ka0_ref, ka1_ref,
               va_ref, o_ref):
    sk = (c + 1) * tq
    t0 = c * tq
    int_min = jnp.int32(-2 ** 31)
    iw_t = misc_ref[...].T
    iq = iq_ref[...]
    score = jnp.zeros((sk, tq), F32)
    for pair in range(IDX_HEADS // 2):
        iq_pair = iq[:, pair * LANES:(pair + 1) * LANES]
        for half, ik_ref in enumerate((ik0_ref, ik1_ref)):
            hd = 2 * pair + half
            rel = lax.dot_general(ik_ref[0:sk, :], iq_pair, _NT, preferred_element_type=F32)
            score = score + jnp.maximum(rel, 0.0) * iw_t[_IW_LO + hd:_IW_LO + hd + 1, :]
    score = jnp.where(score == 0.0, 0.0, score)
    bits = lax.bitcast_convert_type(score, jnp.int32)
    key = bits ^ (lax.shift_right_arithmetic(bits, 31) & jnp.int32(0x7FFFFFFF))
    s_idx = lax.broadcasted_iota(jnp.int32, (sk, tq), 0)
    t_idx = lax.broadcasted_iota(jnp.int32, (sk, tq), 1) + t0
    adm = lax.shift_right_logical(s_idx, 6) <= lax.shift_right_logical(t_idx, 6)
    key = jnp.where(adm, key, int_min)
    kf = jnp.float32(topk)

    def count(mask):
        return jnp.sum(jnp.where(mask, 1.0, 0.0), axis=0, keepdims=True)

    def value_bit(i, cu):
        cand = cu | lax.shift_left(jnp.int32(1), 31 - i)
        return jnp.where(count(key >= (cand ^ int_min)) >= kf, cand, cu)

    thr = lax.fori_loop(0, 32, value_bit, jnp.zeros((1, tq), jnp.int32)) ^ int_min
    gt = key > thr
    n_gt = count(gt)
    rq = jnp.where(key == thr, jnp.int32(2 ** idx_bits - 1) - s_idx, -1)

    def index_bit(i, rc):
        cand = rc | lax.shift_left(jnp.int32(1), idx_bits - 1 - i)
        return jnp.where(n_gt + count(rq >= cand) >= kf, cand, rc)

    rc = lax.fori_loop(0, idx_bits, index_bit, jnp.zeros((1, tq), jnp.int32))
    keep = jnp.where(adm, jnp.where(gt, 1.0, jnp.where(rq >= rc, 1.0, 0.0)), 0.0)
    bias = jnp.where(keep.T > 0.5, 0.0, -jnp.inf)

    qa = qa_ref[...]
    n_pair = A_HEADS // 2
    q_stack = jnp.concatenate([qa[:, g * LANES:(g + 1) * LANES] for g in range(n_pair)], axis=0)
    v = va_ref[0:sk, :]
    lane = _lane_iota((tq, LANES))
    outs = []
    for ka_ref in (ka0_ref, ka1_ref):
        logits = lax.dot_general(q_stack, ka_ref[0:sk, :], _NT, preferred_element_type=F32)
        outs.append([_softmax_pv(logits[g * tq:(g + 1) * tq] + bias, v) for g in range(n_pair)])
    for g in range(n_pair):
        o_ref[:, g * LANES:(g + 1) * LANES] = jnp.where(lane < HALF, outs[0][g], outs[1][g]).astype(BF16)


def _dsa_kernel(tq, n_q, topk, idx_bits, *refs):
    i = pl.program_id(1)
    for c in range(n_q):
        @pl.when(i == c)
        def _(c=c):
            _dsa_block(c, tq, topk, idx_bits, *refs)


def _dsa_attention(iq, misc, qa, ik0, ik1, ka0, ka1, va, batch, seq):
    n = qa.shape[0]
    tq = min(Q_TILE, seq)
    n_q = seq // tq
    topk = min(TOPK_MAX, seq // 4)
    idx_bits = max(1, int(np.ceil(np.log2(seq))))
    qrow = lambda w: pl.BlockSpec((tq, w), lambda b, i: (b * n_q + i, 0))
    krow = pl.BlockSpec((seq, LANES), lambda b, i: (b, 0))
    return pl.pallas_call(
        functools.partial(_dsa_kernel, tq, n_q, topk, idx_bits),
        grid=(batch, n_q),
        in_specs=[qrow(iq.shape[1]), qrow(LANES), qrow(qa.shape[1])] + [krow] * 5,
        out_specs=qrow(qa.shape[1]),
        out_shape=jax.ShapeDtypeStruct((n, qa.shape[1]), BF16),
        compiler_params=_cparams(("parallel", "arbitrary")),
    )(iq, misc, qa, ik0, ik1, ka0, ka1, va)


def _mla_block(c, tq, q_ref, k_ref, v_ref, o_ref):
    sk = (c + 1) * tq
    t_idx = lax.broadcasted_iota(jnp.int32, (tq, sk), 0) + c * tq
    s_idx = lax.broadcasted_iota(jnp.int32, (tq, sk), 1)
    bias = jnp.where(lax.shift_right_logical(s_idx, 6) <= lax.shift_right_logical(t_idx, 6), 0.0, -jnp.inf)
    v = v_ref[0:sk, :]
    outs = []
    for hh in range(2):
        logits = lax.dot_general(q_ref[:, hh * LANES:(hh + 1) * LANES], k_ref[0:sk, hh * LANES:(hh + 1) * LANES],
                                 _NT, preferred_element_type=F32)
        outs.append(_softmax_pv(logits + bias, v))
    lane = _lane_iota((tq, LANES))
    o_ref[...] = jnp.where(lane < HALF, outs[0], outs[1]).astype(BF16)


def _mla_kernel(tq, n_q, *refs):
    i = pl.program_id(2)
    for c in range(n_q):
        @pl.when(i == c)
        def _(c=c):
            _mla_block(c, tq, *refs)


def _mla_attention(qb, kf, vb, batch, seq):
    n = qb.shape[0]
    tq = min(Q_TILE, seq)
    n_q = seq // tq
    n_pair = B_HEADS // 2
    return pl.pallas_call(
        functools.partial(_mla_kernel, tq, n_q),
        grid=(batch, n_pair, n_q),
        in_specs=[pl.BlockSpec((tq, 2 * LANES), lambda b, p, i: (b * n_q + i, p)),
                  pl.BlockSpec((seq, 2 * LANES), lambda b, p, i: (b, p)),
                  pl.BlockSpec((seq, LANES), lambda b, p, i: (b, p))],
        out_specs=pl.BlockSpec((tq, LANES), lambda b, p, i: (b * n_q + i, p)),
        out_shape=jax.ShapeDtypeStruct((n, B_HEADS * B_V), BF16),
        compiler_params=_cparams(("parallel", "parallel", "arbitrary")),
    )(qb, kf, vb)


def _odd_proj_kernel(x_ref, mod_ref, w_ref, o_ref):
    sh = mod_ref[0, 0:1, :]
    sc = mod_ref[0, 1:2, :]
    h = (x_ref[...] * (1.0 + sc) + sh).astype(BF16)
    p = jnp.dot(h, w_ref[...], preferred_element_type=F32)
    dq = C_HEADS * C_HEAD_DIM
    o_ref[:, 0:dq] = (p[:, 0:dq] * (C_HEAD_DIM ** -0.5)).astype(BF16)
    o_ref[:, dq:] = p[:, dq:].astype(BF16)


def _odd_proj(x, mod_l, w_qkv, batch, seq):
    n, d = x.shape
    tm = min(TOKEN_TILE, seq)
    nt = seq // tm
    w = w_qkv.astype(BF16)
    return pl.pallas_call(
        _odd_proj_kernel,
        grid=(batch, nt),
        in_specs=[pl.BlockSpec((tm, d), lambda b, t: (b * nt + t, 0)),
                  pl.BlockSpec((1, 6, d), lambda b, t: (b, 0, 0)), _const_spec(w.shape)],
        out_specs=pl.BlockSpec((tm, w.shape[1]), lambda b, t: (b * nt + t, 0)),
        out_shape=jax.ShapeDtypeStruct((n, w.shape[1]), BF16),
        compiler_params=_cparams(("parallel", "parallel")),
    )(x, mod_l, w)


def _stick_kernel(tq, q_ref, k_ref, v_ref, o_ref):
    i = pl.program_id(2)
    q2 = q_ref[...]
    lane = _lane_iota((tq, LANES))
    row = lax.broadcasted_iota(jnp.int32, (tq, tq), 0)
    col = lax.broadcasted_iota(jnp.int32, (tq, tq), 1)
    later = jnp.where(row > col, 1.0, 0.0).astype(BF16)
    before = col < row

    def block(j, acc, carry, qh, diagonal):
        start = pl.multiple_of(j * tq, tq)
        kj = k_ref[pl.ds(start, tq), :]
        vj = v_ref[pl.ds(start, tq), :]
        z = lax.dot_general(qh, kj, _NT, preferred_element_type=F32)
        soft = jnp.log(1.0 + jnp.exp(-jnp.abs(z)))
        sp = jnp.maximum(z, 0.0) + soft
        log_beta = jnp.minimum(z, 0.0) - soft
        if diagonal:
            sp = jnp.where(before, sp, 0.0)
        hi = sp.astype(BF16)
        lo = (sp - hi.astype(F32)).astype(BF16)
        tail = (jnp.dot(hi, later, preferred_element_type=F32)
                + jnp.dot(lo, later, preferred_element_type=F32))
        a = jnp.exp(log_beta - tail - carry)
        if diagonal:
            a = jnp.where(before, a, 0.0)
        acc = acc + jnp.dot(a.astype(BF16), vj, preferred_element_type=F32)
        return acc, carry + jnp.sum(sp, axis=-1, keepdims=True)

    outs = []
    for hh in range(2):
        qh = jnp.where((lane < HALF) if hh == 0 else (lane >= HALF), q2, jnp.zeros_like(q2))
        acc, carry = block(i, jnp.zeros((tq, LANES), F32), jnp.zeros((tq, 1), F32), qh, True)

        def body(step, state, qh=qh):
            return block(i - 1 - step, state[0], state[1], qh, False)

        acc, _ = lax.fori_loop(0, i, body, (acc, carry))
        outs.append(acc)
    o_ref[...] = jnp.where(lane < HALF, outs[0], outs[1]).astype(BF16)


def _stick_attention(qkv, batch, seq):
    n = qkv.shape[0]
    tq = min(Q_TILE, seq)
    n_q = seq // tq
    n_pair = C_HEADS // 2
    return pl.pallas_call(
        functools.partial(_stick_kernel, tq),
        grid=(batch, n_pair, n_q),
        in_specs=[pl.BlockSpec((tq, LANES), lambda b, p, i: (b * n_q + i, p)),
                  pl.BlockSpec((seq, LANES), lambda b, p, i: (b, n_pair + p)),
                  pl.BlockSpec((seq, LANES), lambda b, p, i: (b, 2 * n_pair + p))],
        out_specs=pl.BlockSpec((tq, LANES), lambda b, p, i: (b * n_q + i, p)),
        out_shape=jax.ShapeDtypeStruct((n, C_HEADS * C_HEAD_DIM), BF16),
        compiler_params=_cparams(("parallel", "parallel", "arbitrary")),
    )(qkv, qkv, qkv)


def _out_norm_kernel(n_in, *refs):
    x_ref, mod_ref = refs[0], refs[1]
    ins = refs[2:2 + n_in]
    ws = refs[2 + n_in:2 + 2 * n_in]
    g_ref, b_ref, o_ref = refs[2 + 2 * n_in:]
    y = jnp.dot(ins[0][...], ws[0][...], preferred_element_type=F32)
    for a_ref, w_ref in zip(ins[1:], ws[1:]):
        y = y + jnp.dot(a_ref[...], w_ref[...], preferred_element_type=F32)
    gate = mod_ref[0, 2:3, :]
    o_ref[...] = _layer_norm(DEEPNORM_ALPHA * x_ref[...] + (1.0 + gate) * y, g_ref[...], b_ref[...])


def _out_norm(x, mod_l, acts, w_out, ln_g, ln_b, batch, seq):
    n, d = x.shape
    tm = min(TOKEN_TILE, seq)
    nt = seq // tm
    row = lambda w: pl.BlockSpec((tm, w), lambda b, t: (b * nt + t, 0))
    ws, off = [], 0
    for a in acts:
        ws.append(w_out[off:off + a.shape[1]].astype(BF16))
        off += a.shape[1]
    return pl.pallas_call(
        functools.partial(_out_norm_kernel, len(acts)),
        grid=(batch, nt),
        in_specs=[row(d), pl.BlockSpec((1, 6, d), lambda b, t: (b, 0, 0))]
                 + [row(a.shape[1]) for a in acts] + [_const_spec(w.shape) for w in ws]
                 + [_const_spec((1, d))] * 2,
        out_specs=row(d),
        out_shape=jax.ShapeDtypeStruct((n, d), F32),
        compiler_params=_cparams(("parallel", "parallel")),
    )(x, mod_l, *acts, *ws, ln_g.reshape(1, d), ln_b.reshape(1, d))


_HALO = 8


def _ffn_kernel(tm, x_ref, mod_ref, wup_ref, cw_ref, cb_ref, wdn_ref, g_ref, b_ref, o_ref, a_buf, tail_buf):
    t = pl.program_id(1)
    x = x_ref[...]
    sh = mod_ref[0, 3:4, :]
    sc = mod_ref[0, 4:5, :]
    gate = mod_ref[0, 5:6, :]
    h = (x * (1.0 + sc) + sh).astype(BF16)

    @pl.when(t == 0)
    def _():
        tail_buf[...] = jnp.zeros_like(tail_buf)

    y = jnp.zeros((tm, x.shape[1]), F32)
    for ci in range(D_FF // FF_CHUNK):
        cs = slice(ci * FF_CHUNK, (ci + 1) * FF_CHUNK)
        a = jnp.dot(h, wup_ref[:, cs], preferred_element_type=F32)
        gt = jnp.dot(h, wup_ref[:, D_FF + ci * FF_CHUNK:D_FF + (ci + 1) * FF_CHUNK], preferred_element_type=F32)
        a_buf[0:_HALO, :] = tail_buf[:, cs]
        a_buf[_HALO:, :] = a
        tail_buf[:, cs] = a[tm - _HALO:, :]
        conv = (a_buf[_HALO - 2:_HALO - 2 + tm, :] * cw_ref[0:1, cs]
                + a_buf[_HALO - 1:_HALO - 1 + tm, :] * cw_ref[1:2, cs]
                + a * cw_ref[2:3, cs] + cb_ref[:, cs])
        u = conv / (1.0 + jnp.exp(-conv)) * gt
        y = y + jnp.dot(u.astype(BF16), wdn_ref[cs, :], preferred_element_type=F32)
    o_ref[...] = _layer_norm(DEEPNORM_ALPHA * x + (1.0 + gate) * y, g_ref[...], b_ref[...])


def _conv_ffn(x, mod_l, w_up, conv_w, conv_b, w_down, ln_g, ln_b, batch, seq):
    n, d = x.shape
    tm = min(TOKEN_TILE, seq)
    nt = seq // tm
    row = pl.BlockSpec((tm, d), lambda b, t: (b * nt + t, 0))
    single = lambda shape: pl.BlockSpec(shape, lambda b, t: (0,) * len(shape), pipeline_mode=pl.Buffered(1))
    return pl.pallas_call(
        functools.partial(_ffn_kernel, tm),
        grid=(batch, nt),
        in_specs=[row, pl.BlockSpec((1, 6, d), lambda b, t: (b, 0, 0)), single((d, 2 * D_FF)),
                  _const_spec((CONV_WIDTH, D_FF)), _const_spec((1, D_FF)), single((D_FF, d)),
                  _const_spec((1, d)), _const_spec((1, d))],
        out_specs=row,
        out_shape=jax.ShapeDtypeStruct((n, d), F32),
        scratch_shapes=[pltpu.VMEM((_HALO + tm, FF_CHUNK), F32), pltpu.VMEM((_HALO, D_FF), F32)],
        compiler_params=_cparams(("parallel", "arbitrary")),
    )(x, mod_l, w_up.astype(BF16), conv_w, conv_b.reshape(1, D_FF), w_down.astype(BF16),
      ln_g.reshape(1, d), ln_b.reshape(1, d))


def kernel(x, c, positions, mod_w, mod_b, ln_mix_g, ln_mix_b, ln_ffn_g, ln_ffn_b, ev_w_in, ev_idx_k_g, ev_idx_k_b, ev_q_norm_g, ev_kv_norm_g, ev_w_uq, ev_w_ukv, ev_w_out, od_w_qkv, od_w_out, ffn_w_up, ffn_conv_w, ffn_conv_b, ffn_w_down):
    batch, seq, d = x.shape
    assert d == D_MODEL and seq % min(Q_TILE, seq) == 0 and seq % CHUNK == 0
    n = batch * seq
    depth = mod_w.shape[0]
    mod = _modulation(c, mod_w, mod_b).reshape(depth, batch, 6, d)
    tables = _rope_tables(positions)
    xs = x.reshape(n, d)
    for l in range(depth):
        i = l // 2
        if l % 2 == 0:
            weights = _even_weights(ev_w_in[i], ev_idx_k_g[i], ev_idx_k_b[i], ev_q_norm_g[i],
                                    ev_kv_norm_g[i], ev_w_uq[i], ev_w_ukv[i])
            qa, iq, ka0, ka1, va, ik0, ik1, misc, qb, kf, vb = _even_proj(xs, mod[l], tables, weights, batch, seq)
            out_a = _dsa_attention(iq, misc, qa, ik0, ik1, ka0, ka1, va, batch, seq)
            out_b = _mla_attention(qb, kf, vb, batch, seq)
            xs = _out_norm(xs, mod[l], [out_a, out_b], ev_w_out[i], ln_mix_g[l], ln_mix_b[l], batch, seq)
        else:
            qkv = _odd_proj(xs, mod[l], od_w_qkv[i], batch, seq)
            out_c = _stick_attention(qkv, batch, seq)
            xs = _out_norm(xs, mod[l], [out_c], od_w_out[i], ln_mix_g[l], ln_mix_b[l], batch, seq)
        xs = _conv_ffn(xs, mod[l], ffn_w_up[l], ffn_conv_w[l], ffn_conv_b[l], ffn_w_down[l],
                       ln_ffn_g[l], ln_ffn_b[l], batch, seq)
    return xs.reshape(batch, seq, d)
```

```python
import functools

import numpy as np
import jax
import jax.numpy as jnp
from jax import lax
from jax.experimental import pallas as pl
from jax.experimental.pallas import tpu as pltpu

F32 = jnp.float32
BF16 = jnp.bfloat16

D_MODEL = 1024
DEPTH = 4
CHUNK = 64
ROPE_THETA = 10000.0
LN_EPS = 1e-5
RMS_EPS = 1e-6

A_HEADS = 8
A_HEAD_DIM = 64
IDX_HEADS = 4
IDX_DIM = 64
TOPK_MAX = 256

B_HEADS = 8
B_NOPE = 64
B_ROPE = 32
B_V = 64
B_Q_RANK = 384
B_KV_RANK = 256

C_HEADS = 16
C_HEAD_DIM = D_MODEL // C_HEADS

D_FF = 2816
CONV_WIDTH = 3

EVEN_IN_SIZES = (A_HEADS * A_HEAD_DIM, A_HEAD_DIM, A_HEAD_DIM, IDX_HEADS * IDX_DIM, IDX_DIM, IDX_HEADS,
                 B_Q_RANK, B_KV_RANK, B_ROPE)
DEEPNORM_ALPHA = (2 * DEPTH) ** 0.25

LANES = 128
HALF = 64
TOKEN_TILE = 512
Q_TILE = 256
FF_CHUNK = 256
VMEM_LIMIT = 56 * 2 ** 20

_P_QA = 0
_P_IQ = _P_QA + A_HEADS * A_HEAD_DIM
_P_CQ = _P_IQ + IDX_HEADS * IDX_DIM
_P_CKV = _P_CQ + B_Q_RANK
_P_GK = _P_CKV + B_KV_RANK
_P_GV = _P_GK + LANES
_P_GI = _P_GV + LANES
_P_G5 = _P_GI + LANES
_P_END = _P_G5 + LANES
_KR_LO = HALF
_KR_HI = HALF + B_ROPE
_IW_LO = _KR_HI

_NT = (((1,), (1,)), ((), ()))


def _cparams(sem):
    return pltpu.CompilerParams(dimension_semantics=sem, vmem_limit_bytes=VMEM_LIMIT)


def _const_spec(shape):
    nd = len(shape)
    return pl.BlockSpec(shape, lambda *_: (0,) * nd)


def _lane_iota(shape):
    return lax.broadcasted_iota(jnp.int32, shape, len(shape) - 1)


def _layer_norm(v, g, b):
    mu = jnp.mean(v, axis=-1, keepdims=True)
    d = v - mu
    var = jnp.mean(d * d, axis=-1, keepdims=True)
    return d * lax.rsqrt(var + LN_EPS) * g + b


def _mod_kernel(c_ref, w_ref, b_ref, o_ref):
    c = c_ref[...]
    ca = c / (1.0 + jnp.exp(-c))
    o_ref[0] = jnp.dot(ca, w_ref[0], preferred_element_type=F32,
                       precision=lax.Precision.HIGHEST) + b_ref[0]


def _modulation(c, mod_w, mod_b):
    depth, d, d6 = mod_w.shape
    b = c.shape[0]
    nj = d6 // d
    return pl.pallas_call(
        _mod_kernel,
        grid=(depth, nj),
        in_specs=[pl.BlockSpec((b, d), lambda l, j: (0, 0)),
                  pl.BlockSpec((1, d, d), lambda l, j: (l, 0, j)),
                  pl.BlockSpec((1, 1, d), lambda l, j: (l, 0, j))],
        out_specs=pl.BlockSpec((1, b, d), lambda l, j: (l, 0, j)),
        out_shape=jax.ShapeDtypeStruct((depth, b, d6), F32),
        compiler_params=_cparams(("parallel", "parallel")),
    )(c, mod_w, mod_b.reshape(depth, 1, d6))


def _rope_kernel(pos_ref, fa_ref, ga_ref, fb_ref, gb_ref, ca_ref, sa_ref, cb_ref, sb_ref):
    pos = pos_ref[...]
    anga = pos * fa_ref[...]
    ca_ref[...] = jnp.cos(anga)
    sa_ref[...] = jnp.sin(anga) * ga_ref[...]
    angb = pos * fb_ref[...]
    cb_ref[...] = jnp.cos(angb)
    sb_ref[...] = jnp.sin(angb) * gb_ref[...]


def _rope_tables(positions):
    n = positions.size
    lane = np.arange(LANES)
    inv_a = ROPE_THETA ** (-jnp.arange(0, A_HEAD_DIM, 2, dtype=F32) / A_HEAD_DIM)
    inv_b = ROPE_THETA ** (-jnp.arange(0, B_ROPE, 2, dtype=F32) / B_ROPE)
    fa = inv_a[lane % (A_HEAD_DIM // 2)]
    ga = jnp.asarray(np.where(lane % A_HEAD_DIM < A_HEAD_DIM // 2, -1.0, 1.0), F32)
    in_rope = (lane >= _KR_LO) & (lane < _KR_HI)
    fb = jnp.where(in_rope, inv_b[(lane - _KR_LO) % (B_ROPE // 2)], 0.0)
    gb = jnp.asarray(np.where(in_rope, np.where(lane < _KR_LO + B_ROPE // 2, -1.0, 1.0), 0.0), F32)
    tm = 2048 if n % 2048 == 0 else TOKEN_TILE
    vec = lambda v: v.reshape(1, LANES).astype(F32)
    tab = jax.ShapeDtypeStruct((n, LANES), F32)
    return pl.pallas_call(
        _rope_kernel,
        grid=(n // tm,),
        in_specs=[pl.BlockSpec((tm, 1), lambda i: (i, 0))] + [_const_spec((1, LANES))] * 4,
        out_specs=[pl.BlockSpec((tm, LANES), lambda i: (i, 0))] * 4,
        out_shape=[tab] * 4,
        compiler_params=_cparams(("parallel",)),
    )(positions.reshape(n, 1).astype(F32), vec(fa), vec(ga), vec(fb), vec(gb))


def _rope_a(x, c, s):
    lane = _lane_iota(x.shape)
    partner = jnp.where(lane % A_HEAD_DIM < A_HEAD_DIM // 2,
                        pltpu.roll(x, LANES - A_HEAD_DIM // 2, 1), pltpu.roll(x, A_HEAD_DIM // 2, 1))
    return x * c + partner * s


def _rope_b(x, c, s):
    lane = _lane_iota(x.shape)
    partner = jnp.where(lane < _KR_LO + B_ROPE // 2,
                        pltpu.roll(x, LANES - B_ROPE // 2, 1), pltpu.roll(x, B_ROPE // 2, 1))
    return x * c + partner * s


def _even_proj_kernel(x_ref, mod_ref, win_ref, ca_ref, sa_ref, cb_ref, sb_ref, ikg_ref, ikb_ref,
                      qg_ref, kvg_ref, wuq_ref, wkn_ref, wv_ref,
                      qa_ref, iq_ref, ka0_ref, ka1_ref, va_ref, ik0_ref, ik1_ref, misc_ref,
                      qb_ref, kf_ref, vb_ref):
    sh = mod_ref[0, 0:1, :]
    sc = mod_ref[0, 1:2, :]
    h = (x_ref[...] * (1.0 + sc) + sh).astype(BF16)
    p = jnp.dot(h, win_ref[...], preferred_element_type=F32)
    ca, sa, cb, sb = ca_ref[...], sa_ref[...], cb_ref[...], sb_ref[...]
    lane = _lane_iota(ca.shape)
    low = lane < HALF

    def group(off):
        return p[:, off:off + LANES]

    for g in range(A_HEADS * A_HEAD_DIM // LANES):
        qa_ref[:, g * LANES:(g + 1) * LANES] = (
            _rope_a(group(_P_QA + g * LANES), ca, sa) * (A_HEAD_DIM ** -0.5)).astype(BF16)
    for g in range(IDX_HEADS * IDX_DIM // LANES):
        iq_ref[:, g * LANES:(g + 1) * LANES] = (
            _rope_a(group(_P_IQ + g * LANES), ca, sa) * (IDX_DIM ** -0.5)).astype(BF16)

    ka = _rope_a(group(_P_GK), ca, sa)
    ka0_ref[...] = jnp.where(low, ka, 0.0).astype(BF16)
    ka1_ref[...] = jnp.where(low, 0.0, ka).astype(BF16)
    va_ref[...] = group(_P_GV).astype(BF16)

    ik = group(_P_GI)
    mu = jnp.sum(jnp.where(low, ik, 0.0), axis=-1, keepdims=True) * (1.0 / IDX_DIM)
    d = ik - mu
    var = jnp.sum(jnp.where(low, d * d, 0.0), axis=-1, keepdims=True) * (1.0 / IDX_DIM)
    ik = _rope_a(d * lax.rsqrt(var + LN_EPS) * ikg_ref[...] + ikb_ref[...], ca, sa)
    ik0_ref[...] = jnp.where(low, ik, 0.0).astype(BF16)
    ik1_ref[...] = jnp.where(low, 0.0, ik).astype(BF16)

    g5 = _rope_b(group(_P_G5), cb, sb)
    misc_ref[...] = g5 * (IDX_HEADS ** -0.5)
    kr = jnp.where((lane >= _KR_LO) & (lane < _KR_HI), g5, 0.0)

    cq = p[:, _P_CQ:_P_CQ + B_Q_RANK]
    cqn = cq * lax.rsqrt(jnp.mean(cq * cq, axis=-1, keepdims=True) + RMS_EPS) * qg_ref[...]
    qb = jnp.dot(cqn.astype(BF16), wuq_ref[...], preferred_element_type=F32)
    ckv = p[:, _P_CKV:_P_CKV + B_KV_RANK]
    ckvn = (ckv * lax.rsqrt(jnp.mean(ckv * ckv, axis=-1, keepdims=True) + RMS_EPS) * kvg_ref[...]).astype(BF16)
    kn = jnp.dot(ckvn, wkn_ref[...], preferred_element_type=F32)
    scale_b = (B_NOPE + B_ROPE) ** -0.5
    for hd in range(B_HEADS):
        sl = slice(hd * LANES, (hd + 1) * LANES)
        qb_ref[:, sl] = (_rope_b(qb[:, sl], cb, sb) * scale_b).astype(BF16)
        kf_ref[:, sl] = (kn[:, sl] + kr).astype(BF16)
    vb_ref[...] = jnp.dot(ckvn, wv_ref[...], preferred_element_type=F32).astype(BF16)


def _even_weights(w_in, idx_k_g, idx_k_b, q_norm_g, kv_norm_g, w_uq, w_ukv):
    d = w_in.shape[0]
    qa, ka, va, iq, ik, iw, cq, ckv, kr = jnp.split(w_in, np.cumsum(EVEN_IN_SIZES)[:-1].tolist(), axis=1)
    z = lambda n: jnp.zeros((d, n), w_in.dtype)
    win = jnp.concatenate([qa, iq, cq, ckv, ka, ka, va, va, ik, ik,
                           z(_KR_LO), kr, iw, z(LANES - _IW_LO - IDX_HEADS)], axis=1).astype(BF16)
    assert win.shape[1] == _P_END
    pad_head = B_NOPE + B_ROPE
    wuq = jnp.pad(w_uq.reshape(B_Q_RANK, B_HEADS, pad_head), ((0, 0), (0, 0), (0, LANES - pad_head)))
    wuq = wuq.reshape(B_Q_RANK, B_HEADS * LANES).astype(BF16)
    wkv = w_ukv.reshape(B_KV_RANK, B_HEADS, B_NOPE + B_V)
    wkn = jnp.pad(wkv[:, :, :B_NOPE], ((0, 0), (0, 0), (0, LANES - B_NOPE)))
    wkn = wkn.reshape(B_KV_RANK, B_HEADS * LANES).astype(BF16)
    wv = wkv[:, :, B_NOPE:].reshape(B_KV_RANK, B_HEADS * B_V).astype(BF16)
    two = lambda v: jnp.concatenate([v, v]).reshape(1, LANES).astype(F32)
    return (win, two(idx_k_g), two(idx_k_b), q_norm_g.reshape(1, -1).astype(F32),
            kv_norm_g.reshape(1, -1).astype(F32), wuq, wkn, wv)


def _even_proj(x, mod_l, tables, weights, batch, seq):
    n, d = x.shape
    tm = min(TOKEN_TILE, seq)
    nt = seq // tm
    win, ikg, ikb, qg, kvg, wuq, wkn, wv = weights
    row = lambda w: pl.BlockSpec((tm, w), lambda b, t: (b * nt + t, 0))
    bf = lambda w: jax.ShapeDtypeStruct((n, w), BF16)
    out_widths = [A_HEADS * A_HEAD_DIM, IDX_HEADS * IDX_DIM, LANES, LANES, LANES, LANES, LANES]
    out_shape = [bf(w) for w in out_widths] + [jax.ShapeDtypeStruct((n, LANES), F32)] + \
                [bf(B_HEADS * LANES), bf(B_HEADS * LANES), bf(B_HEADS * B_V)]
    out_specs = [row(w) for w in out_widths] + [row(LANES)] + \
                [row(B_HEADS * LANES), row(B_HEADS * LANES), row(B_HEADS * B_V)]
    return pl.pallas_call(
        _even_proj_kernel,
        grid=(batch, nt),
        in_specs=[row(d), pl.BlockSpec((1, 6, d), lambda b, t: (b, 0, 0)), _const_spec(win.shape)]
                 + [row(LANES)] * 4
                 + [_const_spec(a.shape) for a in (ikg, ikb, qg, kvg, wuq, wkn, wv)],
        out_specs=out_specs,
        out_shape=out_shape,
        compiler_params=_cparams(("parallel", "parallel")),
    )(x, mod_l, win, *tables, ikg, ikb, qg, kvg, wuq, wkn, wv)


def _softmax_pv(logits, v):
    m = jnp.max(logits, axis=-1, keepdims=True)
    e = jnp.exp(logits - m)
    den = jnp.sum(e, axis=-1, keepdims=True)
    return jnp.dot(e.astype(BF16), v, preferred_element_type=F32) / den


def _dsa_kernel(tq, topk, idx_bits, iq_ref, misc_ref, qa_ref, ik0_ref, ik1_ref, ka0_ref, ka1_ref, va_ref,
                o_ref, key_buf, bias_buf, m_buf, l_buf, acc_buf):
    i = pl.program_id(1)
    n_blk = i + 1
    int_min = jnp.int32(-2 ** 31)
    kf = jnp.float32(topk)
    n_pair = A_HEADS // 2
    rows = n_pair * tq

    def rows_of(j):
        return pl.ds(pl.multiple_of(j * tq, tq), tq)

    iw_t = misc_ref[...].T
    iq = iq_ref[...]
    t_idx = lax.broadcasted_iota(jnp.int32, (tq, tq), 1) + i * tq

    def score_block(j, _):
        score = jnp.zeros((tq, tq), F32)
        for pair in range(IDX_HEADS // 2):
            iq_pair = iq[:, pair * LANES:(pair + 1) * LANES]
            for half, ik_ref in enumerate((ik0_ref, ik1_ref)):
                hd = 2 * pair + half
                rel = lax.dot_general(ik_ref[rows_of(j), :], iq_pair, _NT, preferred_element_type=F32)
                score = score + jnp.maximum(rel, 0.0) * iw_t[_IW_LO + hd:_IW_LO + hd + 1, :]
        score = jnp.where(score == 0.0, 0.0, score)
        bits = lax.bitcast_convert_type(score, jnp.int32)
        key = bits ^ (lax.shift_right_arithmetic(bits, 31) & jnp.int32(0x7FFFFFFF))
        s_idx = lax.broadcasted_iota(jnp.int32, (tq, tq), 0) + j * tq
        adm = lax.shift_right_logical(s_idx, 6) <= lax.shift_right_logical(t_idx, 6)
        key_buf[rows_of(j), :] = jnp.where(adm, key, int_min)
        return 0

    lax.fori_loop(0, n_blk, score_block, 0)

    def count(pred):
        def body(j, acc):
            hit = jnp.where(pred(key_buf[rows_of(j), :], j * tq), 1.0, 0.0)
            return acc + jnp.sum(hit.reshape(tq // 8, 8, tq), axis=0)
        acc = lax.fori_loop(0, n_blk, body, jnp.zeros((8, tq), F32))
        return jnp.sum(acc, axis=0, keepdims=True)

    def value_bit(b, cu):
        cand = cu | lax.shift_left(jnp.int32(1), 31 - b)
        candk = cand ^ int_min
        return jnp.where(count(lambda k, _: k >= candk) >= kf, cand, cu)

    thr = lax.fori_loop(0, 32, value_bit, jnp.zeros((1, tq), jnp.int32)) ^ int_min
    n_gt = count(lambda k, _: k > thr)
    top_idx = jnp.int32(2 ** idx_bits - 1)

    def rev_idx(k, base):
        s_idx = lax.broadcasted_iota(jnp.int32, (tq, tq), 0) + base
        return jnp.where(k == thr, top_idx - s_idx, -1)

    def index_bit(b, rc):
        cand = rc | lax.shift_left(jnp.int32(1), idx_bits - 1 - b)
        return jnp.where(n_gt + count(lambda k, base: rev_idx(k, base) >= cand) >= kf, cand, rc)

    rc = lax.fori_loop(0, idx_bits, index_bit, jnp.zeros((1, tq), jnp.int32))

    def bias_block(j, _):
        k = key_buf[rows_of(j), :]
        keep = jnp.where(k == int_min, 0.0, jnp.where(k > thr, 1.0, jnp.where(rev_idx(k, j * tq) >= rc, 1.0, 0.0)))
        bias_buf[j] = jnp.where(keep.T > 0.5, 0.0, -jnp.inf)
        return 0

    lax.fori_loop(0, n_blk, bias_block, 0)

    qa = qa_ref[...]
    q_stack = jnp.concatenate([qa[:, g * LANES:(g + 1) * LANES] for g in range(n_pair)], axis=0)

    def logits_block(j, ka_ref):
        lg = lax.dot_general(q_stack, ka_ref[rows_of(j), :], _NT, preferred_element_type=F32)
        bias = bias_buf[j]
        return lg + jnp.concatenate([bias] * n_pair, axis=0)

    m_buf[...] = jnp.full(m_buf.shape, -jnp.inf, F32)

    def max_block(j, _):
        for half, ka_ref in enumerate((ka0_ref, ka1_ref)):
            sl = slice(half * rows, (half + 1) * rows)
            lg = logits_block(j, ka_ref)
            folded = jnp.maximum(lg[:, :LANES], lg[:, LANES:]) if tq == 2 * LANES else lg
            m_buf[sl, :] = jnp.maximum(m_buf[sl, :], folded)
        return 0

    lax.fori_loop(0, n_blk, max_block, 0)
    m_buf[...] = jnp.broadcast_to(jnp.max(m_buf[...], axis=-1, keepdims=True), m_buf.shape)
    l_buf[...] = jnp.zeros(l_buf.shape, F32)
    acc_buf[...] = jnp.zeros(acc_buf.shape, F32)

    def pv_block(j, _):
        v = va_ref[rows_of(j), :]
        for half, ka_ref in enumerate((ka0_ref, ka1_ref)):
            sl = slice(half * rows, (half + 1) * rows)
            m = m_buf[sl, :]
            e = jnp.exp(logits_block(j, ka_ref) - jnp.concatenate([m] * (tq // LANES), axis=1))
            l_buf[sl, :] = l_buf[sl, :] + (e[:, :LANES] + e[:, LANES:] if tq == 2 * LANES else e)
            acc_buf[sl, :] = acc_buf[sl, :] + jnp.dot(e.astype(BF16), v, preferred_element_type=F32)
        return 0

    lax.fori_loop(0, n_blk, pv_block, 0)
    out = acc_buf[...] / jnp.sum(l_buf[...], axis=-1, keepdims=True)
    lane = _lane_iota((tq, LANES))
    for g in range(n_pair):
        o_ref[:, g * LANES:(g + 1) * LANES] = jnp.where(
            lane < HALF, out[g * tq:(g + 1) * tq], out[rows + g * tq:rows + (g + 1) * tq]).astype(BF16)


def _dsa_attention(iq, misc, qa, ik0, ik1, ka0, ka1, va, batch, seq):
    n = qa.shape[0]
    tq = min(Q_TILE, seq)
    assert tq % LANES == 0 and tq // LANES in (1, 2)
    n_q = seq // tq
    topk = min(TOPK_MAX, seq // 4)
    idx_bits = max(1, int(np.ceil(np.log2(seq))))
    rows = A_HEADS * tq
    qrow = lambda w: pl.BlockSpec((tq, w), lambda b, i: (b * n_q + i, 0))
    krow = pl.BlockSpec((seq, LANES), lambda b, i: (b, 0))
    return pl.pallas_call(
        functools.partial(_dsa_kernel, tq, topk, idx_bits),
        grid=(batch, n_q),
        in_specs=[qrow(iq.shape[1]), qrow(LANES), qrow(qa.shape[1])] + [krow] * 5,
        out_specs=qrow(qa.shape[1]),
        out_shape=jax.ShapeDtypeStruct((n, qa.shape[1]), BF16),
        scratch_shapes=[pltpu.VMEM((seq, tq), jnp.int32), pltpu.VMEM((n_q, tq, tq), F32),
                        pltpu.VMEM((rows, LANES), F32), pltpu.VMEM((rows, LANES), F32),
                        pltpu.VMEM((rows, LANES), F32)],
        compiler_params=_cparams(("parallel", "arbitrary")),
    )(iq, misc, qa, ik0, ik1, ka0, ka1, va)


def _mla_block(c, tq, q_ref, k_ref, v_ref, o_ref):
    sk = (c + 1) * tq
    t_idx = lax.broadcasted_iota(jnp.int32, (tq, sk), 0) + c * tq
    s_idx = lax.broadcasted_iota(jnp.int32, (tq, sk), 1)
    bias = jnp.where(lax.shift_right_logical(s_idx, 6) <= lax.shift_right_logical(t_idx, 6), 0.0, -jnp.inf)
    v = v_ref[0:sk, :]
    outs = []
    for hh in range(2):
        logits = lax.dot_general(q_ref[:, hh * LANES:(hh + 1) * LANES], k_ref[0:sk, hh * LANES:(hh + 1) * LANES],
                                 _NT, preferred_element_type=F32)
        outs.append(_softmax_pv(logits + bias, v))
    lane = _lane_iota((tq, LANES))
    o_ref[...] = jnp.where(lane < HALF, outs[0], outs[1]).astype(BF16)


def _mla_kernel(tq, n_q, *refs):
    i = pl.program_id(2)
    for c in range(n_q):
        @pl.when(i == c)
        def _(c=c):
            _mla_block(c, tq, *refs)


def _mla_attention(qb, kf, vb, batch, seq):
    n = qb.shape[0]
    tq = min(Q_TILE, seq)
    n_q = seq // tq
    n_pair = B_HEADS // 2
    return pl.pallas_call(
        functools.partial(_mla_kernel, tq, n_q),
        grid=(batch, n_pair, n_q),
        in_specs=[pl.BlockSpec((tq, 2 * LANES), lambda b, p, i: (b * n_q + i, p)),
                  pl.BlockSpec((seq, 2 * LANES), lambda b, p, i: (b, p)),
                  pl.BlockSpec((seq, LANES), lambda b, p, i: (b, p))],
        out_specs=pl.BlockSpec((tq, LANES), lambda b, p, i: (b * n_q + i, p)),
        out_shape=jax.ShapeDtypeStruct((n, B_HEADS * B_V), BF16),
        compiler_params=_cparams(("parallel", "parallel", "arbitrary")),
    )(qb, kf, vb)


def _odd_proj_kernel(x_ref, mod_ref, w_ref, o_ref):
    sh = mod_ref[0, 0:1, :]
    sc = mod_ref[0, 1:2, :]
    h = (x_ref[...] * (1.0 + sc) + sh).astype(BF16)
    p = jnp.dot(h, w_ref[...], preferred_element_type=F32)
    dq = C_HEADS * C_HEAD_DIM
    o_ref[:, 0:dq] = (p[:, 0:dq] * (C_HEAD_DIM ** -0.5)).astype(BF16)
    o_ref[:, dq:] = p[:, dq:].astype(BF16)


def _odd_proj(x, mod_l, w_qkv, batch, seq):
    n, d = x.shape
    tm = min(TOKEN_TILE, seq)
    nt = seq // tm
    w = w_qkv.astype(BF16)
    return pl.pallas_call(
        _odd_proj_kernel,
        grid=(batch, nt),
        in_specs=[pl.BlockSpec((tm, d), lambda b, t: (b * nt + t, 0)),
                  pl.BlockSpec((1, 6, d), lambda b, t: (b, 0, 0)), _const_spec(w.shape)],
        out_specs=pl.BlockSpec((tm, w.shape[1]), lambda b, t: (b * nt + t, 0)),
        out_shape=jax.ShapeDtypeStruct((n, w.shape[1]), BF16),
        compiler_params=_cparams(("parallel", "parallel")),
    )(x, mod_l, w)


def _stick_kernel(tq, q_ref, k_ref, v_ref, o_ref):
    i = pl.program_id(2)
    q2 = q_ref[...]
    lane = _lane_iota((tq, LANES))
    row = lax.broadcasted_iota(jnp.int32, (tq, tq), 0)
    col = lax.broadcasted_iota(jnp.int32, (tq, tq), 1)
    later_eq = jnp.where(row >= col, 1.0, 0.0).astype(BF16)
    before = col < row
    q_heads = (jnp.where(lane < HALF, q2, jnp.zeros_like(q2)), jnp.where(lane < HALF, jnp.zeros_like(q2), q2))

    def block(j, state, diagonal):
        start = pl.multiple_of(j * tq, tq)
        kj = k_ref[pl.ds(start, tq), :]
        vj = v_ref[pl.ds(start, tq), :]
        zs, parts = [], []
        for qh in q_heads:
            z = lax.dot_general(qh, kj, _NT, preferred_element_type=F32)
            sp = jnp.maximum(z, 0.0) + jnp.log(1.0 + jnp.exp(-jnp.abs(z)))
            if diagonal:
                sp = jnp.where(before, sp, 0.0)
            hi = sp.astype(BF16)
            parts += [hi, (sp - hi.astype(F32)).astype(BF16)]
            zs.append(z)
        incl = jnp.dot(jnp.concatenate(parts, axis=0), later_eq, preferred_element_type=F32)
        new_state = []
        for hh, z in enumerate(zs):
            acc, carry = state[hh]
            inc = incl[2 * hh * tq:(2 * hh + 1) * tq] + incl[(2 * hh + 1) * tq:(2 * hh + 2) * tq]
            a = jnp.exp(z - inc - carry)
            if diagonal:
                a = jnp.where(before, a, 0.0)
            acc = acc + jnp.dot(a.astype(BF16), vj, preferred_element_type=F32)
            new_state.append((acc, carry + inc[:, 0:1]))
        return tuple(new_state)

    zero = (jnp.zeros((tq, LANES), F32), jnp.zeros((tq, 1), F32))
    state = block(i, (zero, zero), True)
    state = lax.fori_loop(0, i, lambda step, st: block(i - 1 - step, st, False), state)
    o_ref[...] = jnp.where(lane < HALF, state[0][0], state[1][0]).astype(BF16)


def _stick_attention(qkv, batch, seq):
    n = qkv.shape[0]
    tq = min(Q_TILE, seq)
    n_q = seq // tq
    n_pair = C_HEADS // 2
    return pl.pallas_call(
        functools.partial(_stick_kernel, tq),
        grid=(batch, n_pair, n_q),
        in_specs=[pl.BlockSpec((tq, LANES), lambda b, p, i: (b * n_q + i, p)),
                  pl.BlockSpec((seq, LANES), lambda b, p, i: (b, n_pair + p)),
                  pl.BlockSpec((seq, LANES), lambda b, p, i: (b, 2 * n_pair + p))],
        out_specs=pl.BlockSpec((tq, LANES), lambda b, p, i: (b * n_q + i, p)),
        out_shape=jax.ShapeDtypeStruct((n, C_HEADS * C_HEAD_DIM), BF16),
        compiler_params=_cparams(("parallel", "parallel", "arbitrary")),
    )(qkv, qkv, qkv)


def _out_norm_kernel(n_in, *refs):
    x_ref, mod_ref = refs[0], refs[1]
    ins = refs[2:2 + n_in]
    ws = refs[2 + n_in:2 + 2 * n_in]
    g_ref, b_ref, o_ref = refs[2 + 2 * n_in:]
    y = jnp.dot(ins[0][...], ws[0][...], preferred_element_type=F32)
    for a_ref, w_ref in zip(ins[1:], ws[1:]):
        y = y + jnp.dot(a_ref[...], w_ref[...], preferred_element_type=F32)
    gate = mod_ref[0, 2:3, :]
    o_ref[...] = _layer_norm(DEEPNORM_ALPHA * x_ref[...] + (1.0 + gate) * y, g_ref[...], b_ref[...])


def _out_norm(x, mod_l, acts, w_out, ln_g, ln_b, batch, seq):
    n, d = x.shape
    tm = min(TOKEN_TILE, seq)
    nt = seq // tm
    row = lambda w: pl.BlockSpec((tm, w), lambda b, t: (b * nt + t, 0))
    ws, off = [], 0
    for a in acts:
        ws.append(w_out[off:off + a.shape[1]].astype(BF16))
        off += a.shape[1]
    return pl.pallas_call(
        functools.partial(_out_norm_kernel, len(acts)),
        grid=(batch, nt),
        in_specs=[row(d), pl.BlockSpec((1, 6, d), lambda b, t: (b, 0, 0))]
                 + [row(a.shape[1]) for a in acts] + [_const_spec(w.shape) for w in ws]
                 + [_const_spec((1, d))] * 2,
        out_specs=row(d),
        out_shape=jax.ShapeDtypeStruct((n, d), F32),
        compiler_params=_cparams(("parallel", "parallel")),
    )(x, mod_l, *acts, *ws, ln_g.reshape(1, d), ln_b.reshape(1, d))


_HALO = 8


def _ffn_kernel(tm, x_ref, mod_ref, wup_ref, cw_ref, cb_ref, wdn_ref, g_ref, b_ref, o_ref, a_buf, tail_buf):
    t = pl.program_id(1)
    x = x_ref[...]
    sh = mod_ref[0, 3:4, :]
    sc = mod_ref[0, 4:5, :]
    gate = mod_ref[0, 5:6, :]
    h = (x * (1.0 + sc) + sh).astype(BF16)

    @pl.when(t == 0)
    def _():
        tail_buf[...] = jnp.zeros_like(tail_buf)

    y = jnp.zeros((tm, x.shape[1]), F32)
    for ci in range(D_FF // FF_CHUNK):
        cs = slice(ci * FF_CHUNK, (ci + 1) * FF_CHUNK)
        a = jnp.dot(h, wup_ref[:, cs], preferred_element_type=F32)
        gt = jnp.dot(h, wup_ref[:, D_FF + ci * FF_CHUNK:D_FF + (ci + 1) * FF_CHUNK], preferred_element_type=F32)
        a_buf[0:_HALO, :] = tail_buf[:, cs]
        a_buf[_HALO:, :] = a
        tail_buf[:, cs] = a[tm - _HALO:, :]
        conv = (a_buf[_HALO - 2:_HALO - 2 + tm, :] * cw_ref[0:1, cs]
                + a_buf[_HALO - 1:_HALO - 1 + tm, :] * cw_ref[1:2, cs]
                + a * cw_ref[2:3, cs] + cb_ref[:, cs])
        u = conv / (1.0 + jnp.exp(-conv)) * gt
        y = y + jnp.dot(u.astype(BF16), wdn_ref[cs, :], preferred_element_type=F32)
    o_ref[...] = _layer_norm(DEEPNORM_ALPHA * x + (1.0 + gate) * y, g_ref[...], b_ref[...])


def _conv_ffn(x, mod_l, w_up, conv_w, conv_b, w_down, ln_g, ln_b, batch, seq):
    n, d = x.shape
    tm = min(TOKEN_TILE, seq)
    nt = seq // tm
    row = pl.BlockSpec((tm, d), lambda b, t: (b * nt + t, 0))
    single = lambda shape: pl.BlockSpec(shape, lambda b, t: (0,) * len(shape), pipeline_mode=pl.Buffered(1))
    return pl.pallas_call(
        functools.partial(_ffn_kernel, tm),
        grid=(batch, nt),
        in_specs=[row, pl.BlockSpec((1, 6, d), lambda b, t: (b, 0, 0)), single((d, 2 * D_FF)),
                  _const_spec((CONV_WIDTH, D_FF)), _const_spec((1, D_FF)), single((D_FF, d)),
                  _const_spec((1, d)), _const_spec((1, d))],
        out_specs=row,
        out_shape=jax.ShapeDtypeStruct((n, d), F32),
        scratch_shapes=[pltpu.VMEM((_HALO + tm, FF_CHUNK), F32), pltpu.VMEM((_HALO, D_FF), F32)],
        compiler_params=_cparams(("parallel", "arbitrary")),
    )(x, mod_l, w_up.astype(BF16), conv_w, conv_b.reshape(1, D_FF), w_down.astype(BF16),
      ln_g.reshape(1, d), ln_b.reshape(1, d))


def kernel(x, c, positions, mod_w, mod_b, ln_mix_g, ln_mix_b, ln_ffn_g, ln_ffn_b, ev_w_in, ev_idx_k_g, ev_idx_k_b, ev_q_norm_g, ev_kv_norm_g, ev_w_uq, ev_w_ukv, ev_w_out, od_w_qkv, od_w_out, ffn_w_up, ffn_conv_w, ffn_conv_b, ffn_w_down):
    batch, seq, d = x.shape
    assert d == D_MODEL and seq % min(Q_TILE, seq) == 0 and seq % CHUNK == 0
    n = batch * seq
    depth = mod_w.shape[0]
    mod = _modulation(c, mod_w, mod_b).reshape(depth, batch, 6, d)
    tables = _rope_tables(positions)
    xs = x.reshape(n, d)
    for l in range(depth):
        i = l // 2
        if l % 2 == 0:
            weights = _even_weights(ev_w_in[i], ev_idx_k_g[i], ev_idx_k_b[i], ev_q_norm_g[i],
                                    ev_kv_norm_g[i], ev_w_uq[i], ev_w_ukv[i])
            qa, iq, ka0, ka1, va, ik0, ik1, misc, qb, kf, vb = _even_proj(xs, mod[l], tables, weights, batch, seq)
            out_a = _dsa_attention(iq, misc, qa, ik0, ik1, ka0, ka1, va, batch, seq)
            out_b = _mla_attention(qb, kf, vb, batch, seq)
            xs = _out_norm(xs, mod[l], [out_a, out_b], ev_w_out[i], ln_mix_g[l], ln_mix_b[l], batch, seq)
        else:
            qkv = _odd_proj(xs, mod[l], od_w_qkv[i], batch, seq)
            out_c = _stick_attention(qkv, batch, seq)
            xs = _out_norm(xs, mod[l], [out_c], od_w_out[i], ln_mix_g[l], ln_mix_b[l], batch, seq)
        xs = _conv_ffn(xs, mod[l], ffn_w_up[l], ffn_conv_w[l], ffn_conv_b[l], ffn_w_down[l],
                       ln_ffn_g[l], ln_ffn_b[l], batch, seq)
    return xs.reshape(batch, seq, d)
```

```python
import functools

import numpy as np
import jax
import jax.numpy as jnp
from jax import lax
from jax.experimental import pallas as pl
from jax.experimental.pallas import tpu as pltpu

F32 = jnp.float32
BF16 = jnp.bfloat16

D_MODEL = 1024
DEPTH = 4
CHUNK = 64
ROPE_THETA = 10000.0
LN_EPS = 1e-5
RMS_EPS = 1e-6

A_HEADS = 8
A_HEAD_DIM = 64
IDX_HEADS = 4
IDX_DIM = 64
TOPK_MAX = 256

B_HEADS = 8
B_NOPE = 64
B_ROPE = 32
B_V = 64
B_Q_RANK = 384
B_KV_RANK = 256

C_HEADS = 16
C_HEAD_DIM = D_MODEL // C_HEADS

D_FF = 2816
CONV_WIDTH = 3

EVEN_IN_SIZES = (A_HEADS * A_HEAD_DIM, A_HEAD_DIM, A_HEAD_DIM, IDX_HEADS * IDX_DIM, IDX_DIM, IDX_HEADS,
                 B_Q_RANK, B_KV_RANK, B_ROPE)
DEEPNORM_ALPHA = (2 * DEPTH) ** 0.25

LANES = 128
HALF = 64
TOKEN_TILE = 512
Q_TILE = 256
STICK_Q_TILE = 512
STICK_KEY_BLOCK = 256
SEARCH_STEP = 8
SOFTPLUS_LINEAR = 30.0
FF_CHUNK = 256
VMEM_LIMIT = 56 * 2 ** 20

_P_QA = 0
_P_IQ = _P_QA + A_HEADS * A_HEAD_DIM
_P_CQ = _P_IQ + IDX_HEADS * IDX_DIM
_P_CKV = _P_CQ + B_Q_RANK
_P_GK = _P_CKV + B_KV_RANK
_P_GV = _P_GK + LANES
_P_GI = _P_GV + LANES
_P_G5 = _P_GI + LANES
_P_END = _P_G5 + LANES
_KR_LO = HALF
_KR_HI = HALF + B_ROPE
_IW_LO = _KR_HI

_NT = (((1,), (1,)), ((), ()))


def _cparams(sem):
    return pltpu.CompilerParams(dimension_semantics=sem, vmem_limit_bytes=VMEM_LIMIT)


def _const_spec(shape):
    nd = len(shape)
    return pl.BlockSpec(shape, lambda *_: (0,) * nd)


def _lane_iota(shape):
    return lax.broadcasted_iota(jnp.int32, shape, len(shape) - 1)


def _layer_norm(v, g, b):
    mu = jnp.mean(v, axis=-1, keepdims=True)
    d = v - mu
    var = jnp.mean(d * d, axis=-1, keepdims=True)
    return d * lax.rsqrt(var + LN_EPS) * g + b


def _mod_kernel(c_ref, w_ref, b_ref, o_ref):
    c = c_ref[...]
    ca = c / (1.0 + jnp.exp(-c))
    o_ref[0] = jnp.dot(ca, w_ref[0], preferred_element_type=F32,
                       precision=lax.Precision.HIGHEST) + b_ref[0]


def _modulation(c, mod_w, mod_b):
    depth, d, d6 = mod_w.shape
    b = c.shape[0]
    nj = d6 // d
    return pl.pallas_call(
        _mod_kernel,
        grid=(depth, nj),
        in_specs=[pl.BlockSpec((b, d), lambda l, j: (0, 0)),
                  pl.BlockSpec((1, d, d), lambda l, j: (l, 0, j)),
                  pl.BlockSpec((1, 1, d), lambda l, j: (l, 0, j))],
        out_specs=pl.BlockSpec((1, b, d), lambda l, j: (l, 0, j)),
        out_shape=jax.ShapeDtypeStruct((depth, b, d6), F32),
        compiler_params=_cparams(("parallel", "parallel")),
    )(c, mod_w, mod_b.reshape(depth, 1, d6))


def _rope_kernel(pos_ref, fa_ref, ga_ref, fb_ref, gb_ref, ca_ref, sa_ref, cb_ref, sb_ref):
    pos = pos_ref[...]
    anga = pos * fa_ref[...]
    ca_ref[...] = jnp.cos(anga)
    sa_ref[...] = jnp.sin(anga) * ga_ref[...]
    angb = pos * fb_ref[...]
    cb_ref[...] = jnp.cos(angb)
    sb_ref[...] = jnp.sin(angb) * gb_ref[...]


def _rope_tables(positions):
    n = positions.size
    lane = np.arange(LANES)
    inv_a = ROPE_THETA ** (-jnp.arange(0, A_HEAD_DIM, 2, dtype=F32) / A_HEAD_DIM)
    inv_b = ROPE_THETA ** (-jnp.arange(0, B_ROPE, 2, dtype=F32) / B_ROPE)
    fa = inv_a[lane % (A_HEAD_DIM // 2)]
    ga = jnp.asarray(np.where(lane % A_HEAD_DIM < A_HEAD_DIM // 2, -1.0, 1.0), F32)
    in_rope = (lane >= _KR_LO) & (lane < _KR_HI)
    fb = jnp.where(in_rope, inv_b[(lane - _KR_LO) % (B_ROPE // 2)], 0.0)
    gb = jnp.asarray(np.where(in_rope, np.where(lane < _KR_LO + B_ROPE // 2, -1.0, 1.0), 0.0), F32)
    tm = 2048 if n % 2048 == 0 else TOKEN_TILE
    vec = lambda v: v.reshape(1, LANES).astype(F32)
    tab = jax.ShapeDtypeStruct((n, LANES), F32)
    return pl.pallas_call(
        _rope_kernel,
        grid=(n // tm,),
        in_specs=[pl.BlockSpec((tm, 1), lambda i: (i, 0))] + [_const_spec((1, LANES))] * 4,
        out_specs=[pl.BlockSpec((tm, LANES), lambda i: (i, 0))] * 4,
        out_shape=[tab] * 4,
        compiler_params=_cparams(("parallel",)),
    )(positions.reshape(n, 1).astype(F32), vec(fa), vec(ga), vec(fb), vec(gb))


def _rope_a(x, c, s):
    lane = _lane_iota(x.shape)
    partner = jnp.where(lane % A_HEAD_DIM < A_HEAD_DIM // 2,
                        pltpu.roll(x, LANES - A_HEAD_DIM // 2, 1), pltpu.roll(x, A_HEAD_DIM // 2, 1))
    return x * c + partner * s


def _rope_b(x, c, s):
    lane = _lane_iota(x.shape)
    partner = jnp.where(lane < _KR_LO + B_ROPE // 2,
                        pltpu.roll(x, LANES - B_ROPE // 2, 1), pltpu.roll(x, B_ROPE // 2, 1))
    return x * c + partner * s


def _even_proj_kernel(x_ref, mod_ref, win_ref, ca_ref, sa_ref, cb_ref, sb_ref, ikg_ref, ikb_ref,
                      qg_ref, kvg_ref, wuq_ref, wkn_ref, wv_ref,
                      qa_ref, iq_ref, ka0_ref, ka1_ref, va_ref, ik0_ref, ik1_ref, misc_ref,
                      qb_ref, kf_ref, vb_ref):
    sh = mod_ref[0, 0:1, :]
    sc = mod_ref[0, 1:2, :]
    h = (x_ref[...] * (1.0 + sc) + sh).astype(BF16)
    p = jnp.dot(h, win_ref[...], preferred_element_type=F32)
    ca, sa, cb, sb = ca_ref[...], sa_ref[...], cb_ref[...], sb_ref[...]
    lane = _lane_iota(ca.shape)
    low = lane < HALF

    def group(off):
        return p[:, off:off + LANES]

    for g in range(A_HEADS * A_HEAD_DIM // LANES):
        qa_ref[:, g * LANES:(g + 1) * LANES] = (
            _rope_a(group(_P_QA + g * LANES), ca, sa) * (A_HEAD_DIM ** -0.5)).astype(BF16)
    for g in range(IDX_HEADS * IDX_DIM // LANES):
        iq_ref[:, g * LANES:(g + 1) * LANES] = (
            _rope_a(group(_P_IQ + g * LANES), ca, sa) * (IDX_DIM ** -0.5)).astype(BF16)

    ka = _rope_a(group(_P_GK), ca, sa)
    ka0_ref[...] = jnp.where(low, ka, 0.0).astype(BF16)
    ka1_ref[...] = jnp.where(low, 0.0, ka).astype(BF16)
    va_ref[...] = group(_P_GV).astype(BF16)

    ik = group(_P_GI)
    mu = jnp.sum(jnp.where(low, ik, 0.0), axis=-1, keepdims=True) * (1.0 / IDX_DIM)
    d = ik - mu
    var = jnp.sum(jnp.where(low, d * d, 0.0), axis=-1, keepdims=True) * (1.0 / IDX_DIM)
    ik = _rope_a(d * lax.rsqrt(var + LN_EPS) * ikg_ref[...] + ikb_ref[...], ca, sa)
    ik0_ref[...] = jnp.where(low, ik, 0.0).astype(BF16)
    ik1_ref[...] = jnp.where(low, 0.0, ik).astype(BF16)

    g5 = _rope_b(group(_P_G5), cb, sb)
    misc_ref[...] = g5 * (IDX_HEADS ** -0.5)
    kr = jnp.where((lane >= _KR_LO) & (lane < _KR_HI), g5, 0.0)

    cq = p[:, _P_CQ:_P_CQ + B_Q_RANK]
    cqn = cq * lax.rsqrt(jnp.mean(cq * cq, axis=-1, keepdims=True) + RMS_EPS) * qg_ref[...]
    qb = jnp.dot(cqn.astype(BF16), wuq_ref[...], preferred_element_type=F32)
    ckv = p[:, _P_CKV:_P_CKV + B_KV_RANK]
    ckvn = (ckv * lax.rsqrt(jnp.mean(ckv * ckv, axis=-1, keepdims=True) + RMS_EPS) * kvg_ref[...]).astype(BF16)
    kn = jnp.dot(ckvn, wkn_ref[...], preferred_element_type=F32)
    scale_b = (B_NOPE + B_ROPE) ** -0.5
    for hd in range(B_HEADS):
        sl = slice(hd * LANES, (hd + 1) * LANES)
        qb_ref[:, sl] = (_rope_b(qb[:, sl], cb, sb) * scale_b).astype(BF16)
        kf_ref[:, sl] = (kn[:, sl] + kr).astype(BF16)
    vb_ref[...] = jnp.dot(ckvn, wv_ref[...], preferred_element_type=F32).astype(BF16)


def _even_weights(w_in, idx_k_g, idx_k_b, q_norm_g, kv_norm_g, w_uq, w_ukv):
    d = w_in.shape[0]
    qa, ka, va, iq, ik, iw, cq, ckv, kr = jnp.split(w_in, np.cumsum(EVEN_IN_SIZES)[:-1].tolist(), axis=1)
    z = lambda n: jnp.zeros((d, n), w_in.dtype)
    win = jnp.concatenate([qa, iq, cq, ckv, ka, ka, va, va, ik, ik,
                           z(_KR_LO), kr, iw, z(LANES - _IW_LO - IDX_HEADS)], axis=1).astype(BF16)
    assert win.shape[1] == _P_END
    pad_head = B_NOPE + B_ROPE
    wuq = jnp.pad(w_uq.reshape(B_Q_RANK, B_HEADS, pad_head), ((0, 0), (0, 0), (0, LANES - pad_head)))
    wuq = wuq.reshape(B_Q_RANK, B_HEADS * LANES).astype(BF16)
    wkv = w_ukv.reshape(B_KV_RANK, B_HEADS, B_NOPE + B_V)
    wkn = jnp.pad(wkv[:, :, :B_NOPE], ((0, 0), (0, 0), (0, LANES - B_NOPE)))
    wkn = wkn.reshape(B_KV_RANK, B_HEADS * LANES).astype(BF16)
    wv = wkv[:, :, B_NOPE:].reshape(B_KV_RANK, B_HEADS * B_V).astype(BF16)
    two = lambda v: jnp.concatenate([v, v]).reshape(1, LANES).astype(F32)
    return (win, two(idx_k_g), two(idx_k_b), q_norm_g.reshape(1, -1).astype(F32),
            kv_norm_g.reshape(1, -1).astype(F32), wuq, wkn, wv)


def _even_proj(x, mod_l, tables, weights, batch, seq):
    n, d = x.shape
    tm = min(TOKEN_TILE, seq)
    nt = seq // tm
    win, ikg, ikb, qg, kvg, wuq, wkn, wv = weights
    row = lambda w: pl.BlockSpec((tm, w), lambda b, t: (b * nt + t, 0))
    bf = lambda w: jax.ShapeDtypeStruct((n, w), BF16)
    out_widths = [A_HEADS * A_HEAD_DIM, IDX_HEADS * IDX_DIM, LANES, LANES, LANES, LANES, LANES]
    out_shape = [bf(w) for w in out_widths] + [jax.ShapeDtypeStruct((n, LANES), F32)] + \
                [bf(B_HEADS * LANES), bf(B_HEADS * LANES), bf(B_HEADS * B_V)]
    out_specs = [row(w) for w in out_widths] + [row(LANES)] + \
                [row(B_HEADS * LANES), row(B_HEADS * LANES), row(B_HEADS * B_V)]
    return pl.pallas_call(
        _even_proj_kernel,
        grid=(batch, nt),
        in_specs=[row(d), pl.BlockSpec((1, 6, d), lambda b, t: (b, 0, 0)), _const_spec(win.shape)]
                 + [row(LANES)] * 4
                 + [_const_spec(a.shape) for a in (ikg, ikb, qg, kvg, wuq, wkn, wv)],
        out_specs=out_specs,
        out_shape=out_shape,
        compiler_params=_cparams(("parallel", "parallel")),
    )(x, mod_l, win, *tables, ikg, ikb, qg, kvg, wuq, wkn, wv)


def _softmax_pv(logits, v):
    m = jnp.max(logits, axis=-1, keepdims=True)
    e = jnp.exp(logits - m)
    den = jnp.sum(e, axis=-1, keepdims=True)
    return jnp.dot(e.astype(BF16), v, preferred_element_type=F32) / den


def _dsa_kernel(tq, topk, idx_bits, iq_ref, misc_ref, qa_ref, ik0_ref, ik1_ref, ka0_ref, ka1_ref, va_ref,
                o_ref, key_buf, hi_buf, lo_buf, lg_buf, m_buf, l_buf, acc_buf):
    i = pl.program_id(1)
    n_blk = i + 1
    int_min = jnp.int32(-2 ** 31)
    min16 = jnp.int16(-2 ** 15)
    one16, zero16 = jnp.int16(1), jnp.int16(0)
    kf = jnp.float32(topk)
    n_pair = A_HEADS // 2
    rows = n_pair * tq

    def rows_of(j):
        return pl.ds(pl.multiple_of(j * tq, tq), tq)

    iw_t = misc_ref[...].T
    iq = iq_ref[...]
    t_idx = lax.broadcasted_iota(jnp.int32, (tq, tq), 1) + i * tq

    def score_block(j, _):
        score = jnp.zeros((tq, tq), F32)
        for pair in range(IDX_HEADS // 2):
            iq_pair = iq[:, pair * LANES:(pair + 1) * LANES]
            for half, ik_ref in enumerate((ik0_ref, ik1_ref)):
                hd = 2 * pair + half
                rel = lax.dot_general(ik_ref[rows_of(j), :], iq_pair, _NT, preferred_element_type=F32)
                score = score + jnp.maximum(rel, 0.0) * iw_t[_IW_LO + hd:_IW_LO + hd + 1, :]
        score = jnp.where(score == 0.0, 0.0, score)
        bits = lax.bitcast_convert_type(score, jnp.int32)
        key = bits ^ (lax.shift_right_arithmetic(bits, 31) & jnp.int32(0x7FFFFFFF))
        s_idx = lax.broadcasted_iota(jnp.int32, (tq, tq), 0) + j * tq
        adm = lax.shift_right_logical(s_idx, 6) <= lax.shift_right_logical(t_idx, 6)
        key = jnp.where(adm, key, int_min)
        key_buf[rows_of(j), :] = key
        hi_buf[rows_of(j), :] = lax.shift_right_arithmetic(key, 16).astype(jnp.int16)
        return 0

    lax.fori_loop(0, n_blk, score_block, 0)

    def count(pred, buf):
        def body(j, acc):
            hit = jnp.where(pred(buf[rows_of(j), :]), one16, zero16)
            for r in range(tq // 16):
                acc = acc + hit[r * 16:(r + 1) * 16]
            return acc
        acc = lax.fori_loop(0, n_blk, body, jnp.zeros((16, tq), jnp.int16))
        return jnp.sum(acc.astype(jnp.int32), axis=0, keepdims=True).astype(F32)

    def unresolved(cnt):
        return jnp.max(jnp.where(cnt > kf, 1, 0)) > 0

    def search(buf, base, cnt0, n_bits, flip, early_exit):
        def bit(b, cu, cnt_cu):
            cand = cu | lax.shift_left(jnp.int32(1), n_bits - 1 - b)
            image = (cand ^ jnp.int32(flip)).astype(jnp.int16)
            cnt = base + count(lambda x: x >= image, buf)
            take = cnt >= kf
            return jnp.where(take, cand, cu), jnp.where(take, cnt, cnt_cu)

        start = (jnp.zeros((1, tq), jnp.int32), cnt0)
        if not early_exit:
            return lax.fori_loop(0, n_bits, lambda b, st: bit(b, *st), start)

        def group(st):
            g, cu, cnt_cu = st
            for u in range(SEARCH_STEP):
                cu, cnt_cu = bit(g * SEARCH_STEP + u, cu, cnt_cu)
            return g + 1, cu, cnt_cu

        _, cu, cnt_cu = lax.while_loop(lambda s: (s[0] < n_bits // SEARCH_STEP) & unresolved(s[2]), group,
                                       (jnp.int32(0),) + start)
        return cu, cnt_cu

    zeros_row = jnp.zeros((1, tq), jnp.int32)
    hi_cu, cnt_hi = search(hi_buf, 0.0, count(lambda x: x > min16, hi_buf), 16, 0x8000, False)
    hi16 = (hi_cu ^ jnp.int32(0x8000)).astype(jnp.int16)

    def low_half():
        def prep(j, _):
            low = ((key_buf[rows_of(j), :] & jnp.int32(0xFFFF)) ^ jnp.int32(0x8000)).astype(jnp.int16)
            lo_buf[rows_of(j), :] = jnp.where(hi_buf[rows_of(j), :] == hi16, low, min16)
            return 0

        lax.fori_loop(0, n_blk, prep, 0)
        return search(lo_buf, count(lambda x: x > hi16, hi_buf), cnt_hi, 16, 0x8000, True)

    lo_cu, cnt_ge = lax.cond(unresolved(cnt_hi), low_half, lambda: (zeros_row, cnt_hi))
    thr = lax.shift_left(hi_cu ^ jnp.int32(0x8000), 16) | lo_cu
    top_idx = jnp.int32(2 ** idx_bits - 1)

    def tie_break():
        def prep(j, _):
            s_idx = lax.broadcasted_iota(jnp.int32, (tq, tq), 0) + j * tq
            lo_buf[rows_of(j), :] = jnp.where(key_buf[rows_of(j), :] == thr, top_idx - s_idx, -1).astype(jnp.int16)
            return 0

        lax.fori_loop(0, n_blk, prep, 0)
        n_gt = cnt_ge - count(lambda x: x >= zero16, lo_buf)
        return search(lo_buf, n_gt, cnt_ge, idx_bits, 0, False)[0]

    rc = lax.cond(unresolved(cnt_ge), tie_break, lambda: zeros_row)

    qa = qa_ref[...]
    q_stack = jnp.concatenate([qa[:, g * LANES:(g + 1) * LANES] for g in range(n_pair)], axis=0)
    m_buf[...] = jnp.full(m_buf.shape, -jnp.inf, F32)

    def logits_block(j, _):
        k = key_buf[rows_of(j), :]
        rev = top_idx - (lax.broadcasted_iota(jnp.int32, (tq, tq), 0) + j * tq)
        keep = jnp.where(k == int_min, 0.0,
                         jnp.where(k > thr, 1.0, jnp.where(k == thr, jnp.where(rev >= rc, 1.0, 0.0), 0.0)))
        bias = jnp.where(keep.T > 0.5, 0.0, -jnp.inf)
        bias = jnp.concatenate([bias] * n_pair, axis=0)
        lgs = [lax.dot_general(q_stack, ka_ref[rows_of(j), :], _NT, preferred_element_type=F32)
               for ka_ref in (ka0_ref, ka1_ref)]
        for half, lg in enumerate(lgs):
            sl = slice(half * rows, (half + 1) * rows)
            lg = lg + bias
            lg_buf[j, sl, :] = lg
            folded = jnp.maximum(lg[:, :LANES], lg[:, LANES:]) if tq == 2 * LANES else lg
            m_buf[sl, :] = jnp.maximum(m_buf[sl, :], folded)
        return 0

    lax.fori_loop(0, n_blk, logits_block, 0)
    m_buf[...] = jnp.broadcast_to(jnp.max(m_buf[...], axis=-1, keepdims=True), m_buf.shape)
    l_buf[...] = jnp.zeros(l_buf.shape, F32)
    acc_buf[...] = jnp.zeros(acc_buf.shape, F32)

    def pv_block(j, _):
        v = va_ref[rows_of(j), :]
        for half in range(2):
            sl = slice(half * rows, (half + 1) * rows)
            m = m_buf[sl, :]
            e = jnp.exp(lg_buf[j, sl, :] - jnp.concatenate([m] * (tq // LANES), axis=1))
            l_buf[sl, :] = l_buf[sl, :] + (e[:, :LANES] + e[:, LANES:] if tq == 2 * LANES else e)
            acc_buf[sl, :] = acc_buf[sl, :] + jnp.dot(e.astype(BF16), v, preferred_element_type=F32)
        return 0

    lax.fori_loop(0, n_blk, pv_block, 0)
    out = acc_buf[...] / jnp.sum(l_buf[...], axis=-1, keepdims=True)
    lane = _lane_iota((tq, LANES))
    for g in range(n_pair):
        o_ref[:, g * LANES:(g + 1) * LANES] = jnp.where(
            lane < HALF, out[g * tq:(g + 1) * tq], out[rows + g * tq:rows + (g + 1) * tq]).astype(BF16)


def _dsa_attention(iq, misc, qa, ik0, ik1, ka0, ka1, va, batch, seq):
    n = qa.shape[0]
    tq = min(Q_TILE, seq)
    assert tq % LANES == 0 and tq // LANES in (1, 2) and seq <= 2 ** 15
    n_q = seq // tq
    topk = min(TOPK_MAX, seq // 4)
    idx_bits = max(1, int(np.ceil(np.log2(seq))))
    rows = A_HEADS * tq
    qrow = lambda w: pl.BlockSpec((tq, w), lambda b, i: (b * n_q + i, 0))
    krow = pl.BlockSpec((seq, LANES), lambda b, i: (b, 0))
    return pl.pallas_call(
        functools.partial(_dsa_kernel, tq, topk, idx_bits),
        grid=(batch, n_q),
        in_specs=[qrow(iq.shape[1]), qrow(LANES), qrow(qa.shape[1])] + [krow] * 5,
        out_specs=qrow(qa.shape[1]),
        out_shape=jax.ShapeDtypeStruct((n, qa.shape[1]), BF16),
        scratch_shapes=[pltpu.VMEM((seq, tq), jnp.int32), pltpu.VMEM((seq, tq), jnp.int16),
                        pltpu.VMEM((seq, tq), jnp.int16), pltpu.VMEM((n_q, rows, tq), F32),
                        pltpu.VMEM((rows, LANES), F32), pltpu.VMEM((rows, LANES), F32),
                        pltpu.VMEM((rows, LANES), F32)],
        compiler_params=_cparams(("parallel", "arbitrary")),
    )(iq, misc, qa, ik0, ik1, ka0, ka1, va)


def _mla_block(c, tq, q_ref, k_ref, v_ref, o_ref):
    sk = (c + 1) * tq
    t_idx = lax.broadcasted_iota(jnp.int32, (tq, sk), 0) + c * tq
    s_idx = lax.broadcasted_iota(jnp.int32, (tq, sk), 1)
    bias = jnp.where(lax.shift_right_logical(s_idx, 6) <= lax.shift_right_logical(t_idx, 6), 0.0, -jnp.inf)
    v = v_ref[0:sk, :]
    outs = []
    for hh in range(2):
        logits = lax.dot_general(q_ref[:, hh * LANES:(hh + 1) * LANES], k_ref[0:sk, hh * LANES:(hh + 1) * LANES],
                                 _NT, preferred_element_type=F32)
        outs.append(_softmax_pv(logits + bias, v))
    lane = _lane_iota((tq, LANES))
    o_ref[...] = jnp.where(lane < HALF, outs[0], outs[1]).astype(BF16)


def _mla_kernel(tq, n_q, *refs):
    i = pl.program_id(2)
    for c in range(n_q):
        @pl.when(i == c)
        def _(c=c):
            _mla_block(c, tq, *refs)


def _mla_attention(qb, kf, vb, batch, seq):
    n = qb.shape[0]
    tq = min(Q_TILE, seq)
    n_q = seq // tq
    n_pair = B_HEADS // 2
    return pl.pallas_call(
        functools.partial(_mla_kernel, tq, n_q),
        grid=(batch, n_pair, n_q),
        in_specs=[pl.BlockSpec((tq, 2 * LANES), lambda b, p, i: (b * n_q + i, p)),
                  pl.BlockSpec((seq, 2 * LANES), lambda b, p, i: (b, p)),
                  pl.BlockSpec((seq, LANES), lambda b, p, i: (b, p))],
        out_specs=pl.BlockSpec((tq, LANES), lambda b, p, i: (b * n_q + i, p)),
        out_shape=jax.ShapeDtypeStruct((n, B_HEADS * B_V), BF16),
        compiler_params=_cparams(("parallel", "parallel", "arbitrary")),
    )(qb, kf, vb)


def _odd_proj_kernel(x_ref, mod_ref, w_ref, o_ref):
    sh = mod_ref[0, 0:1, :]
    sc = mod_ref[0, 1:2, :]
    h = (x_ref[...] * (1.0 + sc) + sh).astype(BF16)
    p = jnp.dot(h, w_ref[...], preferred_element_type=F32)
    dq = C_HEADS * C_HEAD_DIM
    o_ref[:, 0:dq] = (p[:, 0:dq] * (C_HEAD_DIM ** -0.5)).astype(BF16)
    o_ref[:, dq:] = p[:, dq:].astype(BF16)


def _odd_proj(x, mod_l, w_qkv, batch, seq):
    n, d = x.shape
    tm = min(TOKEN_TILE, seq)
    nt = seq // tm
    w = w_qkv.astype(BF16)
    return pl.pallas_call(
        _odd_proj_kernel,
        grid=(batch, nt),
        in_specs=[pl.BlockSpec((tm, d), lambda b, t: (b * nt + t, 0)),
                  pl.BlockSpec((1, 6, d), lambda b, t: (b, 0, 0)), _const_spec(w.shape)],
        out_specs=pl.BlockSpec((tm, w.shape[1]), lambda b, t: (b * nt + t, 0)),
        out_shape=jax.ShapeDtypeStruct((n, w.shape[1]), BF16),
        compiler_params=_cparams(("parallel", "parallel")),
    )(x, mod_l, w)


def _stick_kernel(tq, kb, q_ref, k_ref, v_ref, o_ref, acc_ref):
    i = pl.program_id(2)
    per_tile = tq // kb
    q2 = q_ref[...]
    lane = _lane_iota((tq, LANES))
    tri = jnp.where(lax.broadcasted_iota(jnp.int32, (kb, kb), 0) >= lax.broadcasted_iota(jnp.int32, (kb, kb), 1),
                    1.0, 0.0).astype(BF16)
    row = lax.broadcasted_iota(jnp.int32, (tq, kb), 0)
    col = lax.broadcasted_iota(jnp.int32, (tq, kb), 1)
    q_heads = (jnp.where(lane < HALF, q2, jnp.zeros_like(q2)), jnp.where(lane < HALF, jnp.zeros_like(q2), q2))
    acc_ref[...] = jnp.zeros(acc_ref.shape, F32)

    def blocks(js, carries, diags):
        streams = [(u, hh) for u in range(len(js)) for hh in range(2)]
        kv = []
        for j in js:
            start = pl.multiple_of(j * kb, kb)
            kv.append((k_ref[pl.ds(start, kb), :], v_ref[pl.ds(start, kb), :]))
        before = [None if d is None else (col + d * kb < row) for d in diags]
        z, inc = {}, {}
        carry = list(carries)

        def stage_z(s):
            z[s] = lax.dot_general(q_heads[s[1]], kv[s[0]][0], _NT, preferred_element_type=F32)

        def stage_sum(s):
            sp = jnp.where(z[s] > SOFTPLUS_LINEAR, z[s], jnp.log(1.0 + jnp.exp(z[s])))
            if before[s[0]] is not None:
                sp = jnp.where(before[s[0]], sp, 0.0)
            inc[s] = jnp.dot(sp.astype(BF16), tri, preferred_element_type=F32)

        def stage_out(s):
            u, hh = s
            a = jnp.exp(z[s] - inc[s] - carry[hh])
            if before[u] is not None:
                a = jnp.where(before[u], a, 0.0)
            acc_ref[hh] = acc_ref[hh] + jnp.dot(a.astype(BF16), kv[u][1], preferred_element_type=F32)
            carry[hh] = carry[hh] + inc[s][:, 0:1]

        for t in range(len(streams) + 2):
            if t < len(streams):
                stage_z(streams[t])
            if 0 <= t - 1 < len(streams):
                stage_sum(streams[t - 1])
            if 0 <= t - 2 < len(streams):
                stage_out(streams[t - 2])
        return tuple(carry)

    carries = (jnp.zeros((tq, 1), F32), jnp.zeros((tq, 1), F32))
    diag = list(reversed(range(per_tile)))
    carries = blocks([i * per_tile + d for d in diag], carries, diag)
    n_full = i * per_tile
    lax.fori_loop(0, i, lambda s, cs: blocks([n_full - 1 - s * per_tile - u for u in range(per_tile)], cs,
                                             [None] * per_tile), carries)
    o_ref[...] = jnp.where(lane < HALF, acc_ref[0], acc_ref[1]).astype(BF16)


def _stick_attention(qkv, batch, seq):
    n = qkv.shape[0]
    tq = min(STICK_Q_TILE, seq)
    kb = min(STICK_KEY_BLOCK, tq)
    n_q = seq // tq
    n_pair = C_HEADS // 2
    return pl.pallas_call(
        functools.partial(_stick_kernel, tq, kb),
        grid=(batch, n_pair, n_q),
        in_specs=[pl.BlockSpec((tq, LANES), lambda b, p, i: (b * n_q + i, p)),
                  pl.BlockSpec((seq, LANES), lambda b, p, i: (b, n_pair + p)),
                  pl.BlockSpec((seq, LANES), lambda b, p, i: (b, 2 * n_pair + p))],
        out_specs=pl.BlockSpec((tq, LANES), lambda b, p, i: (b * n_q + i, p)),
        out_shape=jax.ShapeDtypeStruct((n, C_HEADS * C_HEAD_DIM), BF16),
        scratch_shapes=[pltpu.VMEM((2, tq, LANES), F32)],
        compiler_params=_cparams(("parallel", "parallel", "arbitrary")),
    )(qkv, qkv, qkv)


def _out_norm_kernel(n_in, *refs):
    x_ref, mod_ref = refs[0], refs[1]
    ins = refs[2:2 + n_in]
    ws = refs[2 + n_in:2 + 2 * n_in]
    g_ref, b_ref, o_ref = refs[2 + 2 * n_in:]
    y = jnp.dot(ins[0][...], ws[0][...], preferred_element_type=F32)
    for a_ref, w_ref in zip(ins[1:], ws[1:]):
        y = y + jnp.dot(a_ref[...], w_ref[...], preferred_element_type=F32)
    gate = mod_ref[0, 2:3, :]
    o_ref[...] = _layer_norm(DEEPNORM_ALPHA * x_ref[...] + (1.0 + gate) * y, g_ref[...], b_ref[...])


def _out_norm(x, mod_l, acts, w_out, ln_g, ln_b, batch, seq):
    n, d = x.shape
    tm = min(TOKEN_TILE, seq)
    nt = seq // tm
    row = lambda w: pl.BlockSpec((tm, w), lambda b, t: (b * nt + t, 0))
    ws, off = [], 0
    for a in acts:
        ws.append(w_out[off:off + a.shape[1]].astype(BF16))
        off += a.shape[1]
    return pl.pallas_call(
        functools.partial(_out_norm_kernel, len(acts)),
        grid=(batch, nt),
        in_specs=[row(d), pl.BlockSpec((1, 6, d), lambda b, t: (b, 0, 0))]
                 + [row(a.shape[1]) for a in acts] + [_const_spec(w.shape) for w in ws]
                 + [_const_spec((1, d))] * 2,
        out_specs=row(d),
        out_shape=jax.ShapeDtypeStruct((n, d), F32),
        compiler_params=_cparams(("parallel", "parallel")),
    )(x, mod_l, *acts, *ws, ln_g.reshape(1, d), ln_b.reshape(1, d))


_HALO = 8


def _ffn_kernel(tm, x_ref, mod_ref, wup_ref, cw_ref, cb_ref, wdn_ref, g_ref, b_ref, o_ref, a_buf, tail_buf):
    t = pl.program_id(1)
    x = x_ref[...]
    sh = mod_ref[0, 3:4, :]
    sc = mod_ref[0, 4:5, :]
    gate = mod_ref[0, 5:6, :]
    h = (x * (1.0 + sc) + sh).astype(BF16)

    @pl.when(t == 0)
    def _():
        tail_buf[...] = jnp.zeros_like(tail_buf)

    def up(ci):
        cs = slice(ci * FF_CHUNK, (ci + 1) * FF_CHUNK)
        gs = slice(D_FF + ci * FF_CHUNK, D_FF + (ci + 1) * FF_CHUNK)
        return (jnp.dot(h, wup_ref[:, cs], preferred_element_type=F32),
                jnp.dot(h, wup_ref[:, gs], preferred_element_type=F32))

    n_chunk = D_FF // FF_CHUNK
    y = jnp.zeros((tm, x.shape[1]), F32)
    nxt = up(0)
    for ci in range(n_chunk):
        cs = slice(ci * FF_CHUNK, (ci + 1) * FF_CHUNK)
        a, gt = nxt
        if ci + 1 < n_chunk:
            nxt = up(ci + 1)
        a_buf[0:_HALO, :] = tail_buf[:, cs]
        a_buf[_HALO:, :] = a
        tail_buf[:, cs] = a[tm - _HALO:, :]
        conv = (a_buf[_HALO - 2:_HALO - 2 + tm, :] * cw_ref[0:1, cs]
                + a_buf[_HALO - 1:_HALO - 1 + tm, :] * cw_ref[1:2, cs]
                + a * cw_ref[2:3, cs] + cb_ref[:, cs])
        u = conv / (1.0 + jnp.exp(-conv)) * gt
        y = y + jnp.dot(u.astype(BF16), wdn_ref[cs, :], preferred_element_type=F32)
    o_ref[...] = _layer_norm(DEEPNORM_ALPHA * x + (1.0 + gate) * y, g_ref[...], b_ref[...])


def _conv_ffn(x, mod_l, w_up, conv_w, conv_b, w_down, ln_g, ln_b, batch, seq):
    n, d = x.shape
    tm = min(TOKEN_TILE, seq)
    nt = seq // tm
    row = pl.BlockSpec((tm, d), lambda b, t: (b * nt + t, 0))
    single = lambda shape: pl.BlockSpec(shape, lambda b, t: (0,) * len(shape), pipeline_mode=pl.Buffered(1))
    return pl.pallas_call(
        functools.partial(_ffn_kernel, tm),
        grid=(batch, nt),
        in_specs=[row, pl.BlockSpec((1, 6, d), lambda b, t: (b, 0, 0)), single((d, 2 * D_FF)),
                  _const_spec((CONV_WIDTH, D_FF)), _const_spec((1, D_FF)), single((D_FF, d)),
                  _const_spec((1, d)), _const_spec((1, d))],
        out_specs=row,
        out_shape=jax.ShapeDtypeStruct((n, d), F32),
        scratch_shapes=[pltpu.VMEM((_HALO + tm, FF_CHUNK), F32), pltpu.VMEM((_HALO, D_FF), F32)],
        compiler_params=_cparams(("parallel", "arbitrary")),
    )(x, mod_l, w_up.astype(BF16), conv_w, conv_b.reshape(1, D_FF), w_down.astype(BF16),
      ln_g.reshape(1, d), ln_b.reshape(1, d))


def kernel(x, c, positions, mod_w, mod_b, ln_mix_g, ln_mix_b, ln_ffn_g, ln_ffn_b, ev_w_in, ev_idx_k_g, ev_idx_k_b, ev_q_norm_g, ev_kv_norm_g, ev_w_uq, ev_w_ukv, ev_w_out, od_w_qkv, od_w_out, ffn_w_up, ffn_conv_w, ffn_conv_b, ffn_w_down):
    batch, seq, d = x.shape
    assert d == D_MODEL and seq % min(Q_TILE, seq) == 0 and seq % CHUNK == 0
    n = batch * seq
    depth = mod_w.shape[0]
    mod = _modulation(c, mod_w, mod_b).reshape(depth, batch, 6, d)
    tables = _rope_tables(positions)
    xs = x.reshape(n, d)
    for l in range(depth):
        i = l // 2
        if l % 2 == 0:
            weights = _even_weights(ev_w_in[i], ev_idx_k_g[i], ev_idx_k_b[i], ev_q_norm_g[i],
                                    ev_kv_norm_g[i], ev_w_uq[i], ev_w_ukv[i])
            qa, iq, ka0, ka1, va, ik0, ik1, misc, qb, kf, vb = _even_proj(xs, mod[l], tables, weights, batch, seq)
            out_a = _dsa_attention(iq, misc, qa, ik0, ik1, ka0, ka1, va, batch, seq)
            out_b = _mla_attention(qb, kf, vb, batch, seq)
            xs = _out_norm(xs, mod[l], [out_a, out_b], ev_w_out[i], ln_mix_g[l], ln_mix_b[l], batch, seq)
        else:
            qkv = _odd_proj(xs, mod[l], od_w_qkv[i], batch, seq)
            out_c = _stick_attention(qkv, batch, seq)
            xs = _out_norm(xs, mod[l], [out_c], od_w_out[i], ln_mix_g[l], ln_mix_b[l], batch, seq)
        xs = _conv_ffn(xs, mod[l], ffn_w_up[l], ffn_conv_w[l], ffn_conv_b[l], ffn_w_down[l],
                       ln_ffn_g[l], ln_ffn_b[l], batch, seq)
    return xs.reshape(batch, seq, d)
```

```python
import functools

import numpy as np
import jax
import jax.numpy as jnp
from jax import lax
from jax.experimental import pallas as pl
from jax.experimental.pallas import tpu as pltpu

F32 = jnp.float32
BF16 = jnp.bfloat16

D_MODEL = 1024
DEPTH = 4
CHUNK = 64
ROPE_THETA = 10000.0
LN_EPS = 1e-5
RMS_EPS = 1e-6

A_HEADS = 8
A_HEAD_DIM = 64
IDX_HEADS = 4
IDX_DIM = 64
TOPK_MAX = 256

B_HEADS = 8
B_NOPE = 64
B_ROPE = 32
B_V = 64
B_Q_RANK = 384
B_KV_RANK = 256

C_HEADS = 16
C_HEAD_DIM = D_MODEL // C_HEADS

D_FF = 2816
CONV_WIDTH = 3

EVEN_IN_SIZES = (A_HEADS * A_HEAD_DIM, A_HEAD_DIM, A_HEAD_DIM, IDX_HEADS * IDX_DIM, IDX_DIM, IDX_HEADS,
                 B_Q_RANK, B_KV_RANK, B_ROPE)
DEEPNORM_ALPHA = (2 * DEPTH) ** 0.25
LOG2E = 1.4426950408889634

LANES = 128
HALF = 64
TOKEN_TILE = 512
Q_TILE = 256
STICK_Q_TILE = 512
STICK_KEY_BLOCK = 256
MLA_Q_TILE = 1024
MLA_KEY_BLOCK = 256
SEARCH_STEP = 8
SOFTPLUS_LINEAR = 30.0
FF_CHUNK = 256
VMEM_LIMIT = 56 * 2 ** 20

_P_QA = 0
_P_IQ = _P_QA + A_HEADS * A_HEAD_DIM
_P_CQ = _P_IQ + IDX_HEADS * IDX_DIM
_P_CKV = _P_CQ + B_Q_RANK
_P_GK = _P_CKV + B_KV_RANK
_P_GV = _P_GK + LANES
_P_GI = _P_GV + LANES
_P_G5 = _P_GI + LANES
_P_END = _P_G5 + LANES
_KR_LO = HALF
_KR_HI = HALF + B_ROPE
_IW_LO = _KR_HI

_NT = (((1,), (1,)), ((), ()))


def _cparams(sem):
    return pltpu.CompilerParams(dimension_semantics=sem, vmem_limit_bytes=VMEM_LIMIT)


def _const_spec(shape):
    nd = len(shape)
    return pl.BlockSpec(shape, lambda *_: (0,) * nd)


def _lane_iota(shape):
    return lax.broadcasted_iota(jnp.int32, shape, len(shape) - 1)


def _layer_norm(v, g, b):
    mu = jnp.mean(v, axis=-1, keepdims=True)
    d = v - mu
    var = jnp.mean(d * d, axis=-1, keepdims=True)
    return d * lax.rsqrt(var + LN_EPS) * g + b


def _mod_kernel(c_ref, w_ref, b_ref, o_ref):
    c = c_ref[...]
    ca = c / (1.0 + jnp.exp(-c))
    o_ref[0] = jnp.dot(ca, w_ref[0], preferred_element_type=F32,
                       precision=lax.Precision.HIGHEST) + b_ref[0]


def _modulation(c, mod_w, mod_b):
    depth, d, d6 = mod_w.shape
    b = c.shape[0]
    nj = d6 // d
    return pl.pallas_call(
        _mod_kernel,
        grid=(depth, nj),
        in_specs=[pl.BlockSpec((b, d), lambda l, j: (0, 0)),
                  pl.BlockSpec((1, d, d), lambda l, j: (l, 0, j)),
                  pl.BlockSpec((1, 1, d), lambda l, j: (l, 0, j))],
        out_specs=pl.BlockSpec((1, b, d), lambda l, j: (l, 0, j)),
        out_shape=jax.ShapeDtypeStruct((depth, b, d6), F32),
        compiler_params=_cparams(("parallel", "parallel")),
    )(c, mod_w, mod_b.reshape(depth, 1, d6))


def _rope_kernel(pos_ref, fa_ref, ga_ref, fb_ref, gb_ref, ca_ref, sa_ref, cb_ref, sb_ref):
    pos = pos_ref[...]
    anga = pos * fa_ref[...]
    ca_ref[...] = jnp.cos(anga)
    sa_ref[...] = jnp.sin(anga) * ga_ref[...]
    angb = pos * fb_ref[...]
    cb_ref[...] = jnp.cos(angb)
    sb_ref[...] = jnp.sin(angb) * gb_ref[...]


def _rope_tables(positions):
    n = positions.size
    lane = np.arange(LANES)
    inv_a = ROPE_THETA ** (-jnp.arange(0, A_HEAD_DIM, 2, dtype=F32) / A_HEAD_DIM)
    inv_b = ROPE_THETA ** (-jnp.arange(0, B_ROPE, 2, dtype=F32) / B_ROPE)
    fa = inv_a[lane % (A_HEAD_DIM // 2)]
    ga = jnp.asarray(np.where(lane % A_HEAD_DIM < A_HEAD_DIM // 2, -1.0, 1.0), F32)
    in_rope = (lane >= _KR_LO) & (lane < _KR_HI)
    fb = jnp.where(in_rope, inv_b[(lane - _KR_LO) % (B_ROPE // 2)], 0.0)
    gb = jnp.asarray(np.where(in_rope, np.where(lane < _KR_LO + B_ROPE // 2, -1.0, 1.0), 0.0), F32)
    tm = 2048 if n % 2048 == 0 else TOKEN_TILE
    vec = lambda v: v.reshape(1, LANES).astype(F32)
    tab = jax.ShapeDtypeStruct((n, LANES), F32)
    return pl.pallas_call(
        _rope_kernel,
        grid=(n // tm,),
        in_specs=[pl.BlockSpec((tm, 1), lambda i: (i, 0))] + [_const_spec((1, LANES))] * 4,
        out_specs=[pl.BlockSpec((tm, LANES), lambda i: (i, 0))] * 4,
        out_shape=[tab] * 4,
        compiler_params=_cparams(("parallel",)),
    )(positions.reshape(n, 1).astype(F32), vec(fa), vec(ga), vec(fb), vec(gb))


def _rope_a(x, c, s):
    lane = _lane_iota(x.shape)
    partner = jnp.where(lane % A_HEAD_DIM < A_HEAD_DIM // 2,
                        pltpu.roll(x, LANES - A_HEAD_DIM // 2, 1), pltpu.roll(x, A_HEAD_DIM // 2, 1))
    return x * c + partner * s


def _rope_b(x, c, s):
    lane = _lane_iota(x.shape)
    partner = jnp.where(lane < _KR_LO + B_ROPE // 2,
                        pltpu.roll(x, LANES - B_ROPE // 2, 1), pltpu.roll(x, B_ROPE // 2, 1))
    return x * c + partner * s


def _even_proj_kernel(x_ref, mod_ref, win_ref, ca_ref, sa_ref, cb_ref, sb_ref, ikg_ref, ikb_ref,
                      qg_ref, kvg_ref, wuq_ref, wkn_ref, wv_ref,
                      qa_ref, iq_ref, ka0_ref, ka1_ref, va_ref, ik0_ref, ik1_ref, misc_ref,
                      qb_ref, kf_ref, vb_ref):
    sh = mod_ref[0, 0:1, :]
    sc = mod_ref[0, 1:2, :]
    h = (x_ref[...] * (1.0 + sc) + sh).astype(BF16)
    ca, sa, cb, sb = ca_ref[...], sa_ref[...], cb_ref[...], sb_ref[...]
    lane = _lane_iota(ca.shape)
    low = lane < HALF
    kr = []

    def rms(v, g_ref):
        return (v * lax.rsqrt(jnp.mean(v * v, axis=-1, keepdims=True) + RMS_EPS) * g_ref[...]).astype(BF16)

    def misc_group(p):
        g5 = _rope_b(p, cb, sb)
        misc_ref[...] = g5 * (IDX_HEADS ** -0.5)
        kr.append(jnp.where((lane >= _KR_LO) & (lane < _KR_HI), g5, 0.0))

    def kv_latent(p):
        ckvn = rms(p, kvg_ref)
        kn = jnp.dot(ckvn, wkn_ref[...], preferred_element_type=F32)
        for hd in range(B_HEADS):
            sl = slice(hd * LANES, (hd + 1) * LANES)
            kf_ref[:, sl] = (kn[:, sl] + kr[0]).astype(BF16)
        vb_ref[...] = jnp.dot(ckvn, wv_ref[...], preferred_element_type=F32).astype(BF16)

    def q_latent(p):
        qb = jnp.dot(rms(p, qg_ref), wuq_ref[...], preferred_element_type=F32)
        scale_b = (B_NOPE + B_ROPE) ** -0.5 * LOG2E
        for hd in range(B_HEADS):
            sl = slice(hd * LANES, (hd + 1) * LANES)
            qb_ref[:, sl] = (_rope_b(qb[:, sl], cb, sb) * scale_b).astype(BF16)

    def roped_heads(o_ref, scale):
        def store(p):
            for g in range(p.shape[1] // LANES):
                sl = slice(g * LANES, (g + 1) * LANES)
                o_ref[:, sl] = (_rope_a(p[:, sl], ca, sa) * scale).astype(BF16)
        return store

    def dsa_key(p):
        ka = _rope_a(p, ca, sa)
        ka0_ref[...] = jnp.where(low, ka, 0.0).astype(BF16)
        ka1_ref[...] = jnp.where(low, 0.0, ka).astype(BF16)

    def dsa_value(p):
        va_ref[...] = p.astype(BF16)

    def indexer_key(p):
        mu = jnp.sum(jnp.where(low, p, 0.0), axis=-1, keepdims=True) * (1.0 / IDX_DIM)
        d = p - mu
        var = jnp.sum(jnp.where(low, d * d, 0.0), axis=-1, keepdims=True) * (1.0 / IDX_DIM)
        ik = _rope_a(d * lax.rsqrt(var + LN_EPS) * ikg_ref[...] + ikb_ref[...], ca, sa)
        ik0_ref[...] = jnp.where(low, ik, 0.0).astype(BF16)
        ik1_ref[...] = jnp.where(low, 0.0, ik).astype(BF16)

    both = lambda first, second: lambda p: (first(p[:, :LANES]), second(p[:, LANES:]))
    stages = [(_P_GI, 2 * LANES, both(indexer_key, misc_group)), (_P_CKV, B_KV_RANK, kv_latent),
              (_P_CQ, B_Q_RANK, q_latent),
              (_P_QA, A_HEADS * A_HEAD_DIM, roped_heads(qa_ref, A_HEAD_DIM ** -0.5 * LOG2E)),
              (_P_IQ, IDX_HEADS * IDX_DIM, roped_heads(iq_ref, IDX_DIM ** -0.5)),
              (_P_GK, 2 * LANES, both(dsa_key, dsa_value))]
    project = lambda off, width: jnp.dot(h, win_ref[:, off:off + width], preferred_element_type=F32)
    nxt = project(*stages[0][:2])
    for s, (_, _, epilogue) in enumerate(stages):
        cur = nxt
        if s + 1 < len(stages):
            nxt = project(*stages[s + 1][:2])
        epilogue(cur)


def _even_weights(w_in, idx_k_g, idx_k_b, q_norm_g, kv_norm_g, w_uq, w_ukv):
    d = w_in.shape[0]
    qa, ka, va, iq, ik, iw, cq, ckv, kr = jnp.split(w_in, np.cumsum(EVEN_IN_SIZES)[:-1].tolist(), axis=1)
    z = lambda n: jnp.zeros((d, n), w_in.dtype)
    win = jnp.concatenate([qa, iq, cq, ckv, ka, ka, va, va, ik, ik,
                           z(_KR_LO), kr, iw, z(LANES - _IW_LO - IDX_HEADS)], axis=1).astype(BF16)
    assert win.shape[1] == _P_END
    pad_head = B_NOPE + B_ROPE
    wuq = jnp.pad(w_uq.reshape(B_Q_RANK, B_HEADS, pad_head), ((0, 0), (0, 0), (0, LANES - pad_head)))
    wuq = wuq.reshape(B_Q_RANK, B_HEADS * LANES).astype(BF16)
    wkv = w_ukv.reshape(B_KV_RANK, B_HEADS, B_NOPE + B_V)
    wkn = jnp.pad(wkv[:, :, :B_NOPE], ((0, 0), (0, 0), (0, LANES - B_NOPE)))
    wkn = wkn.reshape(B_KV_RANK, B_HEADS * LANES).astype(BF16)
    wv = wkv[:, :, B_NOPE:].reshape(B_KV_RANK, B_HEADS * B_V).astype(BF16)
    two = lambda v: jnp.concatenate([v, v]).reshape(1, LANES).astype(F32)
    return (win, two(idx_k_g), two(idx_k_b), q_norm_g.reshape(1, -1).astype(F32),
            kv_norm_g.reshape(1, -1).astype(F32), wuq, wkn, wv)


def _even_proj(x, mod_l, tables, weights, batch, seq):
    n, d = x.shape
    tm = min(TOKEN_TILE, seq)
    nt = seq // tm
    win, ikg, ikb, qg, kvg, wuq, wkn, wv = weights
    row = lambda w: pl.BlockSpec((tm, w), lambda b, t: (b * nt + t, 0))
    bf = lambda w: jax.ShapeDtypeStruct((n, w), BF16)
    out_widths = [A_HEADS * A_HEAD_DIM, IDX_HEADS * IDX_DIM, LANES, LANES, LANES, LANES, LANES]
    out_shape = [bf(w) for w in out_widths] + [jax.ShapeDtypeStruct((n, LANES), F32)] + \
                [bf(B_HEADS * LANES), bf(B_HEADS * LANES), bf(B_HEADS * B_V)]
    out_specs = [row(w) for w in out_widths] + [row(LANES)] + \
                [row(B_HEADS * LANES), row(B_HEADS * LANES), row(B_HEADS * B_V)]
    return pl.pallas_call(
        _even_proj_kernel,
        grid=(batch, nt),
        in_specs=[row(d), pl.BlockSpec((1, 6, d), lambda b, t: (b, 0, 0)), _const_spec(win.shape)]
                 + [row(LANES)] * 4
                 + [_const_spec(a.shape) for a in (ikg, ikb, qg, kvg, wuq, wkn, wv)],
        out_specs=out_specs,
        out_shape=out_shape,
        compiler_params=_cparams(("parallel", "parallel")),
    )(x, mod_l, win, *tables, ikg, ikb, qg, kvg, wuq, wkn, wv)


def _dsa_kernel(tq, topk, idx_bits, iq_ref, misc_ref, qa_ref, ik0_ref, ik1_ref, ka0_ref, ka1_ref, va_ref,
                o_ref, key_buf, hi_buf, lo_buf, lg_buf, m_buf, l_buf, acc_buf):
    i = pl.program_id(1)
    n_blk = i + 1
    int_min = jnp.int32(-2 ** 31)
    min16 = jnp.int16(-2 ** 15)
    one16, zero16 = jnp.int16(1), jnp.int16(0)
    kf = jnp.float32(topk)
    n_pair = A_HEADS // 2
    rows = n_pair * tq

    def rows_of(j):
        return pl.ds(pl.multiple_of(j * tq, tq), tq)

    iw_t = misc_ref[...].T
    iq = iq_ref[...]
    t_idx = lax.broadcasted_iota(jnp.int32, (tq, tq), 1) + i * tq

    def score_block(j, _):
        score = jnp.zeros((tq, tq), F32)
        for pair in range(IDX_HEADS // 2):
            iq_pair = iq[:, pair * LANES:(pair + 1) * LANES]
            for half, ik_ref in enumerate((ik0_ref, ik1_ref)):
                hd = 2 * pair + half
                rel = lax.dot_general(ik_ref[rows_of(j), :], iq_pair, _NT, preferred_element_type=F32)
                score = score + jnp.maximum(rel, 0.0) * iw_t[_IW_LO + hd:_IW_LO + hd + 1, :]
        score = jnp.where(score == 0.0, 0.0, score)
        bits = lax.bitcast_convert_type(score, jnp.int32)
        key = bits ^ (lax.shift_right_arithmetic(bits, 31) & jnp.int32(0x7FFFFFFF))
        s_idx = lax.broadcasted_iota(jnp.int32, (tq, tq), 0) + j * tq
        adm = lax.shift_right_logical(s_idx, 6) <= lax.shift_right_logical(t_idx, 6)
        key = jnp.where(adm, key, int_min)
        key_buf[rows_of(j), :] = key
        hi_buf[rows_of(j), :] = lax.shift_right_arithmetic(key, 16).astype(jnp.int16)
        return 0

    lax.fori_loop(0, n_blk, score_block, 0)

    def count(pred, buf):
        def body(j, acc):
            hit = jnp.where(pred(buf[rows_of(j), :]), one16, zero16)
            for r in range(tq // 16):
                acc = acc + hit[r * 16:(r + 1) * 16]
            return acc
        acc = lax.fori_loop(0, n_blk, body, jnp.zeros((16, tq), jnp.int16))
        return jnp.sum(acc.astype(jnp.int32), axis=0, keepdims=True).astype(F32)

    def unresolved(cnt):
        return jnp.max(jnp.where(cnt > kf, 1, 0)) > 0

    def search(buf, base, cnt0, n_bits, flip, early_exit):
        def bit(b, cu, cnt_cu):
            cand = cu | lax.shift_left(jnp.int32(1), n_bits - 1 - b)
            image = (cand ^ jnp.int32(flip)).astype(jnp.int16)
            cnt = base + count(lambda x: x >= image, buf)
            take = cnt >= kf
            return jnp.where(take, cand, cu), jnp.where(take, cnt, cnt_cu)

        start = (jnp.zeros((1, tq), jnp.int32), cnt0)
        if not early_exit:
            return lax.fori_loop(0, n_bits, lambda b, st: bit(b, *st), start)

        def group(st):
            g, cu, cnt_cu = st
            for u in range(SEARCH_STEP):
                cu, cnt_cu = bit(g * SEARCH_STEP + u, cu, cnt_cu)
            return g + 1, cu, cnt_cu

        _, cu, cnt_cu = lax.while_loop(lambda s: (s[0] < n_bits // SEARCH_STEP) & unresolved(s[2]), group,
                                       (jnp.int32(0),) + start)
        return cu, cnt_cu

    zeros_row = jnp.zeros((1, tq), jnp.int32)
    hi_cu, cnt_hi = search(hi_buf, 0.0, count(lambda x: x > min16, hi_buf), 16, 0x8000, False)
    hi16 = (hi_cu ^ jnp.int32(0x8000)).astype(jnp.int16)

    def low_half():
        def prep(j, _):
            low = ((key_buf[rows_of(j), :] & jnp.int32(0xFFFF)) ^ jnp.int32(0x8000)).astype(jnp.int16)
            lo_buf[rows_of(j), :] = jnp.where(hi_buf[rows_of(j), :] == hi16, low, min16)
            return 0

        lax.fori_loop(0, n_blk, prep, 0)
        return search(lo_buf, count(lambda x: x > hi16, hi_buf), cnt_hi, 16, 0x8000, True)

    lo_cu, cnt_ge = lax.cond(unresolved(cnt_hi), low_half, lambda: (zeros_row, cnt_hi))
    thr = lax.shift_left(hi_cu ^ jnp.int32(0x8000), 16) | lo_cu
    top_idx = jnp.int32(2 ** idx_bits - 1)

    def tie_break():
        def prep(j, _):
            s_idx = lax.broadcasted_iota(jnp.int32, (tq, tq), 0) + j * tq
            lo_buf[rows_of(j), :] = jnp.where(key_buf[rows_of(j), :] == thr, top_idx - s_idx, -1).astype(jnp.int16)
            return 0

        lax.fori_loop(0, n_blk, prep, 0)
        n_gt = cnt_ge - count(lambda x: x >= zero16, lo_buf)
        return search(lo_buf, n_gt, cnt_ge, idx_bits, 0, False)[0]

    rc = lax.cond(unresolved(cnt_ge), tie_break, lambda: zeros_row)

    qa = qa_ref[...]
    q_stack = jnp.concatenate([qa[:, g * LANES:(g + 1) * LANES] for g in range(n_pair)], axis=0)
    m_buf[...] = jnp.full(m_buf.shape, -jnp.inf, F32)

    def logits_block(j, _):
        k = key_buf[rows_of(j), :]
        rev = top_idx - (lax.broadcasted_iota(jnp.int32, (tq, tq), 0) + j * tq)
        keep = jnp.where(k == int_min, 0.0,
                         jnp.where(k > thr, 1.0, jnp.where(k == thr, jnp.where(rev >= rc, 1.0, 0.0), 0.0)))
        bias = jnp.where(keep.T > 0.5, 0.0, -jnp.inf)
        bias = jnp.concatenate([bias] * n_pair, axis=0)
        lgs = [lax.dot_general(q_stack, ka_ref[rows_of(j), :], _NT, preferred_element_type=F32)
               for ka_ref in (ka0_ref, ka1_ref)]
        for half, lg in enumerate(lgs):
            sl = slice(half * rows, (half + 1) * rows)
            lg = lg + bias
            lg_buf[j, sl, :] = lg
            folded = jnp.maximum(lg[:, :LANES], lg[:, LANES:]) if tq == 2 * LANES else lg
            m_buf[sl, :] = jnp.maximum(m_buf[sl, :], folded)
        return 0

    lax.fori_loop(0, n_blk, logits_block, 0)
    m_buf[...] = jnp.broadcast_to(jnp.max(m_buf[...], axis=-1, keepdims=True), m_buf.shape)
    l_buf[...] = jnp.zeros(l_buf.shape, F32)
    acc_buf[...] = jnp.zeros(acc_buf.shape, F32)

    def pv_block(j, _):
        v = va_ref[rows_of(j), :]
        for half in range(2):
            sl = slice(half * rows, (half + 1) * rows)
            m = m_buf[sl, :]
            e = jnp.exp2(lg_buf[j, sl, :] - jnp.concatenate([m] * (tq // LANES), axis=1))
            l_buf[sl, :] = l_buf[sl, :] + (e[:, :LANES] + e[:, LANES:] if tq == 2 * LANES else e)
            acc_buf[sl, :] = acc_buf[sl, :] + jnp.dot(e.astype(BF16), v, preferred_element_type=F32)
        return 0

    lax.fori_loop(0, n_blk, pv_block, 0)
    out = acc_buf[...] / jnp.sum(l_buf[...], axis=-1, keepdims=True)
    lane = _lane_iota((tq, LANES))
    for g in range(n_pair):
        o_ref[:, g * LANES:(g + 1) * LANES] = jnp.where(
            lane < HALF, out[g * tq:(g + 1) * tq], out[rows + g * tq:rows + (g + 1) * tq]).astype(BF16)


def _dsa_attention(iq, misc, qa, ik0, ik1, ka0, ka1, va, batch, seq):
    n = qa.shape[0]
    tq = min(Q_TILE, seq)
    assert tq % LANES == 0 and tq // LANES in (1, 2) and seq <= 2 ** 15
    n_q = seq // tq
    topk = min(TOPK_MAX, seq // 4)
    idx_bits = max(1, int(np.ceil(np.log2(seq))))
    rows = A_HEADS * tq
    qrow = lambda w: pl.BlockSpec((tq, w), lambda b, i: (b * n_q + i, 0))
    krow = pl.BlockSpec((seq, LANES), lambda b, i: (b, 0))
    return pl.pallas_call(
        functools.partial(_dsa_kernel, tq, topk, idx_bits),
        grid=(batch, n_q),
        in_specs=[qrow(iq.shape[1]), qrow(LANES), qrow(qa.shape[1])] + [krow] * 5,
        out_specs=qrow(qa.shape[1]),
        out_shape=jax.ShapeDtypeStruct((n, qa.shape[1]), BF16),
        scratch_shapes=[pltpu.VMEM((seq, tq), jnp.int32), pltpu.VMEM((seq, tq), jnp.int16),
                        pltpu.VMEM((seq, tq), jnp.int16), pltpu.VMEM((n_q, rows, tq), F32),
                        pltpu.VMEM((rows, LANES), F32), pltpu.VMEM((rows, LANES), F32),
                        pltpu.VMEM((rows, LANES), F32)],
        compiler_params=_cparams(("parallel", "arbitrary")),
    )(iq, misc, qa, ik0, ik1, ka0, ka1, va)


def _mla_kernel(tq, kb, q_ref, k_ref, v_ref, o_ref, lg_buf, m_buf, l_buf, acc_buf):
    i = pl.program_id(2)
    per_tile = tq // kb
    q = [q_ref[:, hh * LANES:(hh + 1) * LANES] for hh in range(2)]
    fold = lambda x, op: op(x[:, :LANES], x[:, LANES:]) if kb == 2 * LANES else x

    def rows_of(j):
        return pl.ds(pl.multiple_of(j * kb, kb), kb)

    def first_row(d):
        return 0 if d is None else d * kb

    def logits(js, diags):
        streams = [(u, hh) for u in range(len(js)) for hh in range(2)]
        lg = {}

        def stage_dot(s):
            u, hh = s
            lg[s] = lax.dot_general(q[hh][first_row(diags[u]):], k_ref[rows_of(js[u]), hh * LANES:(hh + 1) * LANES],
                                    _NT, preferred_element_type=F32)

        def stage_store(s):
            u, hh = s
            r0 = first_row(diags[u])
            x = lg[s]
            if diags[u] is not None:
                key_idx = lax.broadcasted_iota(jnp.int32, x.shape, 1) + diags[u] * kb
                query_idx = lax.broadcasted_iota(jnp.int32, x.shape, 0) + r0
                x = jnp.where(lax.shift_right_logical(key_idx, 6) <= lax.shift_right_logical(query_idx, 6),
                              x, -jnp.inf)
            lg_buf[hh, js[u], r0:, :] = x
            m_buf[hh, r0:, :] = jnp.maximum(m_buf[hh, r0:, :], fold(x, jnp.maximum))

        for t in range(len(streams) + 1):
            if t < len(streams):
                stage_dot(streams[t])
            if t >= 1:
                stage_store(streams[t - 1])

    def weighted(js, diags):
        streams = [(u, hh) for u in range(len(js)) for hh in range(2)]
        e = {}

        def stage_exp(s):
            u, hh = s
            r0 = first_row(diags[u])
            x = jnp.exp2(lg_buf[hh, js[u], r0:, :] - jnp.concatenate([m_buf[hh, r0:, :]] * (kb // LANES), axis=1))
            l_buf[hh, r0:, :] = l_buf[hh, r0:, :] + fold(x, jnp.add)
            e[s] = x.astype(BF16)

        def stage_pv(s):
            u, hh = s
            r0 = first_row(diags[u])
            acc_buf[hh, r0:, :] = acc_buf[hh, r0:, :] + jnp.dot(e[s], v_ref[rows_of(js[u]), :],
                                                              preferred_element_type=F32)

        for t in range(len(streams) + 1):
            if t < len(streams):
                stage_exp(streams[t])
            if t >= 1:
                stage_pv(streams[t - 1])

    diag = list(range(per_tile))
    n_full = i * per_tile
    full = lambda s: [s * per_tile + u for u in range(per_tile)]
    none = [None] * per_tile

    m_buf[...] = jnp.full(m_buf.shape, -jnp.inf, F32)
    logits([n_full + d for d in diag], diag)
    lax.fori_loop(0, i, lambda s, c: (logits(full(s), none), c)[1], 0)
    for hh in range(2):
        m_buf[hh] = jnp.broadcast_to(jnp.max(m_buf[hh], axis=-1, keepdims=True), (tq, LANES))
    l_buf[...] = jnp.zeros(l_buf.shape, F32)
    acc_buf[...] = jnp.zeros(acc_buf.shape, F32)
    weighted([n_full + d for d in diag], diag)
    lax.fori_loop(0, i, lambda s, c: (weighted(full(s), none), c)[1], 0)
    lane = _lane_iota((tq, LANES))
    outs = [acc_buf[hh] / jnp.sum(l_buf[hh], axis=-1, keepdims=True) for hh in range(2)]
    o_ref[...] = jnp.where(lane < HALF, outs[0], outs[1]).astype(BF16)


def _mla_attention(qb, kf, vb, batch, seq):
    n = qb.shape[0]
    tq = min(MLA_Q_TILE, seq)
    kb = min(MLA_KEY_BLOCK, tq)
    assert kb % LANES == 0 and kb // LANES in (1, 2) and kb % CHUNK == 0
    n_q = seq // tq
    n_pair = B_HEADS // 2
    return pl.pallas_call(
        functools.partial(_mla_kernel, tq, kb),
        grid=(batch, n_pair, n_q),
        in_specs=[pl.BlockSpec((tq, 2 * LANES), lambda b, p, i: (b * n_q + i, p)),
                  pl.BlockSpec((seq, 2 * LANES), lambda b, p, i: (b, p)),
                  pl.BlockSpec((seq, LANES), lambda b, p, i: (b, p))],
        out_specs=pl.BlockSpec((tq, LANES), lambda b, p, i: (b * n_q + i, p)),
        out_shape=jax.ShapeDtypeStruct((n, B_HEADS * B_V), BF16),
        scratch_shapes=[pltpu.VMEM((2, seq // kb, tq, kb), F32), pltpu.VMEM((2, tq, LANES), F32),
                        pltpu.VMEM((2, tq, LANES), F32), pltpu.VMEM((2, tq, LANES), F32)],
        compiler_params=_cparams(("parallel", "parallel", "arbitrary")),
    )(qb, kf, vb)


def _odd_proj_kernel(x_ref, mod_ref, w_ref, o_ref):
    sh = mod_ref[0, 0:1, :]
    sc = mod_ref[0, 1:2, :]
    h = (x_ref[...] * (1.0 + sc) + sh).astype(BF16)
    p = jnp.dot(h, w_ref[...], preferred_element_type=F32)
    dq = C_HEADS * C_HEAD_DIM
    o_ref[:, 0:dq] = (p[:, 0:dq] * (C_HEAD_DIM ** -0.5)).astype(BF16)
    o_ref[:, dq:] = p[:, dq:].astype(BF16)


def _odd_proj(x, mod_l, w_qkv, batch, seq):
    n, d = x.shape
    tm = min(TOKEN_TILE, seq)
    nt = seq // tm
    w = w_qkv.astype(BF16)
    return pl.pallas_call(
        _odd_proj_kernel,
        grid=(batch, nt),
        in_specs=[pl.BlockSpec((tm, d), lambda b, t: (b * nt + t, 0)),
                  pl.BlockSpec((1, 6, d), lambda b, t: (b, 0, 0)), _const_spec(w.shape)],
        out_specs=pl.BlockSpec((tm, w.shape[1]), lambda b, t: (b * nt + t, 0)),
        out_shape=jax.ShapeDtypeStruct((n, w.shape[1]), BF16),
        compiler_params=_cparams(("parallel", "parallel")),
    )(x, mod_l, w)


def _stick_kernel(tq, kb, q_ref, k_ref, v_ref, o_ref, acc_ref):
    i = pl.program_id(2)
    per_tile = tq // kb
    q2 = q_ref[...]
    lane = _lane_iota((tq, LANES))
    tri = jnp.where(lax.broadcasted_iota(jnp.int32, (kb, kb), 0) >= lax.broadcasted_iota(jnp.int32, (kb, kb), 1),
                    1.0, 0.0).astype(BF16)
    row = lax.broadcasted_iota(jnp.int32, (tq, kb), 0)
    col = lax.broadcasted_iota(jnp.int32, (tq, kb), 1)
    q_heads = (jnp.where(lane < HALF, q2, jnp.zeros_like(q2)), jnp.where(lane < HALF, jnp.zeros_like(q2), q2))
    acc_ref[...] = jnp.zeros(acc_ref.shape, F32)

    def blocks(js, carries, diags):
        streams = [(u, hh) for u in range(len(js)) for hh in range(2)]
        kv = []
        for j in js:
            start = pl.multiple_of(j * kb, kb)
            kv.append((k_ref[pl.ds(start, kb), :], v_ref[pl.ds(start, kb), :]))
        before = [None if d is None else (col + d * kb < row) for d in diags]
        z, inc = {}, {}
        carry = list(carries)

        r0 = [0 if d is None else d * kb for d in diags]

        def stage_z(s):
            z[s] = lax.dot_general(q_heads[s[1]][r0[s[0]]:], kv[s[0]][0], _NT, preferred_element_type=F32)

        def stage_sum(s):
            sp = jnp.where(z[s] > SOFTPLUS_LINEAR, z[s], jnp.log(1.0 + jnp.exp(z[s])))
            if before[s[0]] is not None:
                sp = jnp.where(before[s[0]][r0[s[0]]:], sp, 0.0)
            inc[s] = jnp.dot(sp.astype(BF16), tri, preferred_element_type=F32)

        def stage_out(s):
            u, hh = s
            a = jnp.exp(z[s] - inc[s] - carry[hh][r0[u]:])
            if before[u] is not None:
                a = jnp.where(before[u][r0[u]:], a, 0.0)
            acc_ref[hh, r0[u]:, :] = acc_ref[hh, r0[u]:, :] + jnp.dot(a.astype(BF16), kv[u][1],
                                                                      preferred_element_type=F32)
            tot = carry[hh][r0[u]:] + inc[s][:, 0:1]
            carry[hh] = tot if r0[u] == 0 else jnp.concatenate([carry[hh][:r0[u]], tot], axis=0)

        for t in range(len(streams) + 2):
            if t < len(streams):
                stage_z(streams[t])
            if 0 <= t - 1 < len(streams):
                stage_sum(streams[t - 1])
            if 0 <= t - 2 < len(streams):
                stage_out(streams[t - 2])
        return tuple(carry)

    carries = (jnp.zeros((tq, 1), F32), jnp.zeros((tq, 1), F32))
    diag = list(reversed(range(per_tile)))
    carries = blocks([i * per_tile + d for d in diag], carries, diag)
    n_full = i * per_tile
    lax.fori_loop(0, i, lambda s, cs: blocks([n_full - 1 - s * per_tile - u for u in range(per_tile)], cs,
                                             [None] * per_tile), carries)
    o_ref[...] = jnp.where(lane < HALF, acc_ref[0], acc_ref[1]).astype(BF16)


def _stick_attention(qkv, batch, seq):
    n = qkv.shape[0]
    tq = min(STICK_Q_TILE, seq)
    kb = min(STICK_KEY_BLOCK, tq)
    n_q = seq // tq
    n_pair = C_HEADS // 2
    return pl.pallas_call(
        functools.partial(_stick_kernel, tq, kb),
        grid=(batch, n_pair, n_q),
        in_specs=[pl.BlockSpec((tq, LANES), lambda b, p, i: (b * n_q + i, p)),
                  pl.BlockSpec((seq, LANES), lambda b, p, i: (b, n_pair + p)),
                  pl.BlockSpec((seq, LANES), lambda b, p, i: (b, 2 * n_pair + p))],
        out_specs=pl.BlockSpec((tq, LANES), lambda b, p, i: (b * n_q + i, p)),
        out_shape=jax.ShapeDtypeStruct((n, C_HEADS * C_HEAD_DIM), BF16),
        scratch_shapes=[pltpu.VMEM((2, tq, LANES), F32)],
        compiler_params=_cparams(("parallel", "parallel", "arbitrary")),
    )(qkv, qkv, qkv)


_HALO = 8


def _mix_ffn_kernel(tm, n_in, *refs):
    x_ref, mod_ref = refs[0], refs[1]
    ins = refs[2:2 + n_in]
    wos = refs[2 + n_in:2 + 2 * n_in]
    mg_ref, mb_ref, wup_ref, cw_ref, cb_ref, wdn_ref, g_ref, b_ref, o_ref, a_buf, tail_buf = refs[2 + 2 * n_in:]
    t = pl.program_id(1)
    mix = jnp.dot(ins[0][...], wos[0][...], preferred_element_type=F32)
    for a_ref, w_ref in zip(ins[1:], wos[1:]):
        mix = mix + jnp.dot(a_ref[...], w_ref[...], preferred_element_type=F32)
    x = _layer_norm(DEEPNORM_ALPHA * x_ref[...] + (1.0 + mod_ref[0, 2:3, :]) * mix, mg_ref[...], mb_ref[...])
    sh = mod_ref[0, 3:4, :]
    sc = mod_ref[0, 4:5, :]
    gate = mod_ref[0, 5:6, :]
    h = (x * (1.0 + sc) + sh).astype(BF16)

    @pl.when(t == 0)
    def _():
        tail_buf[...] = jnp.zeros_like(tail_buf)

    def up(ci):
        cs = slice(ci * FF_CHUNK, (ci + 1) * FF_CHUNK)
        gs = slice(D_FF + ci * FF_CHUNK, D_FF + (ci + 1) * FF_CHUNK)
        return (jnp.dot(h, wup_ref[:, cs], preferred_element_type=F32),
                jnp.dot(h, wup_ref[:, gs], preferred_element_type=F32))

    n_chunk = D_FF // FF_CHUNK
    y = jnp.zeros((tm, x.shape[1]), F32)
    nxt = up(0)
    for ci in range(n_chunk):
        cs = slice(ci * FF_CHUNK, (ci + 1) * FF_CHUNK)
        a, gt = nxt
        if ci + 1 < n_chunk:
            nxt = up(ci + 1)
        a_buf[0:_HALO, :] = tail_buf[:, cs]
        a_buf[_HALO:, :] = a
        tail_buf[:, cs] = a[tm - _HALO:, :]
        conv = (a_buf[_HALO - 2:_HALO - 2 + tm, :] * cw_ref[0:1, cs]
                + a_buf[_HALO - 1:_HALO - 1 + tm, :] * cw_ref[1:2, cs]
                + a * cw_ref[2:3, cs] + cb_ref[:, cs])
        u = conv / (1.0 + jnp.exp(-conv)) * gt
        y = y + jnp.dot(u.astype(BF16), wdn_ref[cs, :], preferred_element_type=F32)
    o_ref[...] = _layer_norm(DEEPNORM_ALPHA * x + (1.0 + gate) * y, g_ref[...], b_ref[...])


def _mix_ffn(x, mod_l, acts, w_out, mix_g, mix_b, w_up, conv_w, conv_b, w_down, ffn_g, ffn_b, batch, seq):
    n, d = x.shape
    tm = min(TOKEN_TILE, seq)
    nt = seq // tm
    row = lambda w: pl.BlockSpec((tm, w), lambda b, t: (b * nt + t, 0))
    single = lambda shape: pl.BlockSpec(shape, lambda b, t: (0,) * len(shape), pipeline_mode=pl.Buffered(1))
    wos, off = [], 0
    for a in acts:
        wos.append(w_out[off:off + a.shape[1]].astype(BF16))
        off += a.shape[1]
    vec = lambda v: v.reshape(1, -1)
    return pl.pallas_call(
        functools.partial(_mix_ffn_kernel, tm, len(acts)),
        grid=(batch, nt),
        in_specs=[row(d), pl.BlockSpec((1, 6, d), lambda b, t: (b, 0, 0))]
                 + [row(a.shape[1]) for a in acts] + [single(w.shape) for w in wos]
                 + [_const_spec((1, d))] * 2
                 + [single((d, 2 * D_FF)), _const_spec((CONV_WIDTH, D_FF)), _const_spec((1, D_FF)), single((D_FF, d)),
                    _const_spec((1, d)), _const_spec((1, d))],
        out_specs=row(d),
        out_shape=jax.ShapeDtypeStruct((n, d), F32),
        scratch_shapes=[pltpu.VMEM((_HALO + tm, FF_CHUNK), F32), pltpu.VMEM((_HALO, D_FF), F32)],
        compiler_params=_cparams(("parallel", "arbitrary")),
    )(x, mod_l, *acts, *wos, vec(mix_g), vec(mix_b), w_up.astype(BF16), conv_w, vec(conv_b), w_down.astype(BF16),
      vec(ffn_g), vec(ffn_b))


def kernel(x, c, positions, mod_w, mod_b, ln_mix_g, ln_mix_b, ln_ffn_g, ln_ffn_b, ev_w_in, ev_idx_k_g, ev_idx_k_b, ev_q_norm_g, ev_kv_norm_g, ev_w_uq, ev_w_ukv, ev_w_out, od_w_qkv, od_w_out, ffn_w_up, ffn_conv_w, ffn_conv_b, ffn_w_down):
    batch, seq, d = x.shape
    assert d == D_MODEL and seq % min(Q_TILE, seq) == 0 and seq % CHUNK == 0
    n = batch * seq
    depth = mod_w.shape[0]
    mod = _modulation(c, mod_w, mod_b).reshape(depth, batch, 6, d)
    tables = _rope_tables(positions)
    xs = x.reshape(n, d)
    for l in range(depth):
        i = l // 2
        if l % 2 == 0:
            weights = _even_weights(ev_w_in[i], ev_idx_k_g[i], ev_idx_k_b[i], ev_q_norm_g[i],
                                    ev_kv_norm_g[i], ev_w_uq[i], ev_w_ukv[i])
            qa, iq, ka0, ka1, va, ik0, ik1, misc, qb, kf, vb = _even_proj(xs, mod[l], tables, weights, batch, seq)
            out_a = _dsa_attention(iq, misc, qa, ik0, ik1, ka0, ka1, va, batch, seq)
            out_b = _mla_attention(qb, kf, vb, batch, seq)
            acts, w_out = [out_a, out_b], ev_w_out[i]
        else:
            qkv = _odd_proj(xs, mod[l], od_w_qkv[i], batch, seq)
            acts, w_out = [_stick_attention(qkv, batch, seq)], od_w_out[i]
        xs = _mix_ffn(xs, mod[l], acts, w_out, ln_mix_g[l], ln_mix_b[l], ffn_w_up[l], ffn_conv_w[l], ffn_conv_b[l],
                      ffn_w_down[l], ln_ffn_g[l], ln_ffn_b[l], batch, seq)
    return xs.reshape(batch, seq, d)
```

```python
import functools

import numpy as np
import jax
import jax.numpy as jnp
from jax import lax
from jax.experimental import pallas as pl
from jax.experimental.pallas import tpu as pltpu

F32 = jnp.float32
BF16 = jnp.bfloat16

D_MODEL = 1024
DEPTH = 4
CHUNK = 64
ROPE_THETA = 10000.0
LN_EPS = 1e-5
RMS_EPS = 1e-6

A_HEADS = 8
A_HEAD_DIM = 64
IDX_HEADS = 4
IDX_DIM = 64
TOPK_MAX = 256

B_HEADS = 8
B_NOPE = 64
B_ROPE = 32
B_V = 64
B_Q_RANK = 384
B_KV_RANK = 256

C_HEADS = 16
C_HEAD_DIM = D_MODEL // C_HEADS

D_FF = 2816
CONV_WIDTH = 3

EVEN_IN_SIZES = (A_HEADS * A_HEAD_DIM, A_HEAD_DIM, A_HEAD_DIM, IDX_HEADS * IDX_DIM, IDX_DIM, IDX_HEADS,
                 B_Q_RANK, B_KV_RANK, B_ROPE)
DEEPNORM_ALPHA = (2 * DEPTH) ** 0.25
LOG2E = 1.4426950408889634

LANES = 128
HALF = 64
TOKEN_TILE = 512
Q_TILE = 256
STICK_Q_TILE = 512
STICK_KEY_BLOCK = 256
MLA_Q_TILE = 1024
MLA_KEY_BLOCK = 256
STICK_UNROLL = 2
SOFTPLUS_LINEAR = 30.0
FF_CHUNK = 256
VMEM_LIMIT = 56 * 2 ** 20

_P_QA = 0
_P_IQ = _P_QA + A_HEADS * A_HEAD_DIM
_P_CQ = _P_IQ + IDX_HEADS * IDX_DIM
_P_CKV = _P_CQ + B_Q_RANK
_P_GK = _P_CKV + B_KV_RANK
_P_GV = _P_GK + LANES
_P_GI = _P_GV + LANES
_P_G5 = _P_GI + LANES
_P_END = _P_G5 + LANES
_KR_LO = HALF
_KR_HI = HALF + B_ROPE
_IW_LO = _KR_HI

_NT = (((1,), (1,)), ((), ()))


def _cparams(sem):
    return pltpu.CompilerParams(dimension_semantics=sem, vmem_limit_bytes=VMEM_LIMIT)


def _const_spec(shape):
    nd = len(shape)
    return pl.BlockSpec(shape, lambda *_: (0,) * nd)


def _lane_iota(shape):
    return lax.broadcasted_iota(jnp.int32, shape, len(shape) - 1)


def _layer_norm(v, g, b):
    mu = jnp.mean(v, axis=-1, keepdims=True)
    d = v - mu
    var = jnp.mean(d * d, axis=-1, keepdims=True)
    return d * lax.rsqrt(var + LN_EPS) * g + b


def _mod_kernel(c_ref, w_ref, b_ref, o_ref):
    c = c_ref[...]
    ca = c / (1.0 + jnp.exp(-c))
    o_ref[0] = jnp.dot(ca, w_ref[0], preferred_element_type=F32,
                       precision=lax.Precision.HIGHEST) + b_ref[0]


def _modulation(c, mod_w, mod_b):
    depth, d, d6 = mod_w.shape
    b = c.shape[0]
    nj = d6 // d
    return pl.pallas_call(
        _mod_kernel,
        grid=(depth, nj),
        in_specs=[pl.BlockSpec((b, d), lambda l, j: (0, 0)),
                  pl.BlockSpec((1, d, d), lambda l, j: (l, 0, j)),
                  pl.BlockSpec((1, 1, d), lambda l, j: (l, 0, j))],
        out_specs=pl.BlockSpec((1, b, d), lambda l, j: (l, 0, j)),
        out_shape=jax.ShapeDtypeStruct((depth, b, d6), F32),
        compiler_params=_cparams(("parallel", "parallel")),
    )(c, mod_w, mod_b.reshape(depth, 1, d6))


def _rope_kernel(pos_ref, fa_ref, ga_ref, fb_ref, gb_ref, ca_ref, sa_ref, cb_ref, sb_ref):
    pos = pos_ref[...]
    anga = pos * fa_ref[...]
    ca_ref[...] = jnp.cos(anga)
    sa_ref[...] = jnp.sin(anga) * ga_ref[...]
    angb = pos * fb_ref[...]
    cb_ref[...] = jnp.cos(angb)
    sb_ref[...] = jnp.sin(angb) * gb_ref[...]


def _rope_tables(positions):
    n = positions.size
    lane = np.arange(LANES)
    inv_a = ROPE_THETA ** (-jnp.arange(0, A_HEAD_DIM, 2, dtype=F32) / A_HEAD_DIM)
    inv_b = ROPE_THETA ** (-jnp.arange(0, B_ROPE, 2, dtype=F32) / B_ROPE)
    fa = inv_a[lane % (A_HEAD_DIM // 2)]
    ga = jnp.asarray(np.where(lane % A_HEAD_DIM < A_HEAD_DIM // 2, -1.0, 1.0), F32)
    in_rope = (lane >= _KR_LO) & (lane < _KR_HI)
    fb = jnp.where(in_rope, inv_b[(lane - _KR_LO) % (B_ROPE // 2)], 0.0)
    gb = jnp.asarray(np.where(in_rope, np.where(lane < _KR_LO + B_ROPE // 2, -1.0, 1.0), 0.0), F32)
    tm = 2048 if n % 2048 == 0 else TOKEN_TILE
    vec = lambda v: v.reshape(1, LANES).astype(F32)
    tab = jax.ShapeDtypeStruct((n, LANES), F32)
    return pl.pallas_call(
        _rope_kernel,
        grid=(n // tm,),
        in_specs=[pl.BlockSpec((tm, 1), lambda i: (i, 0))] + [_const_spec((1, LANES))] * 4,
        out_specs=[pl.BlockSpec((tm, LANES), lambda i: (i, 0))] * 4,
        out_shape=[tab] * 4,
        compiler_params=_cparams(("parallel",)),
    )(positions.reshape(n, 1).astype(F32), vec(fa), vec(ga), vec(fb), vec(gb))


def _rope_a(x, c, s):
    lane = _lane_iota(x.shape)
    partner = jnp.where(lane % A_HEAD_DIM < A_HEAD_DIM // 2,
                        pltpu.roll(x, LANES - A_HEAD_DIM // 2, 1), pltpu.roll(x, A_HEAD_DIM // 2, 1))
    return x * c + partner * s


def _rope_b(x, c, s):
    lane = _lane_iota(x.shape)
    partner = jnp.where(lane < _KR_LO + B_ROPE // 2,
                        pltpu.roll(x, LANES - B_ROPE // 2, 1), pltpu.roll(x, B_ROPE // 2, 1))
    return x * c + partner * s


def _even_proj_kernel(x_ref, mod_ref, win_ref, ca_ref, sa_ref, cb_ref, sb_ref, ikg_ref, ikb_ref,
                      qg_ref, kvg_ref, wuq_ref, wkn_ref, wv_ref,
                      qa_ref, iq_ref, ka0_ref, ka1_ref, va_ref, ik0_ref, ik1_ref, misc_ref,
                      qb_ref, kf_ref, vb_ref):
    sh = mod_ref[0, 0:1, :]
    sc = mod_ref[0, 1:2, :]
    h = (x_ref[...] * (1.0 + sc) + sh).astype(BF16)
    ca, sa, cb, sb = ca_ref[...], sa_ref[...], cb_ref[...], sb_ref[...]
    lane = _lane_iota(ca.shape)
    low = lane < HALF
    kr = []

    def rms(v, g_ref):
        return (v * lax.rsqrt(jnp.mean(v * v, axis=-1, keepdims=True) + RMS_EPS) * g_ref[...]).astype(BF16)

    def misc_group(p):
        g5 = _rope_b(p, cb, sb)
        misc_ref[...] = g5 * (IDX_HEADS ** -0.5)
        kr.append(jnp.where((lane >= _KR_LO) & (lane < _KR_HI), g5, 0.0))

    def kv_latent(p):
        ckvn = rms(p, kvg_ref)
        kn = jnp.dot(ckvn, wkn_ref[...], preferred_element_type=F32)
        for hd in range(B_HEADS):
            sl = slice(hd * LANES, (hd + 1) * LANES)
            kf_ref[:, sl] = (kn[:, sl] + kr[0]).astype(BF16)
        vb_ref[...] = jnp.dot(ckvn, wv_ref[...], preferred_element_type=F32).astype(BF16)

    def q_latent(p):
        qb = jnp.dot(rms(p, qg_ref), wuq_ref[...], preferred_element_type=F32)
        scale_b = (B_NOPE + B_ROPE) ** -0.5 * LOG2E
        for hd in range(B_HEADS):
            sl = slice(hd * LANES, (hd + 1) * LANES)
            qb_ref[:, sl] = (_rope_b(qb[:, sl], cb, sb) * scale_b).astype(BF16)

    def roped_heads(o_ref, scale):
        def store(p):
            for g in range(p.shape[1] // LANES):
                sl = slice(g * LANES, (g + 1) * LANES)
                o_ref[:, sl] = (_rope_a(p[:, sl], ca, sa) * scale).astype(BF16)
        return store

    def dsa_key(p):
        ka = _rope_a(p, ca, sa)
        ka0_ref[...] = jnp.where(low, ka, 0.0).astype(BF16)
        ka1_ref[...] = jnp.where(low, 0.0, ka).astype(BF16)

    def dsa_value(p):
        va_ref[...] = p.astype(BF16)

    def indexer_key(p):
        mu = jnp.sum(jnp.where(low, p, 0.0), axis=-1, keepdims=True) * (1.0 / IDX_DIM)
        d = p - mu
        var = jnp.sum(jnp.where(low, d * d, 0.0), axis=-1, keepdims=True) * (1.0 / IDX_DIM)
        ik = _rope_a(d * lax.rsqrt(var + LN_EPS) * ikg_ref[...] + ikb_ref[...], ca, sa)
        ik0_ref[...] = jnp.where(low, ik, 0.0).astype(BF16)
        ik1_ref[...] = jnp.where(low, 0.0, ik).astype(BF16)

    both = lambda first, second: lambda p: (first(p[:, :LANES]), second(p[:, LANES:]))
    stages = [(_P_GI, 2 * LANES, both(indexer_key, misc_group)), (_P_CKV, B_KV_RANK, kv_latent),
              (_P_CQ, B_Q_RANK, q_latent),
              (_P_QA, A_HEADS * A_HEAD_DIM, roped_heads(qa_ref, A_HEAD_DIM ** -0.5 * LOG2E)),
              (_P_IQ, IDX_HEADS * IDX_DIM, roped_heads(iq_ref, IDX_DIM ** -0.5)),
              (_P_GK, 2 * LANES, both(dsa_key, dsa_value))]
    project = lambda off, width: jnp.dot(h, win_ref[:, off:off + width], preferred_element_type=F32)
    nxt = project(*stages[0][:2])
    for s, (_, _, epilogue) in enumerate(stages):
        cur = nxt
        if s + 1 < len(stages):
            nxt = project(*stages[s + 1][:2])
        epilogue(cur)


def _even_weights(w_in, idx_k_g, idx_k_b, q_norm_g, kv_norm_g, w_uq, w_ukv):
    d = w_in.shape[0]
    qa, ka, va, iq, ik, iw, cq, ckv, kr = jnp.split(w_in, np.cumsum(EVEN_IN_SIZES)[:-1].tolist(), axis=1)
    z = lambda n: jnp.zeros((d, n), w_in.dtype)
    win = jnp.concatenate([qa, iq, cq, ckv, ka, ka, va, va, ik, ik,
                           z(_KR_LO), kr, iw, z(LANES - _IW_LO - IDX_HEADS)], axis=1).astype(BF16)
    assert win.shape[1] == _P_END
    pad_head = B_NOPE + B_ROPE
    wuq = jnp.pad(w_uq.reshape(B_Q_RANK, B_HEADS, pad_head), ((0, 0), (0, 0), (0, LANES - pad_head)))
    wuq = wuq.reshape(B_Q_RANK, B_HEADS * LANES).astype(BF16)
    wkv = w_ukv.reshape(B_KV_RANK, B_HEADS, B_NOPE + B_V)
    wkn = jnp.pad(wkv[:, :, :B_NOPE], ((0, 0), (0, 0), (0, LANES - B_NOPE)))
    wkn = wkn.reshape(B_KV_RANK, B_HEADS * LANES).astype(BF16)
    wv = wkv[:, :, B_NOPE:].reshape(B_KV_RANK, B_HEADS * B_V).astype(BF16)
    two = lambda v: jnp.concatenate([v, v]).reshape(1, LANES).astype(F32)
    return (win, two(idx_k_g), two(idx_k_b), q_norm_g.reshape(1, -1).astype(F32),
            kv_norm_g.reshape(1, -1).astype(F32), wuq, wkn, wv)


def _even_proj(x, mod_l, tables, weights, batch, seq):
    n, d = x.shape
    tm = min(TOKEN_TILE, seq)
    nt = seq // tm
    win, ikg, ikb, qg, kvg, wuq, wkn, wv = weights
    row = lambda w: pl.BlockSpec((tm, w), lambda b, t: (b * nt + t, 0))
    bf = lambda w: jax.ShapeDtypeStruct((n, w), BF16)
    out_widths = [A_HEADS * A_HEAD_DIM, IDX_HEADS * IDX_DIM, LANES, LANES, LANES, LANES, LANES]
    out_shape = [bf(w) for w in out_widths] + [jax.ShapeDtypeStruct((n, LANES), F32)] + \
                [bf(B_HEADS * LANES), bf(B_HEADS * LANES), bf(B_HEADS * B_V)]
    out_specs = [row(w) for w in out_widths] + [row(LANES)] + \
                [row(B_HEADS * LANES), row(B_HEADS * LANES), row(B_HEADS * B_V)]
    return pl.pallas_call(
        _even_proj_kernel,
        grid=(batch, nt),
        in_specs=[row(d), pl.BlockSpec((1, 6, d), lambda b, t: (b, 0, 0)), _const_spec(win.shape)]
                 + [row(LANES)] * 4
                 + [_const_spec(a.shape) for a in (ikg, ikb, qg, kvg, wuq, wkn, wv)],
        out_specs=out_specs,
        out_shape=out_shape,
        compiler_params=_cparams(("parallel", "parallel")),
    )(x, mod_l, win, *tables, ikg, ikb, qg, kvg, wuq, wkn, wv)


def _dsa_kernel(tq, topk, iq_ref, misc_ref, qa_ref, ik0_ref, ik1_ref, ka0_ref, ka1_ref, va_ref,
                o_ref, key_buf, hi_buf, lo_buf, lg_buf, m_buf, l_buf, acc_buf):
    i = pl.program_id(1)
    n_blk = i + 1
    int_min = jnp.int32(-2 ** 31)
    min16 = jnp.int16(-2 ** 15)
    one16, zero16 = jnp.int16(1), jnp.int16(0)
    kf = jnp.float32(topk)
    n_pair = A_HEADS // 2
    rows = n_pair * tq

    def rows_of(j):
        return pl.ds(pl.multiple_of(j * tq, tq), tq)

    iw_t = misc_ref[...].T
    iq = iq_ref[...]
    t_idx = lax.broadcasted_iota(jnp.int32, (tq, tq), 1) + i * tq

    def score_block(j, _):
        score = jnp.zeros((tq, tq), F32)
        for pair in range(IDX_HEADS // 2):
            iq_pair = iq[:, pair * LANES:(pair + 1) * LANES]
            for half, ik_ref in enumerate((ik0_ref, ik1_ref)):
                hd = 2 * pair + half
                rel = lax.dot_general(ik_ref[rows_of(j), :], iq_pair, _NT, preferred_element_type=F32)
                score = score + jnp.maximum(rel, 0.0) * iw_t[_IW_LO + hd:_IW_LO + hd + 1, :]
        score = jnp.where(score == 0.0, 0.0, score)
        bits = lax.bitcast_convert_type(score, jnp.int32)
        key = bits ^ (lax.shift_right_arithmetic(bits, 31) & jnp.int32(0x7FFFFFFF))
        s_idx = lax.broadcasted_iota(jnp.int32, (tq, tq), 0) + j * tq
        adm = lax.shift_right_logical(s_idx, 6) <= lax.shift_right_logical(t_idx, 6)
        key = jnp.where(adm, key, int_min)
        key_buf[rows_of(j), :] = key
        hi_buf[rows_of(j), :] = lax.shift_right_arithmetic(key, 16).astype(jnp.int16)
        return 0

    lax.fori_loop(0, n_blk, score_block, 0)

    def count(pred, buf):
        def body(j, acc):
            hit = jnp.where(pred(buf[rows_of(j), :]), one16, zero16)
            for r in range(tq // 16):
                acc = acc + hit[r * 16:(r + 1) * 16]
            return acc
        acc = lax.fori_loop(0, n_blk, body, jnp.zeros((16, tq), jnp.int16))
        return jnp.sum(acc.astype(jnp.int32), axis=0, keepdims=True).astype(F32)

    def search(buf, base):
        def bit(b, cu):
            cand = cu | lax.shift_left(jnp.int32(1), 15 - b)
            image = (cand ^ jnp.int32(0x8000)).astype(jnp.int16)
            return jnp.where(base + count(lambda x: x >= image, buf) >= kf, cand, cu)

        return lax.fori_loop(0, 16, bit, jnp.zeros((1, tq), jnp.int32))

    hi_cu = search(hi_buf, 0.0)
    hi16 = (hi_cu ^ jnp.int32(0x8000)).astype(jnp.int16)

    def low_prep(j, _):
        low = ((key_buf[rows_of(j), :] & jnp.int32(0xFFFF)) ^ jnp.int32(0x8000)).astype(jnp.int16)
        lo_buf[rows_of(j), :] = jnp.where(hi_buf[rows_of(j), :] == hi16, low, min16)
        return 0

    lax.fori_loop(0, n_blk, low_prep, 0)
    n_above = count(lambda x: x > hi16, hi_buf)
    lo_cu = search(lo_buf, n_above)
    lo16 = (lo_cu ^ jnp.int32(0x8000)).astype(jnp.int16)
    thr = lax.shift_left(hi_cu ^ jnp.int32(0x8000), 16) | lo_cu
    need = kf - (n_above + count(lambda x: x > lo16, lo_buf))
    tri = jnp.where(lax.broadcasted_iota(jnp.int32, (tq, tq), 0) >= lax.broadcasted_iota(jnp.int32, (tq, tq), 1),
                    1.0, 0.0).astype(BF16)

    qa = qa_ref[...]
    q_stack = jnp.concatenate([qa[:, g * LANES:(g + 1) * LANES] for g in range(n_pair)], axis=0)
    m_buf[...] = jnp.full(m_buf.shape, -jnp.inf, F32)

    def logits_block(j, ties_before):
        k = key_buf[rows_of(j), :]
        tied = jnp.where(k == thr, 1.0, 0.0)
        rank = jnp.dot(tri, tied.astype(BF16), preferred_element_type=F32) + ties_before
        keep = jnp.where(k == int_min, 0.0,
                         jnp.where(k > thr, 1.0, jnp.where(rank <= need, tied, 0.0)))
        bias = jnp.where(keep.T > 0.5, 0.0, -jnp.inf)
        bias = jnp.concatenate([bias] * n_pair, axis=0)
        lgs = [lax.dot_general(q_stack, ka_ref[rows_of(j), :], _NT, preferred_element_type=F32)
               for ka_ref in (ka0_ref, ka1_ref)]
        for half, lg in enumerate(lgs):
            sl = slice(half * rows, (half + 1) * rows)
            lg = lg + bias
            lg_buf[j, sl, :] = lg
            folded = jnp.maximum(lg[:, :LANES], lg[:, LANES:]) if tq == 2 * LANES else lg
            m_buf[sl, :] = jnp.maximum(m_buf[sl, :], folded)
        return rank[tq - 1:tq, :]

    lax.fori_loop(0, n_blk, logits_block, jnp.zeros((1, tq), F32))
    m_buf[...] = jnp.broadcast_to(jnp.max(m_buf[...], axis=-1, keepdims=True), m_buf.shape)
    l_buf[...] = jnp.zeros(l_buf.shape, F32)
    acc_buf[...] = jnp.zeros(acc_buf.shape, F32)

    def pv_block(j, _):
        v = va_ref[rows_of(j), :]
        for half in range(2):
            sl = slice(half * rows, (half + 1) * rows)
            m = m_buf[sl, :]
            e = jnp.exp2(lg_buf[j, sl, :] - jnp.concatenate([m] * (tq // LANES), axis=1))
            l_buf[sl, :] = l_buf[sl, :] + (e[:, :LANES] + e[:, LANES:] if tq == 2 * LANES else e)
            acc_buf[sl, :] = acc_buf[sl, :] + jnp.dot(e.astype(BF16), v, preferred_element_type=F32)
        return 0

    lax.fori_loop(0, n_blk, pv_block, 0)
    out = acc_buf[...] / jnp.sum(l_buf[...], axis=-1, keepdims=True)
    lane = _lane_iota((tq, LANES))
    for g in range(n_pair):
        o_ref[:, g * LANES:(g + 1) * LANES] = jnp.where(
            lane < HALF, out[g * tq:(g + 1) * tq], out[rows + g * tq:rows + (g + 1) * tq]).astype(BF16)


def _dsa_attention(iq, misc, qa, ik0, ik1, ka0, ka1, va, batch, seq):
    n = qa.shape[0]
    tq = min(Q_TILE, seq)
    assert tq % LANES == 0 and tq // LANES in (1, 2) and seq <= 2 ** 15
    n_q = seq // tq
    topk = min(TOPK_MAX, seq // 4)
    rows = A_HEADS * tq
    qrow = lambda w: pl.BlockSpec((tq, w), lambda b, i: (b * n_q + i, 0))
    krow = pl.BlockSpec((seq, LANES), lambda b, i: (b, 0))
    return pl.pallas_call(
        functools.partial(_dsa_kernel, tq, topk),
        grid=(batch, n_q),
        in_specs=[qrow(iq.shape[1]), qrow(LANES), qrow(qa.shape[1])] + [krow] * 5,
        out_specs=qrow(qa.shape[1]),
        out_shape=jax.ShapeDtypeStruct((n, qa.shape[1]), BF16),
        scratch_shapes=[pltpu.VMEM((seq, tq), jnp.int32), pltpu.VMEM((seq, tq), jnp.int16),
                        pltpu.VMEM((seq, tq), jnp.int16), pltpu.VMEM((n_q, rows, tq), F32),
                        pltpu.VMEM((rows, LANES), F32), pltpu.VMEM((rows, LANES), F32),
                        pltpu.VMEM((rows, LANES), F32)],
        compiler_params=_cparams(("parallel", "arbitrary")),
    )(iq, misc, qa, ik0, ik1, ka0, ka1, va)


def _mla_kernel(tq, kb, q_ref, k_ref, v_ref, o_ref, lg_buf, m_buf, l_buf, acc_buf):
    i = pl.program_id(2)
    per_tile = tq // kb
    q = [q_ref[:, hh * LANES:(hh + 1) * LANES] for hh in range(2)]
    fold = lambda x, op: op(x[:, :LANES], x[:, LANES:]) if kb == 2 * LANES else x

    def rows_of(j):
        return pl.ds(pl.multiple_of(j * kb, kb), kb)

    def first_row(d):
        return 0 if d is None else d * kb

    def logits(js, diags):
        streams = [(u, hh) for u in range(len(js)) for hh in range(2)]
        lg = {}

        def stage_dot(s):
            u, hh = s
            lg[s] = lax.dot_general(q[hh][first_row(diags[u]):], k_ref[rows_of(js[u]), hh * LANES:(hh + 1) * LANES],
                                    _NT, preferred_element_type=F32)

        def stage_store(s):
            u, hh = s
            r0 = first_row(diags[u])
            x = lg[s]
            if diags[u] is not None:
                key_idx = lax.broadcasted_iota(jnp.int32, x.shape, 1) + diags[u] * kb
                query_idx = lax.broadcasted_iota(jnp.int32, x.shape, 0) + r0
                x = jnp.where(lax.shift_right_logical(key_idx, 6) <= lax.shift_right_logical(query_idx, 6),
                              x, -jnp.inf)
            lg_buf[hh, js[u], r0:, :] = x
            m_buf[hh, r0:, :] = jnp.maximum(m_buf[hh, r0:, :], fold(x, jnp.maximum))

        for t in range(len(streams) + 1):
            if t < len(streams):
                stage_dot(streams[t])
            if t >= 1:
                stage_store(streams[t - 1])

    def weighted(js, diags):
        streams = [(u, hh) for u in range(len(js)) for hh in range(2)]
        e = {}

        def stage_exp(s):
            u, hh = s
            r0 = first_row(diags[u])
            x = jnp.exp2(lg_buf[hh, js[u], r0:, :] - jnp.concatenate([m_buf[hh, r0:, :]] * (kb // LANES), axis=1))
            l_buf[hh, r0:, :] = l_buf[hh, r0:, :] + fold(x, jnp.add)
            e[s] = x.astype(BF16)

        def stage_pv(s):
            u, hh = s
            r0 = first_row(diags[u])
            acc_buf[hh, r0:, :] = acc_buf[hh, r0:, :] + jnp.dot(e[s], v_ref[rows_of(js[u]), :],
                                                              preferred_element_type=F32)

        for t in range(len(streams) + 1):
            if t < len(streams):
                stage_exp(streams[t])
            if t >= 1:
                stage_pv(streams[t - 1])

    diag = list(range(per_tile))
    n_full = i * per_tile
    full = lambda s: [s * per_tile + u for u in range(per_tile)]
    none = [None] * per_tile

    m_buf[...] = jnp.full(m_buf.shape, -jnp.inf, F32)
    logits([n_full + d for d in diag], diag)
    lax.fori_loop(0, i, lambda s, c: (logits(full(s), none), c)[1], 0)
    for hh in range(2):
        m_buf[hh] = jnp.broadcast_to(jnp.max(m_buf[hh], axis=-1, keepdims=True), (tq, LANES))
    l_buf[...] = jnp.zeros(l_buf.shape, F32)
    acc_buf[...] = jnp.zeros(acc_buf.shape, F32)
    weighted([n_full + d for d in diag], diag)
    lax.fori_loop(0, i, lambda s, c: (weighted(full(s), none), c)[1], 0)
    lane = _lane_iota((tq, LANES))
    outs = [acc_buf[hh] / jnp.sum(l_buf[hh], axis=-1, keepdims=True) for hh in range(2)]
    o_ref[...] = jnp.where(lane < HALF, outs[0], outs[1]).astype(BF16)


def _mla_attention(qb, kf, vb, batch, seq):
    n = qb.shape[0]
    tq = min(MLA_Q_TILE, seq)
    kb = min(MLA_KEY_BLOCK, tq)
    assert kb % LANES == 0 and kb // LANES in (1, 2) and kb % CHUNK == 0
    n_q = seq // tq
    n_pair = B_HEADS // 2
    return pl.pallas_call(
        functools.partial(_mla_kernel, tq, kb),
        grid=(batch, n_pair, n_q),
        in_specs=[pl.BlockSpec((tq, 2 * LANES), lambda b, p, i: (b * n_q + i, p)),
                  pl.BlockSpec((seq, 2 * LANES), lambda b, p, i: (b, p)),
                  pl.BlockSpec((seq, LANES), lambda b, p, i: (b, p))],
        out_specs=pl.BlockSpec((tq, LANES), lambda b, p, i: (b * n_q + i, p)),
        out_shape=jax.ShapeDtypeStruct((n, B_HEADS * B_V), BF16),
        scratch_shapes=[pltpu.VMEM((2, seq // kb, tq, kb), F32), pltpu.VMEM((2, tq, LANES), F32),
                        pltpu.VMEM((2, tq, LANES), F32), pltpu.VMEM((2, tq, LANES), F32)],
        compiler_params=_cparams(("parallel", "parallel", "arbitrary")),
    )(qb, kf, vb)


def _odd_proj_kernel(x_ref, mod_ref, w_ref, o_ref):
    sh = mod_ref[0, 0:1, :]
    sc = mod_ref[0, 1:2, :]
    h = (x_ref[...] * (1.0 + sc) + sh).astype(BF16)
    p = jnp.dot(h, w_ref[...], preferred_element_type=F32)
    dq = C_HEADS * C_HEAD_DIM
    o_ref[:, 0:dq] = (p[:, 0:dq] * (C_HEAD_DIM ** -0.5)).astype(BF16)
    o_ref[:, dq:] = p[:, dq:].astype(BF16)


def _odd_proj(x, mod_l, w_qkv, batch, seq):
    n, d = x.shape
    tm = min(TOKEN_TILE, seq)
    nt = seq // tm
    w = w_qkv.astype(BF16)
    return pl.pallas_call(
        _odd_proj_kernel,
        grid=(batch, nt),
        in_specs=[pl.BlockSpec((tm, d), lambda b, t: (b * nt + t, 0)),
                  pl.BlockSpec((1, 6, d), lambda b, t: (b, 0, 0)), _const_spec(w.shape)],
        out_specs=pl.BlockSpec((tm, w.shape[1]), lambda b, t: (b * nt + t, 0)),
        out_shape=jax.ShapeDtypeStruct((n, w.shape[1]), BF16),
        compiler_params=_cparams(("parallel", "parallel")),
    )(x, mod_l, w)


def _stick_kernel(tq, kb, q_ref, k_ref, v_ref, o_ref, acc_ref):
    i = pl.program_id(2)
    per_tile = tq // kb
    q2 = q_ref[...]
    lane = _lane_iota((tq, LANES))
    tri = jnp.where(lax.broadcasted_iota(jnp.int32, (kb, kb), 0) >= lax.broadcasted_iota(jnp.int32, (kb, kb), 1),
                    1.0, 0.0).astype(BF16)
    row = lax.broadcasted_iota(jnp.int32, (tq, kb), 0)
    col = lax.broadcasted_iota(jnp.int32, (tq, kb), 1)
    q_heads = (jnp.where(lane < HALF, q2, jnp.zeros_like(q2)), jnp.where(lane < HALF, jnp.zeros_like(q2), q2))
    acc_ref[...] = jnp.zeros(acc_ref.shape, F32)

    def blocks(js, carries, diags):
        streams = [(u, hh) for u in range(len(js)) for hh in range(2)]
        kv = []
        for j in js:
            start = pl.multiple_of(j * kb, kb)
            kv.append((k_ref[pl.ds(start, kb), :], v_ref[pl.ds(start, kb), :]))
        before = [None if d is None else (col + d * kb < row) for d in diags]
        z, inc = {}, {}
        carry = list(carries)

        r0 = [0 if d is None else d * kb for d in diags]

        def stage_z(s):
            z[s] = lax.dot_general(q_heads[s[1]][r0[s[0]]:], kv[s[0]][0], _NT, preferred_element_type=F32)

        def stage_sum(s):
            sp = jnp.where(z[s] > SOFTPLUS_LINEAR, z[s], jnp.log(1.0 + jnp.exp(z[s])))
            if before[s[0]] is not None:
                sp = jnp.where(before[s[0]][r0[s[0]]:], sp, 0.0)
            inc[s] = jnp.dot(sp.astype(BF16), tri, preferred_element_type=F32)

        def stage_out(s):
            u, hh = s
            a = jnp.exp(z[s] - inc[s] - carry[hh][r0[u]:])
            if before[u] is not None:
                a = jnp.where(before[u][r0[u]:], a, 0.0)
            acc_ref[hh, r0[u]:, :] = acc_ref[hh, r0[u]:, :] + jnp.dot(a.astype(BF16), kv[u][1],
                                                                      preferred_element_type=F32)
            tot = carry[hh][r0[u]:] + inc[s][:, 0:1]
            carry[hh] = tot if r0[u] == 0 else jnp.concatenate([carry[hh][:r0[u]], tot], axis=0)

        for t in range(len(streams) + 2):
            if t < len(streams):
                stage_z(streams[t])
            if 0 <= t - 1 < len(streams):
                stage_sum(streams[t - 1])
            if 0 <= t - 2 < len(streams):
                stage_out(streams[t - 2])
        return tuple(carry)

    carries = (jnp.zeros((tq, 1), F32), jnp.zeros((tq, 1), F32))
    diag = list(reversed(range(per_tile)))
    carries = blocks([i * per_tile + d for d in diag], carries, diag)
    n_full = i * per_tile
    wide = STICK_UNROLL * per_tile
    n_wide = n_full // wide
    carries = lax.fori_loop(0, n_wide, lambda s, cs: blocks([n_full - 1 - s * wide - u for u in range(wide)], cs,
                                                            [None] * wide), carries)
    rest = n_full - n_wide * wide
    lax.fori_loop(0, rest // per_tile, lambda s, cs: blocks([rest - 1 - s * per_tile - u for u in range(per_tile)],
                                                            cs, [None] * per_tile), carries)
    o_ref[...] = jnp.where(lane < HALF, acc_ref[0], acc_ref[1]).astype(BF16)


def _stick_attention(qkv, batch, seq):
    n = qkv.shape[0]
    tq = min(STICK_Q_TILE, seq)
    kb = min(STICK_KEY_BLOCK, tq)
    n_q = seq // tq
    n_pair = C_HEADS // 2
    return pl.pallas_call(
        functools.partial(_stick_kernel, tq, kb),
        grid=(batch, n_pair, n_q),
        in_specs=[pl.BlockSpec((tq, LANES), lambda b, p, i: (b * n_q + i, p)),
                  pl.BlockSpec((seq, LANES), lambda b, p, i: (b, n_pair + p)),
                  pl.BlockSpec((seq, LANES), lambda b, p, i: (b, 2 * n_pair + p))],
        out_specs=pl.BlockSpec((tq, LANES), lambda b, p, i: (b * n_q + i, p)),
        out_shape=jax.ShapeDtypeStruct((n, C_HEADS * C_HEAD_DIM), BF16),
        scratch_shapes=[pltpu.VMEM((2, tq, LANES), F32)],
        compiler_params=_cparams(("parallel", "parallel", "arbitrary")),
    )(qkv, qkv, qkv)


_HALO = 8


def _mix_ffn_kernel(tm, n_in, *refs):
    x_ref, mod_ref = refs[0], refs[1]
    ins = refs[2:2 + n_in]
    wos = refs[2 + n_in:2 + 2 * n_in]
    mg_ref, mb_ref, wup_ref, cw_ref, cb_ref, wdn_ref, g_ref, b_ref, o_ref, a_buf, tail_buf = refs[2 + 2 * n_in:]
    t = pl.program_id(1)
    mix = jnp.dot(ins[0][...], wos[0][...], preferred_element_type=F32)
    for a_ref, w_ref in zip(ins[1:], wos[1:]):
        mix = mix + jnp.dot(a_ref[...], w_ref[...], preferred_element_type=F32)
    x = _layer_norm(DEEPNORM_ALPHA * x_ref[...] + (1.0 + mod_ref[0, 2:3, :]) * mix, mg_ref[...], mb_ref[...])
    sh = mod_ref[0, 3:4, :]
    sc = mod_ref[0, 4:5, :]
    gate = mod_ref[0, 5:6, :]
    h = (x * (1.0 + sc) + sh).astype(BF16)

    @pl.when(t == 0)
    def _():
        tail_buf[...] = jnp.zeros_like(tail_buf)

    def up(ci):
        cs = slice(ci * FF_CHUNK, (ci + 1) * FF_CHUNK)
        gs = slice(D_FF + ci * FF_CHUNK, D_FF + (ci + 1) * FF_CHUNK)
        return (jnp.dot(h, wup_ref[:, cs], preferred_element_type=F32),
                jnp.dot(h, wup_ref[:, gs], preferred_element_type=F32))

    n_chunk = D_FF // FF_CHUNK
    y = jnp.zeros((tm, x.shape[1]), F32)
    nxt = up(0)
    for ci in range(n_chunk):
        cs = slice(ci * FF_CHUNK, (ci + 1) * FF_CHUNK)
        a, gt = nxt
        if ci + 1 < n_chunk:
            nxt = up(ci + 1)
        a_buf[0:_HALO, :] = tail_buf[:, cs]
        a_buf[_HALO:, :] = a
        tail_buf[:, cs] = a[tm - _HALO:, :]
        conv = (a_buf[_HALO - 2:_HALO - 2 + tm, :] * cw_ref[0:1, cs]
                + a_buf[_HALO - 1:_HALO - 1 + tm, :] * cw_ref[1:2, cs]
                + a * cw_ref[2:3, cs] + cb_ref[:, cs])
        u = conv / (1.0 + jnp.exp(-conv)) * gt
        y = y + jnp.dot(u.astype(BF16), wdn_ref[cs, :], preferred_element_type=F32)
    o_ref[...] = _layer_norm(DEEPNORM_ALPHA * x + (1.0 + gate) * y, g_ref[...], b_ref[...])


def _mix_ffn(x, mod_l, acts, w_out, mix_g, mix_b, w_up, conv_w, conv_b, w_down, ffn_g, ffn_b, batch, seq):
    n, d = x.shape
    tm = min(TOKEN_TILE, seq)
    nt = seq // tm
    row = lambda w: pl.BlockSpec((tm, w), lambda b, t: (b * nt + t, 0))
    single = lambda shape: pl.BlockSpec(shape, lambda b, t: (0,) * len(shape), pipeline_mode=pl.Buffered(1))
    wos, off = [], 0
    for a in acts:
        wos.append(w_out[off:off + a.shape[1]].astype(BF16))
        off += a.shape[1]
    vec = lambda v: v.reshape(1, -1)
    return pl.pallas_call(
        functools.partial(_mix_ffn_kernel, tm, len(acts)),
        grid=(batch, nt),
        in_specs=[row(d), pl.BlockSpec((1, 6, d), lambda b, t: (b, 0, 0))]
                 + [row(a.shape[1]) for a in acts] + [single(w.shape) for w in wos]
                 + [_const_spec((1, d))] * 2
                 + [single((d, 2 * D_FF)), _const_spec((CONV_WIDTH, D_FF)), _const_spec((1, D_FF)), single((D_FF, d)),
                    _const_spec((1, d)), _const_spec((1, d))],
        out_specs=row(d),
        out_shape=jax.ShapeDtypeStruct((n, d), F32),
        scratch_shapes=[pltpu.VMEM((_HALO + tm, FF_CHUNK), F32), pltpu.VMEM((_HALO, D_FF), F32)],
        compiler_params=_cparams(("parallel", "arbitrary")),
    )(x, mod_l, *acts, *wos, vec(mix_g), vec(mix_b), w_up.astype(BF16), conv_w, vec(conv_b), w_down.astype(BF16),
      vec(ffn_g), vec(ffn_b))


def kernel(x, c, positions, mod_w, mod_b, ln_mix_g, ln_mix_b, ln_ffn_g, ln_ffn_b, ev_w_in, ev_idx_k_g, ev_idx_k_b, ev_q_norm_g, ev_kv_norm_g, ev_w_uq, ev_w_ukv, ev_w_out, od_w_qkv, od_w_out, ffn_w_up, ffn_conv_w, ffn_conv_b, ffn_w_down):
    batch, seq, d = x.shape
    assert d == D_MODEL and seq % min(Q_TILE, seq) == 0 and seq % CHUNK == 0
    n = batch * seq
    depth = mod_w.shape[0]
    mod = _modulation(c, mod_w, mod_b).reshape(depth, batch, 6, d)
    tables = _rope_tables(positions)
    xs = x.reshape(n, d)
    for l in range(depth):
        i = l // 2
        if l % 2 == 0:
            weights = _even_weights(ev_w_in[i], ev_idx_k_g[i], ev_idx_k_b[i], ev_q_norm_g[i],
                                    ev_kv_norm_g[i], ev_w_uq[i], ev_w_ukv[i])
            qa, iq, ka0, ka1, va, ik0, ik1, misc, qb, kf, vb = _even_proj(xs, mod[l], tables, weights, batch, seq)
            out_a = _dsa_attention(iq, misc, qa, ik0, ik1, ka0, ka1, va, batch, seq)
            out_b = _mla_attention(qb, kf, vb, batch, seq)
            acts, w_out = [out_a, out_b], ev_w_out[i]
        else:
            qkv = _odd_proj(xs, mod[l], od_w_qkv[i], batch, seq)
            acts, w_out = [_stick_attention(qkv, batch, seq)], od_w_out[i]
        xs = _mix_ffn(xs, mod[l], acts, w_out, ln_mix_g[l], ln_mix_b[l], ffn_w_up[l], ffn_conv_w[l], ffn_conv_b[l],
                      ffn_w_down[l], ln_ffn_g[l], ln_ffn_b[l], batch, seq)
    return xs.reshape(batch, seq, d)
```

```python
import functools

import numpy as np
import jax
import jax.numpy as jnp
from jax import lax
from jax.experimental import pallas as pl
from jax.experimental.pallas import tpu as pltpu

F32 = jnp.float32
BF16 = jnp.bfloat16

D_MODEL = 1024
DEPTH = 4
CHUNK = 64
ROPE_THETA = 10000.0
LN_EPS = 1e-5
RMS_EPS = 1e-6

A_HEADS = 8
A_HEAD_DIM = 64
IDX_HEADS = 4
IDX_DIM = 64
TOPK_MAX = 256

B_HEADS = 8
B_NOPE = 64
B_ROPE = 32
B_V = 64
B_Q_RANK = 384
B_KV_RANK = 256

C_HEADS = 16
C_HEAD_DIM = D_MODEL // C_HEADS

D_FF = 2816
CONV_WIDTH = 3

EVEN_IN_SIZES = (A_HEADS * A_HEAD_DIM, A_HEAD_DIM, A_HEAD_DIM, IDX_HEADS * IDX_DIM, IDX_DIM, IDX_HEADS,
                 B_Q_RANK, B_KV_RANK, B_ROPE)
DEEPNORM_ALPHA = (2 * DEPTH) ** 0.25
LOG2E = 1.4426950408889634

LANES = 128
HALF = 64
TOKEN_TILE = 512
Q_TILE = 256
STICK_Q_TILE = 1024
STICK_KEY_BLOCK = 256
MLA_Q_TILE = 1024
MLA_KEY_BLOCK = 256
SOFTPLUS_LINEAR = 30.0
FF_CHUNK = 256
VMEM_LIMIT = 56 * 2 ** 20

_P_QA = 0
_P_IQ = _P_QA + A_HEADS * A_HEAD_DIM
_P_CQ = _P_IQ + IDX_HEADS * IDX_DIM
_P_CKV = _P_CQ + B_Q_RANK
_P_GK = _P_CKV + B_KV_RANK
_P_GV = _P_GK + LANES
_P_GI = _P_GV + LANES
_P_G5 = _P_GI + LANES
_P_END = _P_G5 + LANES
_KR_LO = HALF
_KR_HI = HALF + B_ROPE
_IW_LO = _KR_HI

_NT = (((1,), (1,)), ((), ()))


def _cparams(sem):
    return pltpu.CompilerParams(dimension_semantics=sem, vmem_limit_bytes=VMEM_LIMIT)


def _const_spec(shape):
    nd = len(shape)
    return pl.BlockSpec(shape, lambda *_: (0,) * nd)


def _lane_iota(shape):
    return lax.broadcasted_iota(jnp.int32, shape, len(shape) - 1)


def _layer_norm(v, g, b):
    mu = jnp.mean(v, axis=-1, keepdims=True)
    d = v - mu
    var = jnp.mean(d * d, axis=-1, keepdims=True)
    return d * lax.rsqrt(var + LN_EPS) * g + b


def _mod_kernel(c_ref, w_ref, b_ref, o_ref):
    c = c_ref[...]
    ca = c / (1.0 + jnp.exp(-c))
    o_ref[0] = jnp.dot(ca, w_ref[0], preferred_element_type=F32,
                       precision=lax.Precision.HIGHEST) + b_ref[0]


def _modulation(c, mod_w, mod_b):
    depth, d, d6 = mod_w.shape
    b = c.shape[0]
    nj = d6 // d
    return pl.pallas_call(
        _mod_kernel,
        grid=(depth, nj),
        in_specs=[pl.BlockSpec((b, d), lambda l, j: (0, 0)),
                  pl.BlockSpec((1, d, d), lambda l, j: (l, 0, j)),
                  pl.BlockSpec((1, 1, d), lambda l, j: (l, 0, j))],
        out_specs=pl.BlockSpec((1, b, d), lambda l, j: (l, 0, j)),
        out_shape=jax.ShapeDtypeStruct((depth, b, d6), F32),
        compiler_params=_cparams(("parallel", "parallel")),
    )(c, mod_w, mod_b.reshape(depth, 1, d6))


def _rope_kernel(pos_ref, f_ref, ga_ref, gb_ref, ca_ref, sa_ref, cb_ref, sb_ref):
    ang = pos_ref[...] * f_ref[...]
    c, s = jnp.cos(ang), jnp.sin(ang)
    na, nb = A_HEAD_DIM // 2, B_ROPE // 2
    ones = jnp.ones((ang.shape[0], _KR_LO), F32)
    ca_ref[...] = jnp.concatenate([c[:, :na]] * (LANES // na), axis=1)
    sa_ref[...] = jnp.concatenate([s[:, :na]] * (LANES // na), axis=1) * ga_ref[...]
    cb, sb = c[:, na:na + nb], s[:, na:na + nb]
    cb_ref[...] = jnp.concatenate([ones, cb, cb, ones[:, :LANES - _KR_HI]], axis=1)
    sb_ref[...] = jnp.concatenate([0.0 * ones, sb, sb, 0.0 * ones[:, :LANES - _KR_HI]], axis=1) * gb_ref[...]


def _rope_tables(positions):
    n = positions.size
    lane = np.arange(LANES)
    inv_a = ROPE_THETA ** (-jnp.arange(0, A_HEAD_DIM, 2, dtype=F32) / A_HEAD_DIM)
    inv_b = ROPE_THETA ** (-jnp.arange(0, B_ROPE, 2, dtype=F32) / B_ROPE)
    freqs = jnp.concatenate([inv_a, inv_b, jnp.zeros((LANES - inv_a.size - inv_b.size,), F32)])
    ga = jnp.asarray(np.where(lane % A_HEAD_DIM < A_HEAD_DIM // 2, -1.0, 1.0), F32)
    in_rope = (lane >= _KR_LO) & (lane < _KR_HI)
    gb = jnp.asarray(np.where(in_rope, np.where(lane < _KR_LO + B_ROPE // 2, -1.0, 1.0), 0.0), F32)
    tm = 2048 if n % 2048 == 0 else TOKEN_TILE
    vec = lambda v: v.reshape(1, LANES).astype(F32)
    tab = jax.ShapeDtypeStruct((n, LANES), F32)
    return pl.pallas_call(
        _rope_kernel,
        grid=(n // tm,),
        in_specs=[pl.BlockSpec((tm, 1), lambda i: (i, 0))] + [_const_spec((1, LANES))] * 3,
        out_specs=[pl.BlockSpec((tm, LANES), lambda i: (i, 0))] * 4,
        out_shape=[tab] * 4,
        compiler_params=_cparams(("parallel",)),
    )(positions.reshape(n, 1).astype(F32), vec(freqs), vec(ga), vec(gb))


def _rope_a(x, c, s):
    lane = _lane_iota(x.shape)
    partner = jnp.where(lane % A_HEAD_DIM < A_HEAD_DIM // 2,
                        pltpu.roll(x, LANES - A_HEAD_DIM // 2, 1), pltpu.roll(x, A_HEAD_DIM // 2, 1))
    return x * c + partner * s


def _rope_b(x, c, s):
    lane = _lane_iota(x.shape)
    partner = jnp.where(lane < _KR_LO + B_ROPE // 2,
                        pltpu.roll(x, LANES - B_ROPE // 2, 1), pltpu.roll(x, B_ROPE // 2, 1))
    return x * c + partner * s


def _even_proj_kernel(x_ref, mod_ref, win_ref, ca_ref, sa_ref, cb_ref, sb_ref, ikg_ref, ikb_ref,
                      qg_ref, kvg_ref, wuq_ref, wkn_ref, wv_ref,
                      qa_ref, iq_ref, ka0_ref, ka1_ref, va_ref, ik0_ref, ik1_ref, misc_ref,
                      qb_ref, kf_ref, vb_ref):
    sh = mod_ref[0, 0:1, :]
    sc = mod_ref[0, 1:2, :]
    h = (x_ref[...] * (1.0 + sc) + sh).astype(BF16)
    ca, sa, cb, sb = ca_ref[...], sa_ref[...], cb_ref[...], sb_ref[...]
    lane = _lane_iota(ca.shape)
    low = lane < HALF
    kr = []

    def rms(v, g_ref):
        return (v * lax.rsqrt(jnp.mean(v * v, axis=-1, keepdims=True) + RMS_EPS) * g_ref[...]).astype(BF16)

    def misc_group(p):
        g5 = _rope_b(p, cb, sb)
        misc_ref[...] = g5 * (IDX_HEADS ** -0.5)
        kr.append(jnp.where((lane >= _KR_LO) & (lane < _KR_HI), g5, 0.0))

    def kv_latent(p):
        ckvn = rms(p, kvg_ref)
        kn = jnp.dot(ckvn, wkn_ref[...], preferred_element_type=F32)
        for hd in range(B_HEADS):
            sl = slice(hd * LANES, (hd + 1) * LANES)
            kf_ref[:, sl] = (kn[:, sl] + kr[0]).astype(BF16)
        vb_ref[...] = jnp.dot(ckvn, wv_ref[...], preferred_element_type=F32).astype(BF16)

    def q_latent(p):
        qb = jnp.dot(rms(p, qg_ref), wuq_ref[...], preferred_element_type=F32)
        scale_b = (B_NOPE + B_ROPE) ** -0.5 * LOG2E
        for hd in range(B_HEADS):
            sl = slice(hd * LANES, (hd + 1) * LANES)
            qb_ref[:, sl] = (_rope_b(qb[:, sl], cb, sb) * scale_b).astype(BF16)

    def roped_heads(o_ref, scale):
        def store(p):
            for g in range(p.shape[1] // LANES):
                sl = slice(g * LANES, (g + 1) * LANES)
                o_ref[:, sl] = (_rope_a(p[:, sl], ca, sa) * scale).astype(BF16)
        return store

    def dsa_key(p):
        ka = _rope_a(p, ca, sa)
        ka0_ref[...] = jnp.where(low, ka, 0.0).astype(BF16)
        ka1_ref[...] = jnp.where(low, 0.0, ka).astype(BF16)

    def dsa_value(p):
        va_ref[...] = p.astype(BF16)

    def indexer_key(p):
        mu = jnp.sum(jnp.where(low, p, 0.0), axis=-1, keepdims=True) * (1.0 / IDX_DIM)
        d = p - mu
        var = jnp.sum(jnp.where(low, d * d, 0.0), axis=-1, keepdims=True) * (1.0 / IDX_DIM)
        ik = _rope_a(d * lax.rsqrt(var + LN_EPS) * ikg_ref[...] + ikb_ref[...], ca, sa)
        ik0_ref[...] = jnp.where(low, ik, 0.0).astype(BF16)
        ik1_ref[...] = jnp.where(low, 0.0, ik).astype(BF16)

    both = lambda first, second: lambda p: (first(p[:, :LANES]), second(p[:, LANES:]))
    stages = [(_P_GI, 2 * LANES, both(indexer_key, misc_group)), (_P_CKV, B_KV_RANK, kv_latent),
              (_P_CQ, B_Q_RANK, q_latent),
              (_P_QA, A_HEADS * A_HEAD_DIM, roped_heads(qa_ref, A_HEAD_DIM ** -0.5 * LOG2E)),
              (_P_IQ, IDX_HEADS * IDX_DIM, roped_heads(iq_ref, IDX_DIM ** -0.5)),
              (_P_GK, 2 * LANES, both(dsa_key, dsa_value))]
    project = lambda off, width: jnp.dot(h, win_ref[:, off:off + width], preferred_element_type=F32)
    nxt = project(*stages[0][:2])
    for s, (_, _, epilogue) in enumerate(stages):
        cur = nxt
        if s + 1 < len(stages):
            nxt = project(*stages[s + 1][:2])
        epilogue(cur)


def _even_weights(w_in, idx_k_g, idx_k_b, q_norm_g, kv_norm_g, w_uq, w_ukv):
    d = w_in.shape[0]
    qa, ka, va, iq, ik, iw, cq, ckv, kr = jnp.split(w_in, np.cumsum(EVEN_IN_SIZES)[:-1].tolist(), axis=1)
    z = lambda n: jnp.zeros((d, n), w_in.dtype)
    win = jnp.concatenate([qa, iq, cq, ckv, ka, ka, va, va, ik, ik,
                           z(_KR_LO), kr, iw, z(LANES - _IW_LO - IDX_HEADS)], axis=1).astype(BF16)
    assert win.shape[1] == _P_END
    pad_head = B_NOPE + B_ROPE
    wuq = jnp.pad(w_uq.reshape(B_Q_RANK, B_HEADS, pad_head), ((0, 0), (0, 0), (0, LANES - pad_head)))
    wuq = wuq.reshape(B_Q_RANK, B_HEADS * LANES).astype(BF16)
    wkv = w_ukv.reshape(B_KV_RANK, B_HEADS, B_NOPE + B_V)
    wkn = jnp.pad(wkv[:, :, :B_NOPE], ((0, 0), (0, 0), (0, LANES - B_NOPE)))
    wkn = wkn.reshape(B_KV_RANK, B_HEADS * LANES).astype(BF16)
    wv = wkv[:, :, B_NOPE:].reshape(B_KV_RANK, B_HEADS * B_V).astype(BF16)
    two = lambda v: jnp.concatenate([v, v]).reshape(1, LANES).astype(F32)
    return (win, two(idx_k_g), two(idx_k_b), q_norm_g.reshape(1, -1).astype(F32),
            kv_norm_g.reshape(1, -1).astype(F32), wuq, wkn, wv)


def _even_proj(x, mod_l, tables, weights, batch, seq):
    n, d = x.shape
    tm = min(TOKEN_TILE, seq)
    nt = seq // tm
    win, ikg, ikb, qg, kvg, wuq, wkn, wv = weights
    row = lambda w: pl.BlockSpec((tm, w), lambda b, t: (b * nt + t, 0))
    bf = lambda w: jax.ShapeDtypeStruct((n, w), BF16)
    out_widths = [A_HEADS * A_HEAD_DIM, IDX_HEADS * IDX_DIM, LANES, LANES, LANES, LANES, LANES]
    out_shape = [bf(w) for w in out_widths] + [jax.ShapeDtypeStruct((n, LANES), F32)] + \
                [bf(B_HEADS * LANES), bf(B_HEADS * LANES), bf(B_HEADS * B_V)]
    out_specs = [row(w) for w in out_widths] + [row(LANES)] + \
                [row(B_HEADS * LANES), row(B_HEADS * LANES), row(B_HEADS * B_V)]
    return pl.pallas_call(
        _even_proj_kernel,
        grid=(batch, nt),
        in_specs=[row(d), pl.BlockSpec((1, 6, d), lambda b, t: (b, 0, 0)), _const_spec(win.shape)]
                 + [row(LANES)] * 4
                 + [_const_spec(a.shape) for a in (ikg, ikb, qg, kvg, wuq, wkn, wv)],
        out_specs=out_specs,
        out_shape=out_shape,
        compiler_params=_cparams(("parallel", "parallel")),
    )(x, mod_l, win, *tables, ikg, ikb, qg, kvg, wuq, wkn, wv)


def _dsa_kernel(tq, topk, iq_ref, misc_ref, qa_ref, ik0_ref, ik1_ref, ka0_ref, ka1_ref, va_ref,
                o_ref, key_buf, hi_buf, lo_buf, lg_buf, m_buf, l_buf, acc_buf):
    i = pl.program_id(1)
    n_blk = i + 1
    int_min = jnp.int32(-2 ** 31)
    min16 = jnp.int16(-2 ** 15)
    one16, zero16 = jnp.int16(1), jnp.int16(0)
    kf = jnp.float32(topk)
    n_pair = A_HEADS // 2
    rows = n_pair * tq

    def rows_of(j):
        return pl.ds(pl.multiple_of(j * tq, tq), tq)

    iw_t = misc_ref[...].T
    iq = iq_ref[...]

    def score_block(j, diagonal):
        rels = [lax.dot_general(ik_ref[rows_of(j), :], iq[:, pair * LANES:(pair + 1) * LANES], _NT,
                                preferred_element_type=F32)
                for pair in range(IDX_HEADS // 2) for ik_ref in (ik0_ref, ik1_ref)]
        score = jnp.zeros((tq, tq), F32)
        for hd, rel in enumerate(rels):
            score = score + jnp.maximum(rel, 0.0) * iw_t[_IW_LO + hd:_IW_LO + hd + 1, :]
        score = jnp.where(score == 0.0, 0.0, score)
        bits = lax.bitcast_convert_type(score, jnp.int32)
        key = bits ^ (lax.shift_right_arithmetic(bits, 31) & jnp.int32(0x7FFFFFFF))
        if diagonal:
            key_chunk = lax.shift_right_logical(lax.broadcasted_iota(jnp.int32, (tq, tq), 0), 6)
            query_chunk = lax.shift_right_logical(lax.broadcasted_iota(jnp.int32, (tq, tq), 1), 6)
            key = jnp.where(key_chunk <= query_chunk, key, int_min)
        key_buf[rows_of(j), :] = key
        hi_buf[rows_of(j), :] = lax.shift_right_arithmetic(key, 16).astype(jnp.int16)

    lax.fori_loop(0, i, lambda j, c: (score_block(j, False), c)[1], 0)
    score_block(i, True)

    def count(pred, buf):
        def body(j, acc):
            hit = jnp.where(pred(buf[rows_of(j), :]), one16, zero16)
            for r in range(tq // 16):
                acc = acc + hit[r * 16:(r + 1) * 16]
            return acc
        acc = lax.fori_loop(0, n_blk, body, jnp.zeros((16, tq), jnp.int16))
        return jnp.sum(acc.astype(jnp.int32), axis=0, keepdims=True).astype(F32)

    def search(buf, base):
        def bit(b, cu):
            cand = cu | lax.shift_left(jnp.int32(1), 15 - b)
            image = (cand ^ jnp.int32(0x8000)).astype(jnp.int16)
            return jnp.where(base + count(lambda x: x >= image, buf) >= kf, cand, cu)

        return lax.fori_loop(0, 16, bit, jnp.zeros((1, tq), jnp.int32))

    hi_cu = search(hi_buf, 0.0)
    hi16 = (hi_cu ^ jnp.int32(0x8000)).astype(jnp.int16)

    def low_prep(j, _):
        low = ((key_buf[rows_of(j), :] & jnp.int32(0xFFFF)) ^ jnp.int32(0x8000)).astype(jnp.int16)
        lo_buf[rows_of(j), :] = jnp.where(hi_buf[rows_of(j), :] == hi16, low, min16)
        return 0

    lax.fori_loop(0, n_blk, low_prep, 0)
    n_above = count(lambda x: x > hi16, hi_buf)
    lo_cu = search(lo_buf, n_above)
    lo16 = (lo_cu ^ jnp.int32(0x8000)).astype(jnp.int16)
    thr = lax.shift_left(hi_cu ^ jnp.int32(0x8000), 16) | lo_cu
    need = kf - (n_above + count(lambda x: x > lo16, lo_buf))
    tri = jnp.where(lax.broadcasted_iota(jnp.int32, (tq, tq), 0) >= lax.broadcasted_iota(jnp.int32, (tq, tq), 1),
                    1.0, 0.0).astype(BF16)

    qa = qa_ref[...]
    q_stack = jnp.concatenate([qa[:, g * LANES:(g + 1) * LANES] for g in range(n_pair)], axis=0)
    m_buf[...] = jnp.full(m_buf.shape, -jnp.inf, F32)

    def logits_block(j, ties_before):
        k = key_buf[rows_of(j), :]
        tied = jnp.where(k == thr, 1.0, 0.0)
        rank = jnp.dot(tri, tied.astype(BF16), preferred_element_type=F32) + ties_before
        keep = jnp.where(k == int_min, 0.0,
                         jnp.where(k > thr, 1.0, jnp.where(rank <= need, tied, 0.0)))
        bias = jnp.where(keep.T > 0.5, 0.0, -jnp.inf)
        bias = jnp.concatenate([bias] * n_pair, axis=0)
        lgs = [lax.dot_general(q_stack, ka_ref[rows_of(j), :], _NT, preferred_element_type=F32)
               for ka_ref in (ka0_ref, ka1_ref)]
        for half, lg in enumerate(lgs):
            sl = slice(half * rows, (half + 1) * rows)
            lg = lg + bias
            lg_buf[j, sl, :] = lg
            folded = jnp.maximum(lg[:, :LANES], lg[:, LANES:]) if tq == 2 * LANES else lg
            m_buf[sl, :] = jnp.maximum(m_buf[sl, :], folded)
        return rank[tq - 1:tq, :]

    lax.fori_loop(0, n_blk, logits_block, jnp.zeros((1, tq), F32))
    m_buf[...] = jnp.broadcast_to(jnp.max(m_buf[...], axis=-1, keepdims=True), m_buf.shape)
    l_buf[...] = jnp.zeros(l_buf.shape, F32)
    acc_buf[...] = jnp.zeros(acc_buf.shape, F32)

    def pv_block(j, _):
        v = va_ref[rows_of(j), :]
        for half in range(2):
            sl = slice(half * rows, (half + 1) * rows)
            m = m_buf[sl, :]
            e = jnp.exp2(lg_buf[j, sl, :] - jnp.concatenate([m] * (tq // LANES), axis=1))
            l_buf[sl, :] = l_buf[sl, :] + (e[:, :LANES] + e[:, LANES:] if tq == 2 * LANES else e)
            acc_buf[sl, :] = acc_buf[sl, :] + jnp.dot(e.astype(BF16), v, preferred_element_type=F32)
        return 0

    lax.fori_loop(0, n_blk, pv_block, 0)
    out = acc_buf[...] / jnp.sum(l_buf[...], axis=-1, keepdims=True)
    lane = _lane_iota((tq, LANES))
    for g in range(n_pair):
        o_ref[:, g * LANES:(g + 1) * LANES] = jnp.where(
            lane < HALF, out[g * tq:(g + 1) * tq], out[rows + g * tq:rows + (g + 1) * tq]).astype(BF16)


def _dsa_attention(iq, misc, qa, ik0, ik1, ka0, ka1, va, batch, seq):
    n = qa.shape[0]
    tq = min(Q_TILE, seq)
    assert tq % LANES == 0 and tq // LANES in (1, 2) and seq <= 2 ** 15
    n_q = seq // tq
    topk = min(TOPK_MAX, seq // 4)
    rows = A_HEADS * tq
    qrow = lambda w: pl.BlockSpec((tq, w), lambda b, i: (b * n_q + i, 0))
    krow = pl.BlockSpec((seq, LANES), lambda b, i: (b, 0))
    return pl.pallas_call(
        functools.partial(_dsa_kernel, tq, topk),
        grid=(batch, n_q),
        in_specs=[qrow(iq.shape[1]), qrow(LANES), qrow(qa.shape[1])] + [krow] * 5,
        out_specs=qrow(qa.shape[1]),
        out_shape=jax.ShapeDtypeStruct((n, qa.shape[1]), BF16),
        scratch_shapes=[pltpu.VMEM((seq, tq), jnp.int32), pltpu.VMEM((seq, tq), jnp.int16),
                        pltpu.VMEM((seq, tq), jnp.int16), pltpu.VMEM((n_q, rows, tq), F32),
                        pltpu.VMEM((rows, LANES), F32), pltpu.VMEM((rows, LANES), F32),
                        pltpu.VMEM((rows, LANES), F32)],
        compiler_params=_cparams(("parallel", "arbitrary")),
    )(iq, misc, qa, ik0, ik1, ka0, ka1, va)


def _mla_kernel(tq, kb, q_ref, k_ref, v_ref, o_ref, lg_buf, m_buf, l_buf, acc_buf):
    i = pl.program_id(2)
    per_tile = tq // kb
    q = [q_ref[:, hh * LANES:(hh + 1) * LANES] for hh in range(2)]
    fold = lambda x, op: op(x[:, :LANES], x[:, LANES:]) if kb == 2 * LANES else x

    def rows_of(j):
        return pl.ds(pl.multiple_of(j * kb, kb), kb)

    def first_row(d):
        return 0 if d is None else d * kb

    def logits(js, diags):
        streams = [(u, hh) for u in range(len(js)) for hh in range(2)]
        lg = {}

        def stage_dot(s):
            u, hh = s
            lg[s] = lax.dot_general(q[hh][first_row(diags[u]):], k_ref[rows_of(js[u]), hh * LANES:(hh + 1) * LANES],
                                    _NT, preferred_element_type=F32)

        def stage_store(s):
            u, hh = s
            r0 = first_row(diags[u])
            x = lg[s]
            if diags[u] is not None:
                key_idx = lax.broadcasted_iota(jnp.int32, x.shape, 1) + diags[u] * kb
                query_idx = lax.broadcasted_iota(jnp.int32, x.shape, 0) + r0
                x = jnp.where(lax.shift_right_logical(key_idx, 6) <= lax.shift_right_logical(query_idx, 6),
                              x, -jnp.inf)
            lg_buf[hh, js[u], r0:, :] = x
            m_buf[hh, r0:, :] = jnp.maximum(m_buf[hh, r0:, :], fold(x, jnp.maximum))

        for t in range(len(streams) + 1):
            if t < len(streams):
                stage_dot(streams[t])
            if t >= 1:
                stage_store(streams[t - 1])

    def weighted(js, diags):
        streams = [(u, hh) for u in range(len(js)) for hh in range(2)]
        e = {}

        def stage_exp(s):
            u, hh = s
            r0 = first_row(diags[u])
            x = jnp.exp2(lg_buf[hh, js[u], r0:, :] - jnp.concatenate([m_buf[hh, r0:, :]] * (kb // LANES), axis=1))
            l_buf[hh, r0:, :] = l_buf[hh, r0:, :] + fold(x, jnp.add)
            e[s] = x.astype(BF16)

        def stage_pv(s):
            u, hh = s
            r0 = first_row(diags[u])
            acc_buf[hh, r0:, :] = acc_buf[hh, r0:, :] + jnp.dot(e[s], v_ref[rows_of(js[u]), :],
                                                              preferred_element_type=F32)

        for t in range(len(streams) + 1):
            if t < len(streams):
                stage_exp(streams[t])
            if t >= 1:
                stage_pv(streams[t - 1])

    diag = list(range(per_tile))
    n_full = i * per_tile
    full = lambda s: [s * per_tile + u for u in range(per_tile)]
    none = [None] * per_tile

    m_buf[...] = jnp.full(m_buf.shape, -jnp.inf, F32)
    logits([n_full + d for d in diag], diag)
    lax.fori_loop(0, i, lambda s, c: (logits(full(s), none), c)[1], 0)
    for hh in range(2):
        m_buf[hh] = jnp.broadcast_to(jnp.max(m_buf[hh], axis=-1, keepdims=True), (tq, LANES))
    l_buf[...] = jnp.zeros(l_buf.shape, F32)
    acc_buf[...] = jnp.zeros(acc_buf.shape, F32)
    weighted([n_full + d for d in diag], diag)
    lax.fori_loop(0, i, lambda s, c: (weighted(full(s), none), c)[1], 0)
    lane = _lane_iota((tq, LANES))
    outs = [acc_buf[hh] / jnp.sum(l_buf[hh], axis=-1, keepdims=True) for hh in range(2)]
    o_ref[...] = jnp.where(lane < HALF, outs[0], outs[1]).astype(BF16)


def _mla_attention(qb, kf, vb, batch, seq):
    n = qb.shape[0]
    tq = min(MLA_Q_TILE, seq)
    kb = min(MLA_KEY_BLOCK, tq)
    assert kb % LANES == 0 and kb // LANES in (1, 2) and kb % CHUNK == 0
    n_q = seq // tq
    n_pair = B_HEADS // 2
    return pl.pallas_call(
        functools.partial(_mla_kernel, tq, kb),
        grid=(batch, n_pair, n_q),
        in_specs=[pl.BlockSpec((tq, 2 * LANES), lambda b, p, i: (b * n_q + i, p)),
                  pl.BlockSpec((seq, 2 * LANES), lambda b, p, i: (b, p)),
                  pl.BlockSpec((seq, LANES), lambda b, p, i: (b, p))],
        out_specs=pl.BlockSpec((tq, LANES), lambda b, p, i: (b * n_q + i, p)),
        out_shape=jax.ShapeDtypeStruct((n, B_HEADS * B_V), BF16),
        scratch_shapes=[pltpu.VMEM((2, seq // kb, tq, kb), F32), pltpu.VMEM((2, tq, LANES), F32),
                        pltpu.VMEM((2, tq, LANES), F32), pltpu.VMEM((2, tq, LANES), F32)],
        compiler_params=_cparams(("parallel", "parallel", "arbitrary")),
    )(qb, kf, vb)


def _odd_proj_kernel(x_ref, mod_ref, w_ref, o_ref):
    sh = mod_ref[0, 0:1, :]
    sc = mod_ref[0, 1:2, :]
    h = (x_ref[...] * (1.0 + sc) + sh).astype(BF16)
    p = jnp.dot(h, w_ref[...], preferred_element_type=F32)
    dq = C_HEADS * C_HEAD_DIM
    o_ref[:, 0:dq] = (p[:, 0:dq] * (C_HEAD_DIM ** -0.5)).astype(BF16)
    o_ref[:, dq:] = p[:, dq:].astype(BF16)


def _odd_proj(x, mod_l, w_qkv, batch, seq):
    n, d = x.shape
    tm = min(TOKEN_TILE, seq)
    nt = seq // tm
    w = w_qkv.astype(BF16)
    return pl.pallas_call(
        _odd_proj_kernel,
        grid=(batch, nt),
        in_specs=[pl.BlockSpec((tm, d), lambda b, t: (b * nt + t, 0)),
                  pl.BlockSpec((1, 6, d), lambda b, t: (b, 0, 0)), _const_spec(w.shape)],
        out_specs=pl.BlockSpec((tm, w.shape[1]), lambda b, t: (b * nt + t, 0)),
        out_shape=jax.ShapeDtypeStruct((n, w.shape[1]), BF16),
        compiler_params=_cparams(("parallel", "parallel")),
    )(x, mod_l, w)


def _stick_kernel(tq, kb, q_ref, k_ref, v_ref, o_ref, acc_ref):
    i = pl.program_id(2)
    per_tile = tq // kb
    q2 = q_ref[...]
    lane = _lane_iota((tq, LANES))
    tri = jnp.where(lax.broadcasted_iota(jnp.int32, (kb, kb), 0) >= lax.broadcasted_iota(jnp.int32, (kb, kb), 1),
                    1.0, 0.0).astype(BF16)
    row = lax.broadcasted_iota(jnp.int32, (tq, kb), 0)
    col = lax.broadcasted_iota(jnp.int32, (tq, kb), 1)
    q_heads = (jnp.where(lane < HALF, q2, jnp.zeros_like(q2)), jnp.where(lane < HALF, jnp.zeros_like(q2), q2))
    acc_ref[...] = jnp.zeros(acc_ref.shape, F32)

    def blocks(js, carries, diags):
        streams = [(u, hh) for u in range(len(js)) for hh in range(2)]
        kv = []
        for j in js:
            start = pl.multiple_of(j * kb, kb)
            kv.append((k_ref[pl.ds(start, kb), :], v_ref[pl.ds(start, kb), :]))
        before = [None if d is None else (col + d * kb < row) for d in diags]
        z, inc = {}, {}
        carry = list(carries)

        r0 = [0 if d is None else d * kb for d in diags]

        def stage_z(s):
            z[s] = lax.dot_general(q_heads[s[1]][r0[s[0]]:], kv[s[0]][0], _NT, preferred_element_type=F32)

        def stage_sum(s):
            sp = jnp.where(z[s] > SOFTPLUS_LINEAR, z[s], jnp.log(1.0 + jnp.exp(z[s])))
            if before[s[0]] is not None:
                sp = jnp.where(before[s[0]][r0[s[0]]:], sp, 0.0)
            inc[s] = jnp.dot(sp.astype(BF16), tri, preferred_element_type=F32)

        def stage_out(s):
            u, hh = s
            a = jnp.exp(z[s] - inc[s] - carry[hh][r0[u]:])
            if before[u] is not None:
                a = jnp.where(before[u][r0[u]:], a, 0.0)
            acc_ref[hh, r0[u]:, :] = acc_ref[hh, r0[u]:, :] + jnp.dot(a.astype(BF16), kv[u][1],
                                                                      preferred_element_type=F32)
            tot = carry[hh][r0[u]:] + inc[s][:, 0:1]
            carry[hh] = tot if r0[u] == 0 else jnp.concatenate([carry[hh][:r0[u]], tot], axis=0)

        for t in range(len(streams) + 2):
            if t < len(streams):
                stage_z(streams[t])
            if 0 <= t - 1 < len(streams):
                stage_sum(streams[t - 1])
            if 0 <= t - 2 < len(streams):
                stage_out(streams[t - 2])
        return tuple(carry)

    carries = (jnp.zeros((tq, 1), F32), jnp.zeros((tq, 1), F32))
    diag = list(reversed(range(per_tile)))
    carries = blocks([i * per_tile + d for d in diag], carries, diag)
    n_full = i * per_tile
    lax.fori_loop(0, i, lambda s, cs: blocks([n_full - 1 - s * per_tile - u for u in range(per_tile)], cs,
                                             [None] * per_tile), carries)
    o_ref[...] = jnp.where(lane < HALF, acc_ref[0], acc_ref[1]).astype(BF16)


def _stick_attention(qkv, batch, seq):
    n = qkv.shape[0]
    tq = min(STICK_Q_TILE, seq)
    kb = min(STICK_KEY_BLOCK, tq)
    n_q = seq // tq
    n_pair = C_HEADS // 2
    return pl.pallas_call(
        functools.partial(_stick_kernel, tq, kb),
        grid=(batch, n_pair, n_q),
        in_specs=[pl.BlockSpec((tq, LANES), lambda b, p, i: (b * n_q + i, p)),
                  pl.BlockSpec((seq, LANES), lambda b, p, i: (b, n_pair + p)),
                  pl.BlockSpec((seq, LANES), lambda b, p, i: (b, 2 * n_pair + p))],
        out_specs=pl.BlockSpec((tq, LANES), lambda b, p, i: (b * n_q + i, p)),
        out_shape=jax.ShapeDtypeStruct((n, C_HEADS * C_HEAD_DIM), BF16),
        scratch_shapes=[pltpu.VMEM((2, tq, LANES), F32)],
        compiler_params=_cparams(("parallel", "parallel", "arbitrary")),
    )(qkv, qkv, qkv)


_HALO = 8


def _mix_ffn_kernel(tm, n_in, *refs):
    x_ref, mod_ref = refs[0], refs[1]
    ins = refs[2:2 + n_in]
    wos = refs[2 + n_in:2 + 2 * n_in]
    mg_ref, mb_ref, wup_ref, cw_ref, cb_ref, wdn_ref, g_ref, b_ref, o_ref, a_buf, tail_buf = refs[2 + 2 * n_in:]
    t = pl.program_id(1)
    mix = jnp.dot(ins[0][...], wos[0][...], preferred_element_type=F32)
    for a_ref, w_ref in zip(ins[1:], wos[1:]):
        mix = mix + jnp.dot(a_ref[...], w_ref[...], preferred_element_type=F32)
    x = _layer_norm(DEEPNORM_ALPHA * x_ref[...] + (1.0 + mod_ref[0, 2:3, :]) * mix, mg_ref[...], mb_ref[...])
    sh = mod_ref[0, 3:4, :]
    sc = mod_ref[0, 4:5, :]
    gate = mod_ref[0, 5:6, :]
    h = (x * (1.0 + sc) + sh).astype(BF16)

    @pl.when(t == 0)
    def _():
        tail_buf[...] = jnp.zeros_like(tail_buf)

    def up(ci):
        cs = slice(ci * FF_CHUNK, (ci + 1) * FF_CHUNK)
        gs = slice(D_FF + ci * FF_CHUNK, D_FF + (ci + 1) * FF_CHUNK)
        return (jnp.dot(h, wup_ref[:, cs], preferred_element_type=F32),
                jnp.dot(h, wup_ref[:, gs], preferred_element_type=F32))

    n_chunk = D_FF // FF_CHUNK
    y = jnp.zeros((tm, x.shape[1]), F32)
    nxt = up(0)
    for ci in range(n_chunk):
        cs = slice(ci * FF_CHUNK, (ci + 1) * FF_CHUNK)
        a, gt = nxt
        if ci + 1 < n_chunk:
            nxt = up(ci + 1)
        a_buf[0:_HALO, :] = tail_buf[:, cs]
        a_buf[_HALO:, :] = a
        tail_buf[:, cs] = a[tm - _HALO:, :]
        conv = (a_buf[_HALO - 2:_HALO - 2 + tm, :] * cw_ref[0:1, cs]
                + a_buf[_HALO - 1:_HALO - 1 + tm, :] * cw_ref[1:2, cs]
                + a * cw_ref[2:3, cs] + cb_ref[:, cs])
        u = conv / (1.0 + jnp.exp(-conv)) * gt
        y = y + jnp.dot(u.astype(BF16), wdn_ref[cs, :], preferred_element_type=F32)
    o_ref[...] = _layer_norm(DEEPNORM_ALPHA * x + (1.0 + gate) * y, g_ref[...], b_ref[...])


def _mix_ffn(x, mod_l, acts, w_out, mix_g, mix_b, w_up, conv_w, conv_b, w_down, ffn_g, ffn_b, batch, seq):
    n, d = x.shape
    tm = min(TOKEN_TILE, seq)
    nt = seq // tm
    row = lambda w: pl.BlockSpec((tm, w), lambda b, t: (b * nt + t, 0))
    single = lambda shape: pl.BlockSpec(shape, lambda b, t: (0,) * len(shape), pipeline_mode=pl.Buffered(1))
    wos, off = [], 0
    for a in acts:
        wos.append(w_out[off:off + a.shape[1]].astype(BF16))
        off += a.shape[1]
    vec = lambda v: v.reshape(1, -1)
    return pl.pallas_call(
        functools.partial(_mix_ffn_kernel, tm, len(acts)),
        grid=(batch, nt),
        in_specs=[row(d), pl.BlockSpec((1, 6, d), lambda b, t: (b, 0, 0))]
                 + [row(a.shape[1]) for a in acts] + [single(w.shape) for w in wos]
                 + [_const_spec((1, d))] * 2
                 + [single((d, 2 * D_FF)), _const_spec((CONV_WIDTH, D_FF)), _const_spec((1, D_FF)), single((D_FF, d)),
                    _const_spec((1, d)), _const_spec((1, d))],
        out_specs=row(d),
        out_shape=jax.ShapeDtypeStruct((n, d), F32),
        scratch_shapes=[pltpu.VMEM((_HALO + tm, FF_CHUNK), F32), pltpu.VMEM((_HALO, D_FF), F32)],
        compiler_params=_cparams(("parallel", "arbitrary")),
    )(x, mod_l, *acts, *wos, vec(mix_g), vec(mix_b), w_up.astype(BF16), conv_w, vec(conv_b), w_down.astype(BF16),
      vec(ffn_g), vec(ffn_b))


def kernel(x, c, positions, mod_w, mod_b, ln_mix_g, ln_mix_b, ln_ffn_g, ln_ffn_b, ev_w_in, ev_idx_k_g, ev_idx_k_b, ev_q_norm_g, ev_kv_norm_g, ev_w_uq, ev_w_ukv, ev_w_out, od_w_qkv, od_w_out, ffn_w_up, ffn_conv_w, ffn_conv_b, ffn_w_down):
    batch, seq, d = x.shape
    assert d == D_MODEL and seq % min(Q_TILE, seq) == 0 and seq % CHUNK == 0
    n = batch * seq
    depth = mod_w.shape[0]
    mod = _modulation(c, mod_w, mod_b).reshape(depth, batch, 6, d)
    tables = _rope_tables(positions)
    xs = x.reshape(n, d)
    for l in range(depth):
        i = l // 2
        if l % 2 == 0:
            weights = _even_weights(ev_w_in[i], ev_idx_k_g[i], ev_idx_k_b[i], ev_q_norm_g[i],
                                    ev_kv_norm_g[i], ev_w_uq[i], ev_w_ukv[i])
            qa, iq, ka0, ka1, va, ik0, ik1, misc, qb, kf, vb = _even_proj(xs, mod[l], tables, weights, batch, seq)
            out_a = _dsa_attention(iq, misc, qa, ik0, ik1, ka0, ka1, va, batch, seq)
            out_b = _mla_attention(qb, kf, vb, batch, seq)
            acts, w_out = [out_a, out_b], ev_w_out[i]
        else:
            qkv = _odd_proj(xs, mod[l], od_w_qkv[i], batch, seq)
            acts, w_out = [_stick_attention(qkv, batch, seq)], od_w_out[i]
        xs = _mix_ffn(xs, mod[l], acts, w_out, ln_mix_g[l], ln_mix_b[l], ffn_w_up[l], ffn_conv_w[l], ffn_conv_b[l],
                      ffn_w_down[l], ln_ffn_g[l], ln_ffn_b[l], batch, seq)
    return xs.reshape(batch, seq, d)
```

```python
import functools

import numpy as np
import jax
import jax.numpy as jnp
from jax import lax
from jax.experimental import pallas as pl
from jax.experimental.pallas import tpu as pltpu

F32 = jnp.float32
BF16 = jnp.bfloat16

D_MODEL = 1024
DEPTH = 4
CHUNK = 64
ROPE_THETA = 10000.0
LN_EPS = 1e-5
RMS_EPS = 1e-6

A_HEADS = 8
A_HEAD_DIM = 64
IDX_HEADS = 4
IDX_DIM = 64
TOPK_MAX = 256

B_HEADS = 8
B_NOPE = 64
B_ROPE = 32
B_V = 64
B_Q_RANK = 384
B_KV_RANK = 256

C_HEADS = 16
C_HEAD_DIM = D_MODEL // C_HEADS

D_FF = 2816
CONV_WIDTH = 3

EVEN_IN_SIZES = (A_HEADS * A_HEAD_DIM, A_HEAD_DIM, A_HEAD_DIM, IDX_HEADS * IDX_DIM, IDX_DIM, IDX_HEADS,
                 B_Q_RANK, B_KV_RANK, B_ROPE)
DEEPNORM_ALPHA = (2 * DEPTH) ** 0.25
LOG2E = 1.4426950408889634

LANES = 128
HALF = 64
TOKEN_TILE = 512
PROJ_TILE = 1024
Q_TILE = 256
STICK_Q_TILE = 1024
STICK_KEY_BLOCK = 256
MLA_Q_TILE = 1024
MLA_KEY_BLOCK = 256
SOFTPLUS_LINEAR = 30.0
FF_CHUNK = 256
VMEM_LIMIT = 56 * 2 ** 20

_P_QA = 0
_P_IQ = _P_QA + A_HEADS * A_HEAD_DIM
_P_CQ = _P_IQ + IDX_HEADS * IDX_DIM
_P_CKV = _P_CQ + B_Q_RANK
_P_GK = _P_CKV + B_KV_RANK
_P_GV = _P_GK + LANES
_P_GI = _P_GV + LANES
_P_G5 = _P_GI + LANES
_P_END = _P_G5 + LANES
_KR_LO = HALF
_KR_HI = HALF + B_ROPE
_IW_LO = _KR_HI

_NT = (((1,), (1,)), ((), ()))


def _cparams(sem):
    return pltpu.CompilerParams(dimension_semantics=sem, vmem_limit_bytes=VMEM_LIMIT)


def _const_spec(shape):
    nd = len(shape)
    return pl.BlockSpec(shape, lambda *_: (0,) * nd)


def _lane_iota(shape):
    return lax.broadcasted_iota(jnp.int32, shape, len(shape) - 1)


def _layer_norm(v, g, b):
    mu = jnp.mean(v, axis=-1, keepdims=True)
    d = v - mu
    var = jnp.mean(d * d, axis=-1, keepdims=True)
    return d * lax.rsqrt(var + LN_EPS) * g + b


def _mod_kernel(c_ref, w_ref, b_ref, o_ref):
    c = c_ref[...]
    ca = c / (1.0 + jnp.exp(-c))
    o_ref[0] = jnp.dot(ca, w_ref[0], preferred_element_type=F32,
                       precision=lax.Precision.HIGHEST) + b_ref[0]


def _modulation(c, mod_w, mod_b):
    depth, d, d6 = mod_w.shape
    b = c.shape[0]
    nj = d6 // d
    return pl.pallas_call(
        _mod_kernel,
        grid=(depth, nj),
        in_specs=[pl.BlockSpec((b, d), lambda l, j: (0, 0)),
                  pl.BlockSpec((1, d, d), lambda l, j: (l, 0, j)),
                  pl.BlockSpec((1, 1, d), lambda l, j: (l, 0, j))],
        out_specs=pl.BlockSpec((1, b, d), lambda l, j: (l, 0, j)),
        out_shape=jax.ShapeDtypeStruct((depth, b, d6), F32),
        compiler_params=_cparams(("parallel", "parallel")),
    )(c, mod_w, mod_b.reshape(depth, 1, d6))


def _rope_kernel(pos_ref, f_ref, ga_ref, gb_ref, ca_ref, sa_ref, cb_ref, sb_ref):
    ang = pos_ref[...] * f_ref[...]
    c, s = jnp.cos(ang), jnp.sin(ang)
    na, nb = A_HEAD_DIM // 2, B_ROPE // 2
    ones = jnp.ones((ang.shape[0], _KR_LO), F32)
    ca_ref[...] = jnp.concatenate([c[:, :na]] * (LANES // na), axis=1)
    sa_ref[...] = jnp.concatenate([s[:, :na]] * (LANES // na), axis=1) * ga_ref[...]
    cb, sb = c[:, na:na + nb], s[:, na:na + nb]
    cb_ref[...] = jnp.concatenate([ones, cb, cb, ones[:, :LANES - _KR_HI]], axis=1)
    sb_ref[...] = jnp.concatenate([0.0 * ones, sb, sb, 0.0 * ones[:, :LANES - _KR_HI]], axis=1) * gb_ref[...]


def _rope_tables(positions):
    n = positions.size
    lane = np.arange(LANES)
    inv_a = ROPE_THETA ** (-jnp.arange(0, A_HEAD_DIM, 2, dtype=F32) / A_HEAD_DIM)
    inv_b = ROPE_THETA ** (-jnp.arange(0, B_ROPE, 2, dtype=F32) / B_ROPE)
    freqs = jnp.concatenate([inv_a, inv_b, jnp.zeros((LANES - inv_a.size - inv_b.size,), F32)])
    ga = jnp.asarray(np.where(lane % A_HEAD_DIM < A_HEAD_DIM // 2, -1.0, 1.0), F32)
    in_rope = (lane >= _KR_LO) & (lane < _KR_HI)
    gb = jnp.asarray(np.where(in_rope, np.where(lane < _KR_LO + B_ROPE // 2, -1.0, 1.0), 0.0), F32)
    tm = 2048 if n % 2048 == 0 else TOKEN_TILE
    vec = lambda v: v.reshape(1, LANES).astype(F32)
    tab = jax.ShapeDtypeStruct((n, LANES), F32)
    return pl.pallas_call(
        _rope_kernel,
        grid=(n // tm,),
        in_specs=[pl.BlockSpec((tm, 1), lambda i: (i, 0))] + [_const_spec((1, LANES))] * 3,
        out_specs=[pl.BlockSpec((tm, LANES), lambda i: (i, 0))] * 4,
        out_shape=[tab] * 4,
        compiler_params=_cparams(("parallel",)),
    )(positions.reshape(n, 1).astype(F32), vec(freqs), vec(ga), vec(gb))


def _rope_a(x, c, s):
    lane = _lane_iota(x.shape)
    partner = jnp.where(lane % A_HEAD_DIM < A_HEAD_DIM // 2,
                        pltpu.roll(x, LANES - A_HEAD_DIM // 2, 1), pltpu.roll(x, A_HEAD_DIM // 2, 1))
    return x * c + partner * s


def _rope_b(x, c, s):
    lane = _lane_iota(x.shape)
    partner = jnp.where(lane < _KR_LO + B_ROPE // 2,
                        pltpu.roll(x, LANES - B_ROPE // 2, 1), pltpu.roll(x, B_ROPE // 2, 1))
    return x * c + partner * s


def _even_proj_kernel(x_ref, mod_ref, win_ref, ca_ref, sa_ref, cb_ref, sb_ref, ikg_ref, ikb_ref,
                      qg_ref, kvg_ref, wuq_ref, wkn_ref, wv_ref,
                      qa_ref, iq_ref, ka0_ref, ka1_ref, va_ref, ik0_ref, ik1_ref, misc_ref,
                      qb_ref, kf_ref, vb_ref):
    sh = mod_ref[0, 0:1, :]
    sc = mod_ref[0, 1:2, :]
    h = (x_ref[...] * (1.0 + sc) + sh).astype(BF16)
    ca, sa, cb, sb = ca_ref[...], sa_ref[...], cb_ref[...], sb_ref[...]
    lane = _lane_iota(ca.shape)
    low = lane < HALF
    kr = []

    def rms(v, g_ref):
        return (v * lax.rsqrt(jnp.mean(v * v, axis=-1, keepdims=True) + RMS_EPS) * g_ref[...]).astype(BF16)

    def misc_group(p):
        g5 = _rope_b(p, cb, sb)
        misc_ref[...] = g5 * (IDX_HEADS ** -0.5)
        kr.append(jnp.where((lane >= _KR_LO) & (lane < _KR_HI), g5, 0.0))

    def kv_latent(p):
        ckvn = rms(p, kvg_ref)
        kn = jnp.dot(ckvn, wkn_ref[...], preferred_element_type=F32)
        for hd in range(B_HEADS):
            sl = slice(hd * LANES, (hd + 1) * LANES)
            kf_ref[:, sl] = (kn[:, sl] + kr[0]).astype(BF16)
        vb_ref[...] = jnp.dot(ckvn, wv_ref[...], preferred_element_type=F32).astype(BF16)

    def q_latent(p):
        qb = jnp.dot(rms(p, qg_ref), wuq_ref[...], preferred_element_type=F32)
        scale_b = (B_NOPE + B_ROPE) ** -0.5 * LOG2E
        for hd in range(B_HEADS):
            sl = slice(hd * LANES, (hd + 1) * LANES)
            qb_ref[:, sl] = (_rope_b(qb[:, sl], cb, sb) * scale_b).astype(BF16)

    def roped_heads(o_ref, scale):
        def store(p):
            for g in range(p.shape[1] // LANES):
                sl = slice(g * LANES, (g + 1) * LANES)
                o_ref[:, sl] = (_rope_a(p[:, sl], ca, sa) * scale).astype(BF16)
        return store

    def dsa_key(p):
        ka = _rope_a(p, ca, sa)
        ka0_ref[...] = jnp.where(low, ka, 0.0).astype(BF16)
        ka1_ref[...] = jnp.where(low, 0.0, ka).astype(BF16)

    def dsa_value(p):
        va_ref[...] = p.astype(BF16)

    def indexer_key(p):
        mu = jnp.sum(jnp.where(low, p, 0.0), axis=-1, keepdims=True) * (1.0 / IDX_DIM)
        d = p - mu
        var = jnp.sum(jnp.where(low, d * d, 0.0), axis=-1, keepdims=True) * (1.0 / IDX_DIM)
        ik = _rope_a(d * lax.rsqrt(var + LN_EPS) * ikg_ref[...] + ikb_ref[...], ca, sa)
        ik0_ref[...] = jnp.where(low, ik, 0.0).astype(BF16)
        ik1_ref[...] = jnp.where(low, 0.0, ik).astype(BF16)

    both = lambda first, second: lambda p: (first(p[:, :LANES]), second(p[:, LANES:]))
    stages = [(_P_GI, 2 * LANES, both(indexer_key, misc_group)), (_P_CKV, B_KV_RANK, kv_latent),
              (_P_CQ, B_Q_RANK, q_latent),
              (_P_QA, A_HEADS * A_HEAD_DIM, roped_heads(qa_ref, A_HEAD_DIM ** -0.5 * LOG2E)),
              (_P_IQ, IDX_HEADS * IDX_DIM, roped_heads(iq_ref, IDX_DIM ** -0.5)),
              (_P_GK, 2 * LANES, both(dsa_key, dsa_value))]
    project = lambda off, width: jnp.dot(h, win_ref[:, off:off + width], preferred_element_type=F32)
    nxt = project(*stages[0][:2])
    for s, (_, _, epilogue) in enumerate(stages):
        cur = nxt
        if s + 1 < len(stages):
            nxt = project(*stages[s + 1][:2])
        epilogue(cur)


def _even_weights(w_in, idx_k_g, idx_k_b, q_norm_g, kv_norm_g, w_uq, w_ukv):
    d = w_in.shape[0]
    qa, ka, va, iq, ik, iw, cq, ckv, kr = jnp.split(w_in, np.cumsum(EVEN_IN_SIZES)[:-1].tolist(), axis=1)
    z = lambda n: jnp.zeros((d, n), w_in.dtype)
    win = jnp.concatenate([qa, iq, cq, ckv, ka, ka, va, va, ik, ik,
                           z(_KR_LO), kr, iw, z(LANES - _IW_LO - IDX_HEADS)], axis=1).astype(BF16)
    assert win.shape[1] == _P_END
    pad_head = B_NOPE + B_ROPE
    wuq = jnp.pad(w_uq.reshape(B_Q_RANK, B_HEADS, pad_head), ((0, 0), (0, 0), (0, LANES - pad_head)))
    wuq = wuq.reshape(B_Q_RANK, B_HEADS * LANES).astype(BF16)
    wkv = w_ukv.reshape(B_KV_RANK, B_HEADS, B_NOPE + B_V)
    wkn = jnp.pad(wkv[:, :, :B_NOPE], ((0, 0), (0, 0), (0, LANES - B_NOPE)))
    wkn = wkn.reshape(B_KV_RANK, B_HEADS * LANES).astype(BF16)
    wv = wkv[:, :, B_NOPE:].reshape(B_KV_RANK, B_HEADS * B_V).astype(BF16)
    two = lambda v: jnp.concatenate([v, v]).reshape(1, LANES).astype(F32)
    return (win, two(idx_k_g), two(idx_k_b), q_norm_g.reshape(1, -1).astype(F32),
            kv_norm_g.reshape(1, -1).astype(F32), wuq, wkn, wv)


def _even_proj(x, mod_l, tables, weights, batch, seq):
    n, d = x.shape
    tm = min(PROJ_TILE, seq)
    nt = seq // tm
    win, ikg, ikb, qg, kvg, wuq, wkn, wv = weights
    row = lambda w: pl.BlockSpec((tm, w), lambda b, t: (b * nt + t, 0))
    bf = lambda w: jax.ShapeDtypeStruct((n, w), BF16)
    out_widths = [A_HEADS * A_HEAD_DIM, IDX_HEADS * IDX_DIM, LANES, LANES, LANES, LANES, LANES]
    out_shape = [bf(w) for w in out_widths] + [jax.ShapeDtypeStruct((n, LANES), F32)] + \
                [bf(B_HEADS * LANES), bf(B_HEADS * LANES), bf(B_HEADS * B_V)]
    out_specs = [row(w) for w in out_widths] + [row(LANES)] + \
                [row(B_HEADS * LANES), row(B_HEADS * LANES), row(B_HEADS * B_V)]
    return pl.pallas_call(
        _even_proj_kernel,
        grid=(batch, nt),
        in_specs=[row(d), pl.BlockSpec((1, 6, d), lambda b, t: (b, 0, 0)), _const_spec(win.shape)]
                 + [row(LANES)] * 4
                 + [_const_spec(a.shape) for a in (ikg, ikb, qg, kvg, wuq, wkn, wv)],
        out_specs=out_specs,
        out_shape=out_shape,
        compiler_params=_cparams(("parallel", "parallel")),
    )(x, mod_l, win, *tables, ikg, ikb, qg, kvg, wuq, wkn, wv)


def _dsa_kernel(tq, topk, iq_ref, misc_ref, qa_ref, ik0_ref, ik1_ref, ka0_ref, ka1_ref, va_ref,
                o_ref, key_buf, hi_buf, lo_buf, lg_buf, m_buf, l_buf, acc_buf):
    i = pl.program_id(1)
    n_blk = i + 1
    int_min = jnp.int32(-2 ** 31)
    min16 = jnp.int16(-2 ** 15)
    one16, zero16 = jnp.int16(1), jnp.int16(0)
    kf = jnp.float32(topk)
    n_pair = A_HEADS // 2
    rows = n_pair * tq

    def rows_of(j):
        return pl.ds(pl.multiple_of(j * tq, tq), tq)

    iw_t = misc_ref[...].T
    iq = iq_ref[...]

    def score_block(j, diagonal):
        rels = [lax.dot_general(ik_ref[rows_of(j), :], iq[:, pair * LANES:(pair + 1) * LANES], _NT,
                                preferred_element_type=F32)
                for pair in range(IDX_HEADS // 2) for ik_ref in (ik0_ref, ik1_ref)]
        score = jnp.zeros((tq, tq), F32)
        for hd, rel in enumerate(rels):
            score = score + jnp.maximum(rel, 0.0) * iw_t[_IW_LO + hd:_IW_LO + hd + 1, :]
        score = jnp.where(score == 0.0, 0.0, score)
        bits = lax.bitcast_convert_type(score, jnp.int32)
        key = bits ^ (lax.shift_right_arithmetic(bits, 31) & jnp.int32(0x7FFFFFFF))
        if diagonal:
            key_chunk = lax.shift_right_logical(lax.broadcasted_iota(jnp.int32, (tq, tq), 0), 6)
            query_chunk = lax.shift_right_logical(lax.broadcasted_iota(jnp.int32, (tq, tq), 1), 6)
            key = jnp.where(key_chunk <= query_chunk, key, int_min)
        key_buf[rows_of(j), :] = key
        hi_buf[rows_of(j), :] = lax.shift_right_arithmetic(key, 16).astype(jnp.int16)

    lax.fori_loop(0, i, lambda j, c: (score_block(j, False), c)[1], 0)
    score_block(i, True)

    def count(pred, buf):
        def body(j, acc):
            hit = jnp.where(pred(buf[rows_of(j), :]), one16, zero16)
            for r in range(tq // 16):
                acc = acc + hit[r * 16:(r + 1) * 16]
            return acc
        acc = lax.fori_loop(0, n_blk, body, jnp.zeros((16, tq), jnp.int16))
        return jnp.sum(acc.astype(jnp.int32), axis=0, keepdims=True).astype(F32)

    def search(buf, base):
        def bit(b, cu):
            cand = cu | lax.shift_left(jnp.int32(1), 15 - b)
            image = (cand ^ jnp.int32(0x8000)).astype(jnp.int16)
            return jnp.where(base + count(lambda x: x >= image, buf) >= kf, cand, cu)

        return lax.fori_loop(0, 16, bit, jnp.zeros((1, tq), jnp.int32))

    hi_cu = search(hi_buf, 0.0)
    hi16 = (hi_cu ^ jnp.int32(0x8000)).astype(jnp.int16)

    def low_prep(j, _):
        low = ((key_buf[rows_of(j), :] & jnp.int32(0xFFFF)) ^ jnp.int32(0x8000)).astype(jnp.int16)
        lo_buf[rows_of(j), :] = jnp.where(hi_buf[rows_of(j), :] == hi16, low, min16)
        return 0

    lax.fori_loop(0, n_blk, low_prep, 0)
    n_above = count(lambda x: x > hi16, hi_buf)
    lo_cu = search(lo_buf, n_above)
    lo16 = (lo_cu ^ jnp.int32(0x8000)).astype(jnp.int16)
    thr = lax.shift_left(hi_cu ^ jnp.int32(0x8000), 16) | lo_cu
    need = kf - (n_above + count(lambda x: x > lo16, lo_buf))
    tri = jnp.where(lax.broadcasted_iota(jnp.int32, (tq, tq), 0) >= lax.broadcasted_iota(jnp.int32, (tq, tq), 1),
                    1.0, 0.0).astype(BF16)

    qa = qa_ref[...]
    q_stack = jnp.concatenate([qa[:, g * LANES:(g + 1) * LANES] for g in range(n_pair)], axis=0)
    m_buf[...] = jnp.full(m_buf.shape, -jnp.inf, F32)

    def logits_block(j, ties_before):
        k = key_buf[rows_of(j), :]
        tied = jnp.where(k == thr, 1.0, 0.0)
        rank = jnp.dot(tri, tied.astype(BF16), preferred_element_type=F32) + ties_before
        keep = jnp.where(k == int_min, 0.0,
                         jnp.where(k > thr, 1.0, jnp.where(rank <= need, tied, 0.0)))
        bias = jnp.where(keep.T > 0.5, 0.0, -jnp.inf)
        bias = jnp.concatenate([bias] * n_pair, axis=0)
        lgs = [lax.dot_general(q_stack, ka_ref[rows_of(j), :], _NT, preferred_element_type=F32)
               for ka_ref in (ka0_ref, ka1_ref)]
        for half, lg in enumerate(lgs):
            sl = slice(half * rows, (half + 1) * rows)
            lg = lg + bias
            lg_buf[j, sl, :] = lg
            folded = jnp.maximum(lg[:, :LANES], lg[:, LANES:]) if tq == 2 * LANES else lg
            m_buf[sl, :] = jnp.maximum(m_buf[sl, :], folded)
        return rank[tq - 1:tq, :]

    lax.fori_loop(0, n_blk, logits_block, jnp.zeros((1, tq), F32))
    m_buf[...] = jnp.broadcast_to(jnp.max(m_buf[...], axis=-1, keepdims=True), m_buf.shape)
    l_buf[...] = jnp.zeros(l_buf.shape, F32)
    acc_buf[...] = jnp.zeros(acc_buf.shape, F32)

    def pv_block(j, _):
        v = va_ref[rows_of(j), :]
        for half in range(2):
            sl = slice(half * rows, (half + 1) * rows)
            m = m_buf[sl, :]
            e = jnp.exp2(lg_buf[j, sl, :] - jnp.concatenate([m] * (tq // LANES), axis=1))
            l_buf[sl, :] = l_buf[sl, :] + (e[:, :LANES] + e[:, LANES:] if tq == 2 * LANES else e)
            acc_buf[sl, :] = acc_buf[sl, :] + jnp.dot(e.astype(BF16), v, preferred_element_type=F32)
        return 0

    lax.fori_loop(0, n_blk, pv_block, 0)
    out = acc_buf[...] / jnp.sum(l_buf[...], axis=-1, keepdims=True)
    lane = _lane_iota((tq, LANES))
    for g in range(n_pair):
        o_ref[:, g * LANES:(g + 1) * LANES] = jnp.where(
            lane < HALF, out[g * tq:(g + 1) * tq], out[rows + g * tq:rows + (g + 1) * tq]).astype(BF16)


def _dsa_attention(iq, misc, qa, ik0, ik1, ka0, ka1, va, batch, seq):
    n = qa.shape[0]
    tq = min(Q_TILE, seq)
    assert tq % LANES == 0 and tq // LANES in (1, 2) and seq <= 2 ** 15
    n_q = seq // tq
    topk = min(TOPK_MAX, seq // 4)
    rows = A_HEADS * tq
    qrow = lambda w: pl.BlockSpec((tq, w), lambda b, i: (b * n_q + i, 0))
    krow = pl.BlockSpec((seq, LANES), lambda b, i: (b, 0))
    return pl.pallas_call(
        functools.partial(_dsa_kernel, tq, topk),
        grid=(batch, n_q),
        in_specs=[qrow(iq.shape[1]), qrow(LANES), qrow(qa.shape[1])] + [krow] * 5,
        out_specs=qrow(qa.shape[1]),
        out_shape=jax.ShapeDtypeStruct((n, qa.shape[1]), BF16),
        scratch_shapes=[pltpu.VMEM((seq, tq), jnp.int32), pltpu.VMEM((seq, tq), jnp.int16),
                        pltpu.VMEM((seq, tq), jnp.int16), pltpu.VMEM((n_q, rows, tq), F32),
                        pltpu.VMEM((rows, LANES), F32), pltpu.VMEM((rows, LANES), F32),
                        pltpu.VMEM((rows, LANES), F32)],
        compiler_params=_cparams(("parallel", "arbitrary")),
    )(iq, misc, qa, ik0, ik1, ka0, ka1, va)


def _mla_kernel(tq, kb, q_ref, k_ref, v_ref, o_ref, lg_buf, m_buf, l_buf, acc_buf):
    i = pl.program_id(2)
    per_tile = tq // kb
    q = [q_ref[:, hh * LANES:(hh + 1) * LANES] for hh in range(2)]
    fold = lambda x, op: op(x[:, :LANES], x[:, LANES:]) if kb == 2 * LANES else x

    def rows_of(j):
        return pl.ds(pl.multiple_of(j * kb, kb), kb)

    def first_row(d):
        return 0 if d is None else d * kb

    def logits(js, diags):
        streams = [(u, hh) for u in range(len(js)) for hh in range(2)]
        lg = {}

        def stage_dot(s):
            u, hh = s
            lg[s] = lax.dot_general(q[hh][first_row(diags[u]):], k_ref[rows_of(js[u]), hh * LANES:(hh + 1) * LANES],
                                    _NT, preferred_element_type=F32)

        def stage_store(s):
            u, hh = s
            r0 = first_row(diags[u])
            x = lg[s]
            if diags[u] is not None:
                key_idx = lax.broadcasted_iota(jnp.int32, x.shape, 1) + diags[u] * kb
                query_idx = lax.broadcasted_iota(jnp.int32, x.shape, 0) + r0
                x = jnp.where(lax.shift_right_logical(key_idx, 6) <= lax.shift_right_logical(query_idx, 6),
                              x, -jnp.inf)
            lg_buf[hh, js[u], r0:, :] = x
            m_buf[hh, r0:, :] = jnp.maximum(m_buf[hh, r0:, :], fold(x, jnp.maximum))

        for t in range(len(streams) + 1):
            if t < len(streams):
                stage_dot(streams[t])
            if t >= 1:
                stage_store(streams[t - 1])

    def weighted(js, diags):
        streams = [(u, hh) for u in range(len(js)) for hh in range(2)]
        e = {}

        def stage_exp(s):
            u, hh = s
            r0 = first_row(diags[u])
            x = jnp.exp2(lg_buf[hh, js[u], r0:, :] - jnp.concatenate([m_buf[hh, r0:, :]] * (kb // LANES), axis=1))
            l_buf[hh, r0:, :] = l_buf[hh, r0:, :] + fold(x, jnp.add)
            e[s] = x.astype(BF16)

        def stage_pv(s):
            u, hh = s
            r0 = first_row(diags[u])
            acc_buf[hh, r0:, :] = acc_buf[hh, r0:, :] + jnp.dot(e[s], v_ref[rows_of(js[u]), :],
                                                              preferred_element_type=F32)

        for t in range(len(streams) + 1):
            if t < len(streams):
                stage_exp(streams[t])
            if t >= 1:
                stage_pv(streams[t - 1])

    diag = list(range(per_tile))
    n_full = i * per_tile
    full = lambda s: [s * per_tile + u for u in range(per_tile)]
    none = [None] * per_tile

    m_buf[...] = jnp.full(m_buf.shape, -jnp.inf, F32)
    logits([n_full + d for d in diag], diag)
    lax.fori_loop(0, i, lambda s, c: (logits(full(s), none), c)[1], 0)
    for hh in range(2):
        m_buf[hh] = jnp.broadcast_to(jnp.max(m_buf[hh], axis=-1, keepdims=True), (tq, LANES))
    l_buf[...] = jnp.zeros(l_buf.shape, F32)
    acc_buf[...] = jnp.zeros(acc_buf.shape, F32)
    weighted([n_full + d for d in diag], diag)
    lax.fori_loop(0, i, lambda s, c: (weighted(full(s), none), c)[1], 0)
    lane = _lane_iota((tq, LANES))
    outs = [acc_buf[hh] / jnp.sum(l_buf[hh], axis=-1, keepdims=True) for hh in range(2)]
    o_ref[...] = jnp.where(lane < HALF, outs[0], outs[1]).astype(BF16)


def _mla_attention(qb, kf, vb, batch, seq):
    n = qb.shape[0]
    tq = min(MLA_Q_TILE, seq)
    kb = min(MLA_KEY_BLOCK, tq)
    assert kb % LANES == 0 and kb // LANES in (1, 2) and kb % CHUNK == 0
    n_q = seq // tq
    n_pair = B_HEADS // 2
    return pl.pallas_call(
        functools.partial(_mla_kernel, tq, kb),
        grid=(batch, n_pair, n_q),
        in_specs=[pl.BlockSpec((tq, 2 * LANES), lambda b, p, i: (b * n_q + i, p)),
                  pl.BlockSpec((seq, 2 * LANES), lambda b, p, i: (b, p)),
                  pl.BlockSpec((seq, LANES), lambda b, p, i: (b, p))],
        out_specs=pl.BlockSpec((tq, LANES), lambda b, p, i: (b * n_q + i, p)),
        out_shape=jax.ShapeDtypeStruct((n, B_HEADS * B_V), BF16),
        scratch_shapes=[pltpu.VMEM((2, seq // kb, tq, kb), F32), pltpu.VMEM((2, tq, LANES), F32),
                        pltpu.VMEM((2, tq, LANES), F32), pltpu.VMEM((2, tq, LANES), F32)],
        compiler_params=_cparams(("parallel", "parallel", "arbitrary")),
    )(qb, kf, vb)


def _odd_proj_kernel(x_ref, mod_ref, w_ref, o_ref):
    sh = mod_ref[0, 0:1, :]
    sc = mod_ref[0, 1:2, :]
    h = (x_ref[...] * (1.0 + sc) + sh).astype(BF16)
    p = jnp.dot(h, w_ref[...], preferred_element_type=F32)
    dq = C_HEADS * C_HEAD_DIM
    o_ref[:, 0:dq] = (p[:, 0:dq] * (C_HEAD_DIM ** -0.5)).astype(BF16)
    o_ref[:, dq:] = p[:, dq:].astype(BF16)


def _odd_proj(x, mod_l, w_qkv, batch, seq):
    n, d = x.shape
    tm = min(PROJ_TILE, seq)
    nt = seq // tm
    w = w_qkv.astype(BF16)
    return pl.pallas_call(
        _odd_proj_kernel,
        grid=(batch, nt),
        in_specs=[pl.BlockSpec((tm, d), lambda b, t: (b * nt + t, 0)),
                  pl.BlockSpec((1, 6, d), lambda b, t: (b, 0, 0)), _const_spec(w.shape)],
        out_specs=pl.BlockSpec((tm, w.shape[1]), lambda b, t: (b * nt + t, 0)),
        out_shape=jax.ShapeDtypeStruct((n, w.shape[1]), BF16),
        compiler_params=_cparams(("parallel", "parallel")),
    )(x, mod_l, w)


def _stick_kernel(tq, kb, q_ref, k_ref, v_ref, o_ref, acc_ref):
    i = pl.program_id(2)
    per_tile = tq // kb
    q2 = q_ref[...]
    lane = _lane_iota((tq, LANES))
    tri = jnp.where(lax.broadcasted_iota(jnp.int32, (kb, kb), 0) >= lax.broadcasted_iota(jnp.int32, (kb, kb), 1),
                    1.0, 0.0).astype(BF16)
    row = lax.broadcasted_iota(jnp.int32, (tq, kb), 0)
    col = lax.broadcasted_iota(jnp.int32, (tq, kb), 1)
    q_heads = (jnp.where(lane < HALF, q2, jnp.zeros_like(q2)), jnp.where(lane < HALF, jnp.zeros_like(q2), q2))
    acc_ref[...] = jnp.zeros(acc_ref.shape, F32)

    def blocks(js, carries, diags):
        streams = [(u, hh) for u in range(len(js)) for hh in range(2)]
        kv = []
        for j in js:
            start = pl.multiple_of(j * kb, kb)
            kv.append((k_ref[pl.ds(start, kb), :], v_ref[pl.ds(start, kb), :]))
        before = [None if d is None else (col + d * kb < row) for d in diags]
        z, inc = {}, {}
        carry = list(carries)

        r0 = [0 if d is None else d * kb for d in diags]

        def stage_z(s):
            z[s] = lax.dot_general(q_heads[s[1]][r0[s[0]]:], kv[s[0]][0], _NT, preferred_element_type=F32)

        def stage_sum(s):
            sp = jnp.where(z[s] > SOFTPLUS_LINEAR, z[s], jnp.log(1.0 + jnp.exp(z[s])))
            if before[s[0]] is not None:
                sp = jnp.where(before[s[0]][r0[s[0]]:], sp, 0.0)
            inc[s] = jnp.dot(sp.astype(BF16), tri, preferred_element_type=F32)

        def stage_out(s):
            u, hh = s
            a = jnp.exp(z[s] - inc[s] - carry[hh][r0[u]:])
            if before[u] is not None:
                a = jnp.where(before[u][r0[u]:], a, 0.0)
            acc_ref[hh, r0[u]:, :] = acc_ref[hh, r0[u]:, :] + jnp.dot(a.astype(BF16), kv[u][1],
                                                                      preferred_element_type=F32)
            tot = carry[hh][r0[u]:] + inc[s][:, 0:1]
            carry[hh] = tot if r0[u] == 0 else jnp.concatenate([carry[hh][:r0[u]], tot], axis=0)

        for t in range(len(streams) + 2):
            if t < len(streams):
                stage_z(streams[t])
            if 0 <= t - 1 < len(streams):
                stage_sum(streams[t - 1])
            if 0 <= t - 2 < len(streams):
                stage_out(streams[t - 2])
        return tuple(carry)

    carries = (jnp.zeros((tq, 1), F32), jnp.zeros((tq, 1), F32))
    diag = list(reversed(range(per_tile)))
    carries = blocks([i * per_tile + d for d in diag], carries, diag)
    n_full = i * per_tile
    lax.fori_loop(0, i, lambda s, cs: blocks([n_full - 1 - s * per_tile - u for u in range(per_tile)], cs,
                                             [None] * per_tile), carries)
    o_ref[...] = jnp.where(lane < HALF, acc_ref[0], acc_ref[1]).astype(BF16)


def _stick_attention(qkv, batch, seq):
    n = qkv.shape[0]
    tq = min(STICK_Q_TILE, seq)
    kb = min(STICK_KEY_BLOCK, tq)
    n_q = seq // tq
    n_pair = C_HEADS // 2
    return pl.pallas_call(
        functools.partial(_stick_kernel, tq, kb),
        grid=(batch, n_pair, n_q),
        in_specs=[pl.BlockSpec((tq, LANES), lambda b, p, i: (b * n_q + i, p)),
                  pl.BlockSpec((seq, LANES), lambda b, p, i: (b, n_pair + p)),
                  pl.BlockSpec((seq, LANES), lambda b, p, i: (b, 2 * n_pair + p))],
        out_specs=pl.BlockSpec((tq, LANES), lambda b, p, i: (b * n_q + i, p)),
        out_shape=jax.ShapeDtypeStruct((n, C_HEADS * C_HEAD_DIM), BF16),
        scratch_shapes=[pltpu.VMEM((2, tq, LANES), F32)],
        compiler_params=_cparams(("parallel", "parallel", "arbitrary")),
    )(qkv, qkv, qkv)


_HALO = 8


def _mix_ffn_kernel(tm, n_in, *refs):
    x_ref, mod_ref = refs[0], refs[1]
    ins = refs[2:2 + n_in]
    wos = refs[2 + n_in:2 + 2 * n_in]
    mg_ref, mb_ref, wup_ref, cw_ref, cb_ref, wdn_ref, g_ref, b_ref, o_ref, a_buf, tail_buf = refs[2 + 2 * n_in:]
    t = pl.program_id(1)
    mix = jnp.dot(ins[0][...], wos[0][...], preferred_element_type=F32)
    for a_ref, w_ref in zip(ins[1:], wos[1:]):
        mix = mix + jnp.dot(a_ref[...], w_ref[...], preferred_element_type=F32)
    x = _layer_norm(DEEPNORM_ALPHA * x_ref[...] + (1.0 + mod_ref[0, 2:3, :]) * mix, mg_ref[...], mb_ref[...])
    sh = mod_ref[0, 3:4, :]
    sc = mod_ref[0, 4:5, :]
    gate = mod_ref[0, 5:6, :]
    h = (x * (1.0 + sc) + sh).astype(BF16)

    @pl.when(t == 0)
    def _():
        tail_buf[...] = jnp.zeros_like(tail_buf)

    def up(ci):
        cs = slice(ci * FF_CHUNK, (ci + 1) * FF_CHUNK)
        gs = slice(D_FF + ci * FF_CHUNK, D_FF + (ci + 1) * FF_CHUNK)
        return (jnp.dot(h, wup_ref[:, cs], preferred_element_type=F32),
                jnp.dot(h, wup_ref[:, gs], preferred_element_type=F32))

    n_chunk = D_FF // FF_CHUNK
    y = jnp.zeros((tm, x.shape[1]), F32)
    nxt = up(0)
    for ci in range(n_chunk):
        cs = slice(ci * FF_CHUNK, (ci + 1) * FF_CHUNK)
        a, gt = nxt
        if ci + 1 < n_chunk:
            nxt = up(ci + 1)
        a_buf[0:_HALO, :] = tail_buf[:, cs]
        a_buf[_HALO:, :] = a
        tail_buf[:, cs] = a[tm - _HALO:, :]
        conv = (a_buf[_HALO - 2:_HALO - 2 + tm, :] * cw_ref[0:1, cs]
                + a_buf[_HALO - 1:_HALO - 1 + tm, :] * cw_ref[1:2, cs]
                + a * cw_ref[2:3, cs] + cb_ref[:, cs])
        u = conv / (1.0 + jnp.exp(-conv)) * gt
        y = y + jnp.dot(u.astype(BF16), wdn_ref[cs, :], preferred_element_type=F32)
    o_ref[...] = _layer_norm(DEEPNORM_ALPHA * x + (1.0 + gate) * y, g_ref[...], b_ref[...])


def _mix_ffn(x, mod_l, acts, w_out, mix_g, mix_b, w_up, conv_w, conv_b, w_down, ffn_g, ffn_b, batch, seq):
    n, d = x.shape
    tm = min(TOKEN_TILE, seq)
    nt = seq // tm
    row = lambda w: pl.BlockSpec((tm, w), lambda b, t: (b * nt + t, 0))
    single = lambda shape: pl.BlockSpec(shape, lambda b, t: (0,) * len(shape), pipeline_mode=pl.Buffered(1))
    wos, off = [], 0
    for a in acts:
        wos.append(w_out[off:off + a.shape[1]].astype(BF16))
        off += a.shape[1]
    vec = lambda v: v.reshape(1, -1)
    return pl.pallas_call(
        functools.partial(_mix_ffn_kernel, tm, len(acts)),
        grid=(batch, nt),
        in_specs=[row(d), pl.BlockSpec((1, 6, d), lambda b, t: (b, 0, 0))]
                 + [row(a.shape[1]) for a in acts] + [single(w.shape) for w in wos]
                 + [_const_spec((1, d))] * 2
                 + [single((d, 2 * D_FF)), _const_spec((CONV_WIDTH, D_FF)), _const_spec((1, D_FF)), single((D_FF, d)),
                    _const_spec((1, d)), _const_spec((1, d))],
        out_specs=row(d),
        out_shape=jax.ShapeDtypeStruct((n, d), F32),
        scratch_shapes=[pltpu.VMEM((_HALO + tm, FF_CHUNK), F32), pltpu.VMEM((_HALO, D_FF), F32)],
        compiler_params=_cparams(("parallel", "arbitrary")),
    )(x, mod_l, *acts, *wos, vec(mix_g), vec(mix_b), w_up.astype(BF16), conv_w, vec(conv_b), w_down.astype(BF16),
      vec(ffn_g), vec(ffn_b))


def kernel(x, c, positions, mod_w, mod_b, ln_mix_g, ln_mix_b, ln_ffn_g, ln_ffn_b, ev_w_in, ev_idx_k_g, ev_idx_k_b, ev_q_norm_g, ev_kv_norm_g, ev_w_uq, ev_w_ukv, ev_w_out, od_w_qkv, od_w_out, ffn_w_up, ffn_conv_w, ffn_conv_b, ffn_w_down):
    batch, seq, d = x.shape
    assert d == D_MODEL and seq % min(Q_TILE, seq) == 0 and seq % CHUNK == 0
    n = batch * seq
    depth = mod_w.shape[0]
    mod = _modulation(c, mod_w, mod_b).reshape(depth, batch, 6, d)
    tables = _rope_tables(positions)
    xs = x.reshape(n, d)
    for l in range(depth):
        i = l // 2
        if l % 2 == 0:
            weights = _even_weights(ev_w_in[i], ev_idx_k_g[i], ev_idx_k_b[i], ev_q_norm_g[i],
                                    ev_kv_norm_g[i], ev_w_uq[i], ev_w_ukv[i])
            qa, iq, ka0, ka1, va, ik0, ik1, misc, qb, kf, vb = _even_proj(xs, mod[l], tables, weights, batch, seq)
            out_a = _dsa_attention(iq, misc, qa, ik0, ik1, ka0, ka1, va, batch, seq)
            out_b = _mla_attention(qb, kf, vb, batch, seq)
            acts, w_out = [out_a, out_b], ev_w_out[i]
        else:
            qkv = _odd_proj(xs, mod[l], od_w_qkv[i], batch, seq)
            acts, w_out = [_stick_attention(qkv, batch, seq)], od_w_out[i]
        xs = _mix_ffn(xs, mod[l], acts, w_out, ln_mix_g[l], ln_mix_b[l], ffn_w_up[l], ffn_conv_w[l], ffn_conv_b[l],
                      ffn_w_down[l], ln_ffn_g[l], ln_ffn_b[l], batch, seq)
    return xs.reshape(batch, seq, d)
```

```python
import functools

import numpy as np
import jax
import jax.numpy as jnp
from jax import lax
from jax.experimental import pallas as pl
from jax.experimental.pallas import tpu as pltpu

F32 = jnp.float32
BF16 = jnp.bfloat16

D_MODEL = 1024
DEPTH = 4
CHUNK = 64
ROPE_THETA = 10000.0
LN_EPS = 1e-5
RMS_EPS = 1e-6

A_HEADS = 8
A_HEAD_DIM = 64
IDX_HEADS = 4
IDX_DIM = 64
TOPK_MAX = 256

B_HEADS = 8
B_NOPE = 64
B_ROPE = 32
B_V = 64
B_Q_RANK = 384
B_KV_RANK = 256

C_HEADS = 16
C_HEAD_DIM = D_MODEL // C_HEADS

D_FF = 2816
CONV_WIDTH = 3

EVEN_IN_SIZES = (A_HEADS * A_HEAD_DIM, A_HEAD_DIM, A_HEAD_DIM, IDX_HEADS * IDX_DIM, IDX_DIM, IDX_HEADS,
                 B_Q_RANK, B_KV_RANK, B_ROPE)
DEEPNORM_ALPHA = (2 * DEPTH) ** 0.25
LOG2E = 1.4426950408889634

LANES = 128
HALF = 64
TOKEN_TILE = 512
PROJ_TILE = 1024
Q_TILE = 256
STICK_Q_TILE = 1024
STICK_KEY_BLOCK = 256
MLA_Q_TILE = 1024
MLA_KEY_BLOCK = 256
SOFTPLUS_LINEAR = 30.0
FF_CHUNK = 256
VMEM_LIMIT = 56 * 2 ** 20

_P_QA = 0
_P_IQ = _P_QA + A_HEADS * A_HEAD_DIM
_P_CQ = _P_IQ + IDX_HEADS * IDX_DIM
_P_CKV = _P_CQ + B_Q_RANK
_P_GK = _P_CKV + B_KV_RANK
_P_GV = _P_GK + LANES
_P_GI = _P_GV + LANES
_P_G5 = _P_GI + LANES
_P_END = _P_G5 + LANES
_KR_LO = HALF
_KR_HI = HALF + B_ROPE
_IW_LO = _KR_HI

_NT = (((1,), (1,)), ((), ()))


def _cparams(sem):
    return pltpu.CompilerParams(dimension_semantics=sem, vmem_limit_bytes=VMEM_LIMIT)


def _const_spec(shape):
    nd = len(shape)
    return pl.BlockSpec(shape, lambda *_: (0,) * nd)


def _lane_iota(shape):
    return lax.broadcasted_iota(jnp.int32, shape, len(shape) - 1)


def _layer_norm(v, g, b):
    mu = jnp.mean(v, axis=-1, keepdims=True)
    d = v - mu
    var = jnp.mean(d * d, axis=-1, keepdims=True)
    return d * lax.rsqrt(var + LN_EPS) * g + b


def _mod_kernel(c_ref, w_ref, b_ref, o_ref):
    c = c_ref[...]
    ca = c / (1.0 + jnp.exp(-c))
    o_ref[0] = jnp.dot(ca, w_ref[0], preferred_element_type=F32,
                       precision=lax.Precision.HIGHEST) + b_ref[0]


def _modulation(c, mod_w, mod_b):
    depth, d, d6 = mod_w.shape
    b = c.shape[0]
    nj = d6 // d
    return pl.pallas_call(
        _mod_kernel,
        grid=(depth, nj),
        in_specs=[pl.BlockSpec((b, d), lambda l, j: (0, 0)),
                  pl.BlockSpec((1, d, d), lambda l, j: (l, 0, j)),
                  pl.BlockSpec((1, 1, d), lambda l, j: (l, 0, j))],
        out_specs=pl.BlockSpec((1, b, d), lambda l, j: (l, 0, j)),
        out_shape=jax.ShapeDtypeStruct((depth, b, d6), F32),
        compiler_params=_cparams(("parallel", "parallel")),
    )(c, mod_w, mod_b.reshape(depth, 1, d6))


def _rope_kernel(pos_ref, f_ref, ga_ref, gb_ref, ca_ref, sa_ref, cb_ref, sb_ref):
    ang = pos_ref[...] * f_ref[...]
    c, s = jnp.cos(ang), jnp.sin(ang)
    na, nb = A_HEAD_DIM // 2, B_ROPE // 2
    ones = jnp.ones((ang.shape[0], _KR_LO), F32)
    ca_ref[...] = jnp.concatenate([c[:, :na]] * (LANES // na), axis=1)
    sa_ref[...] = jnp.concatenate([s[:, :na]] * (LANES // na), axis=1) * ga_ref[...]
    cb, sb = c[:, na:na + nb], s[:, na:na + nb]
    cb_ref[...] = jnp.concatenate([ones, cb, cb, ones[:, :LANES - _KR_HI]], axis=1)
    sb_ref[...] = jnp.concatenate([0.0 * ones, sb, sb, 0.0 * ones[:, :LANES - _KR_HI]], axis=1) * gb_ref[...]


def _rope_tables(positions):
    n = positions.size
    lane = np.arange(LANES)
    inv_a = ROPE_THETA ** (-jnp.arange(0, A_HEAD_DIM, 2, dtype=F32) / A_HEAD_DIM)
    inv_b = ROPE_THETA ** (-jnp.arange(0, B_ROPE, 2, dtype=F32) / B_ROPE)
    freqs = jnp.concatenate([inv_a, inv_b, jnp.zeros((LANES - inv_a.size - inv_b.size,), F32)])
    ga = jnp.asarray(np.where(lane % A_HEAD_DIM < A_HEAD_DIM // 2, -1.0, 1.0), F32)
    in_rope = (lane >= _KR_LO) & (lane < _KR_HI)
    gb = jnp.asarray(np.where(in_rope, np.where(lane < _KR_LO + B_ROPE // 2, -1.0, 1.0), 0.0), F32)
    tm = 2048 if n % 2048 == 0 else TOKEN_TILE
    vec = lambda v: v.reshape(1, LANES).astype(F32)
    tab = jax.ShapeDtypeStruct((n, LANES), F32)
    return pl.pallas_call(
        _rope_kernel,
        grid=(n // tm,),
        in_specs=[pl.BlockSpec((tm, 1), lambda i: (i, 0))] + [_const_spec((1, LANES))] * 3,
        out_specs=[pl.BlockSpec((tm, LANES), lambda i: (i, 0))] * 4,
        out_shape=[tab] * 4,
        compiler_params=_cparams(("parallel",)),
    )(positions.reshape(n, 1).astype(F32), vec(freqs), vec(ga), vec(gb))


def _rope_a(x, c, s):
    lane = _lane_iota(x.shape)
    partner = jnp.where(lane % A_HEAD_DIM < A_HEAD_DIM // 2,
                        pltpu.roll(x, LANES - A_HEAD_DIM // 2, 1), pltpu.roll(x, A_HEAD_DIM // 2, 1))
    return x * c + partner * s


def _rope_b(x, c, s):
    lane = _lane_iota(x.shape)
    partner = jnp.where(lane < _KR_LO + B_ROPE // 2,
                        pltpu.roll(x, LANES - B_ROPE // 2, 1), pltpu.roll(x, B_ROPE // 2, 1))
    return x * c + partner * s


def _even_proj_kernel(x_ref, mod_ref, win_ref, ca_ref, sa_ref, cb_ref, sb_ref, ikg_ref, ikb_ref,
                      qg_ref, kvg_ref, wuq_ref, wkn_ref, wv_ref,
                      qa_ref, iq_ref, ka0_ref, ka1_ref, va_ref, ik0_ref, ik1_ref, misc_ref,
                      qb_ref, kf_ref, vb_ref):
    sh = mod_ref[0, 0:1, :]
    sc = mod_ref[0, 1:2, :]
    h = (x_ref[...] * (1.0 + sc) + sh).astype(BF16)
    ca, sa, cb, sb = ca_ref[...], sa_ref[...], cb_ref[...], sb_ref[...]
    lane = _lane_iota(ca.shape)
    low = lane < HALF
    kr = []

    def rms(v, g_ref):
        return (v * lax.rsqrt(jnp.mean(v * v, axis=-1, keepdims=True) + RMS_EPS) * g_ref[...]).astype(BF16)

    def misc_group(p):
        g5 = _rope_b(p, cb, sb)
        misc_ref[...] = g5 * (IDX_HEADS ** -0.5)
        kr.append(jnp.where((lane >= _KR_LO) & (lane < _KR_HI), g5, 0.0))

    def kv_latent(p):
        ckvn = rms(p, kvg_ref)
        kn = jnp.dot(ckvn, wkn_ref[...], preferred_element_type=F32)
        for hd in range(B_HEADS):
            sl = slice(hd * LANES, (hd + 1) * LANES)
            kf_ref[:, sl] = (kn[:, sl] + kr[0]).astype(BF16)
        vb_ref[...] = jnp.dot(ckvn, wv_ref[...], preferred_element_type=F32).astype(BF16)

    def q_latent(p):
        qb = jnp.dot(rms(p, qg_ref), wuq_ref[...], preferred_element_type=F32)
        scale_b = (B_NOPE + B_ROPE) ** -0.5 * LOG2E
        for hd in range(B_HEADS):
            sl = slice(hd * LANES, (hd + 1) * LANES)
            qb_ref[:, sl] = (_rope_b(qb[:, sl], cb, sb) * scale_b).astype(BF16)

    def roped_heads(o_ref, scale):
        def store(p):
            for g in range(p.shape[1] // LANES):
                sl = slice(g * LANES, (g + 1) * LANES)
                o_ref[:, sl] = (_rope_a(p[:, sl], ca, sa) * scale).astype(BF16)
        return store

    def dsa_key(p):
        ka = _rope_a(p, ca, sa)
        ka0_ref[...] = jnp.where(low, ka, 0.0).astype(BF16)
        ka1_ref[...] = jnp.where(low, 0.0, ka).astype(BF16)

    def dsa_value(p):
        va_ref[...] = p.astype(BF16)

    def indexer_key(p):
        mu = jnp.sum(jnp.where(low, p, 0.0), axis=-1, keepdims=True) * (1.0 / IDX_DIM)
        d = p - mu
        var = jnp.sum(jnp.where(low, d * d, 0.0), axis=-1, keepdims=True) * (1.0 / IDX_DIM)
        ik = _rope_a(d * lax.rsqrt(var + LN_EPS) * ikg_ref[...] + ikb_ref[...], ca, sa)
        ik0_ref[...] = jnp.where(low, ik, 0.0).astype(BF16)
        ik1_ref[...] = jnp.where(low, 0.0, ik).astype(BF16)

    both = lambda first, second: lambda p: (first(p[:, :LANES]), second(p[:, LANES:]))
    stages = [(_P_GI, 2 * LANES, both(indexer_key, misc_group)), (_P_CKV, B_KV_RANK, kv_latent),
              (_P_CQ, B_Q_RANK, q_latent),
              (_P_QA, A_HEADS * A_HEAD_DIM, roped_heads(qa_ref, A_HEAD_DIM ** -0.5 * LOG2E)),
              (_P_IQ, IDX_HEADS * IDX_DIM, roped_heads(iq_ref, IDX_DIM ** -0.5)),
              (_P_GK, 2 * LANES, both(dsa_key, dsa_value))]
    project = lambda off, width: jnp.dot(h, win_ref[:, off:off + width], preferred_element_type=F32)
    nxt = project(*stages[0][:2])
    for s, (_, _, epilogue) in enumerate(stages):
        cur = nxt
        if s + 1 < len(stages):
            nxt = project(*stages[s + 1][:2])
        epilogue(cur)


def _even_weights(w_in, idx_k_g, idx_k_b, q_norm_g, kv_norm_g, w_uq, w_ukv):
    d = w_in.shape[0]
    qa, ka, va, iq, ik, iw, cq, ckv, kr = jnp.split(w_in, np.cumsum(EVEN_IN_SIZES)[:-1].tolist(), axis=1)
    z = lambda n: jnp.zeros((d, n), w_in.dtype)
    win = jnp.concatenate([qa, iq, cq, ckv, ka, ka, va, va, ik, ik,
                           z(_KR_LO), kr, iw, z(LANES - _IW_LO - IDX_HEADS)], axis=1).astype(BF16)
    assert win.shape[1] == _P_END
    pad_head = B_NOPE + B_ROPE
    wuq = jnp.pad(w_uq.reshape(B_Q_RANK, B_HEADS, pad_head), ((0, 0), (0, 0), (0, LANES - pad_head)))
    wuq = wuq.reshape(B_Q_RANK, B_HEADS * LANES).astype(BF16)
    wkv = w_ukv.reshape(B_KV_RANK, B_HEADS, B_NOPE + B_V)
    wkn = jnp.pad(wkv[:, :, :B_NOPE], ((0, 0), (0, 0), (0, LANES - B_NOPE)))
    wkn = wkn.reshape(B_KV_RANK, B_HEADS * LANES).astype(BF16)
    wv = wkv[:, :, B_NOPE:].reshape(B_KV_RANK, B_HEADS * B_V).astype(BF16)
    two = lambda v: jnp.concatenate([v, v]).reshape(1, LANES).astype(F32)
    return (win, two(idx_k_g), two(idx_k_b), q_norm_g.reshape(1, -1).astype(F32),
            kv_norm_g.reshape(1, -1).astype(F32), wuq, wkn, wv)


def _even_proj(x, mod_l, tables, weights, batch, seq):
    n, d = x.shape
    tm = min(PROJ_TILE, seq)
    nt = seq // tm
    win, ikg, ikb, qg, kvg, wuq, wkn, wv = weights
    row = lambda w: pl.BlockSpec((tm, w), lambda b, t: (b * nt + t, 0))
    bf = lambda w: jax.ShapeDtypeStruct((n, w), BF16)
    out_widths = [A_HEADS * A_HEAD_DIM, IDX_HEADS * IDX_DIM, LANES, LANES, LANES, LANES, LANES]
    out_shape = [bf(w) for w in out_widths] + [jax.ShapeDtypeStruct((n, LANES), F32)] + \
                [bf(B_HEADS * LANES), bf(B_HEADS * LANES), bf(B_HEADS * B_V)]
    out_specs = [row(w) for w in out_widths] + [row(LANES)] + \
                [row(B_HEADS * LANES), row(B_HEADS * LANES), row(B_HEADS * B_V)]
    return pl.pallas_call(
        _even_proj_kernel,
        grid=(batch, nt),
        in_specs=[row(d), pl.BlockSpec((1, 6, d), lambda b, t: (b, 0, 0)), _const_spec(win.shape)]
                 + [row(LANES)] * 4
                 + [_const_spec(a.shape) for a in (ikg, ikb, qg, kvg, wuq, wkn, wv)],
        out_specs=out_specs,
        out_shape=out_shape,
        compiler_params=_cparams(("parallel", "parallel")),
    )(x, mod_l, win, *tables, ikg, ikb, qg, kvg, wuq, wkn, wv)


def _dsa_kernel(tq, topk, iq_ref, misc_ref, qa_ref, ik0_ref, ik1_ref, ka0_ref, ka1_ref, va_ref,
                o_ref, key_buf, hi_buf, lo_buf, lg_buf, m_buf, l_buf, acc_buf):
    i = pl.program_id(1)
    n_blk = i + 1
    int_min = jnp.int32(-2 ** 31)
    min16 = jnp.int16(-2 ** 15)
    one16, zero16 = jnp.int16(1), jnp.int16(0)
    kf = jnp.float32(topk)
    n_pair = A_HEADS // 2
    rows = n_pair * tq

    def rows_of(j):
        return pl.ds(pl.multiple_of(j * tq, tq), tq)

    iw_t = misc_ref[...].T
    iq = iq_ref[...]

    def score_block(j, diagonal):
        rels = [lax.dot_general(ik_ref[rows_of(j), :], iq[:, pair * LANES:(pair + 1) * LANES], _NT,
                                preferred_element_type=F32)
                for pair in range(IDX_HEADS // 2) for ik_ref in (ik0_ref, ik1_ref)]
        score = jnp.zeros((tq, tq), F32)
        for hd, rel in enumerate(rels):
            score = score + jnp.maximum(rel, 0.0) * iw_t[_IW_LO + hd:_IW_LO + hd + 1, :]
        score = jnp.where(score == 0.0, 0.0, score)
        bits = lax.bitcast_convert_type(score, jnp.int32)
        key = bits ^ (lax.shift_right_arithmetic(bits, 31) & jnp.int32(0x7FFFFFFF))
        if diagonal:
            key_chunk = lax.shift_right_logical(lax.broadcasted_iota(jnp.int32, (tq, tq), 0), 6)
            query_chunk = lax.shift_right_logical(lax.broadcasted_iota(jnp.int32, (tq, tq), 1), 6)
            key = jnp.where(key_chunk <= query_chunk, key, int_min)
        key_buf[rows_of(j), :] = key
        hi_buf[rows_of(j), :] = lax.shift_right_arithmetic(key, 16).astype(jnp.int16)

    lax.fori_loop(0, i, lambda j, c: (score_block(j, False), c)[1], 0)
    score_block(i, True)

    def count(pred, buf):
        def body(j, acc):
            hit = jnp.where(pred(buf[rows_of(j), :]), one16, zero16)
            for r in range(tq // 16):
                acc = acc + hit[r * 16:(r + 1) * 16]
            return acc
        acc = lax.fori_loop(0, n_blk, body, jnp.zeros((16, tq), jnp.int16))
        return jnp.sum(acc.astype(jnp.int32), axis=0, keepdims=True).astype(F32)

    def search(buf, base):
        def bit(b, cu):
            cand = cu | lax.shift_left(jnp.int32(1), 15 - b)
            image = (cand ^ jnp.int32(0x8000)).astype(jnp.int16)
            return jnp.where(base + count(lambda x: x >= image, buf) >= kf, cand, cu)

        return lax.fori_loop(0, 16, bit, jnp.zeros((1, tq), jnp.int32))

    hi_cu = search(hi_buf, 0.0)
    hi16 = (hi_cu ^ jnp.int32(0x8000)).astype(jnp.int16)

    def low_prep(j, _):
        low = ((key_buf[rows_of(j), :] & jnp.int32(0xFFFF)) ^ jnp.int32(0x8000)).astype(jnp.int16)
        lo_buf[rows_of(j), :] = jnp.where(hi_buf[rows_of(j), :] == hi16, low, min16)
        return 0

    lax.fori_loop(0, n_blk, low_prep, 0)
    n_above = count(lambda x: x > hi16, hi_buf)
    lo_cu = search(lo_buf, n_above)
    lo16 = (lo_cu ^ jnp.int32(0x8000)).astype(jnp.int16)
    thr = lax.shift_left(hi_cu ^ jnp.int32(0x8000), 16) | lo_cu
    need = kf - (n_above + count(lambda x: x > lo16, lo_buf))
    tri = jnp.where(lax.broadcasted_iota(jnp.int32, (tq, tq), 0) >= lax.broadcasted_iota(jnp.int32, (tq, tq), 1),
                    1.0, 0.0).astype(BF16)

    qa = qa_ref[...]
    q_stack = jnp.concatenate([qa[:, g * LANES:(g + 1) * LANES] for g in range(n_pair)], axis=0)
    m_buf[...] = jnp.full(m_buf.shape, -jnp.inf, F32)

    def logits_block(j, ties_before):
        k = key_buf[rows_of(j), :]
        tied = jnp.where(k == thr, 1.0, 0.0)
        rank = jnp.dot(tri, tied.astype(BF16), preferred_element_type=F32) + ties_before
        keep = jnp.where(k == int_min, 0.0,
                         jnp.where(k > thr, 1.0, jnp.where(rank <= need, tied, 0.0)))
        bias = jnp.where(keep.T > 0.5, 0.0, -jnp.inf)
        bias = jnp.concatenate([bias] * n_pair, axis=0)
        lgs = [lax.dot_general(q_stack, ka_ref[rows_of(j), :], _NT, preferred_element_type=F32)
               for ka_ref in (ka0_ref, ka1_ref)]
        for half, lg in enumerate(lgs):
            sl = slice(half * rows, (half + 1) * rows)
            lg = lg + bias
            lg_buf[j, sl, :] = lg
            folded = jnp.maximum(lg[:, :LANES], lg[:, LANES:]) if tq == 2 * LANES else lg
            m_buf[sl, :] = jnp.maximum(m_buf[sl, :], folded)
        return rank[tq - 1:tq, :]

    lax.fori_loop(0, n_blk, logits_block, jnp.zeros((1, tq), F32))
    m_buf[...] = jnp.broadcast_to(jnp.max(m_buf[...], axis=-1, keepdims=True), m_buf.shape)
    l_buf[...] = jnp.zeros(l_buf.shape, F32)
    acc_buf[...] = jnp.zeros(acc_buf.shape, F32)

    def pv_block(j, _):
        v = va_ref[rows_of(j), :]
        for half in range(2):
            sl = slice(half * rows, (half + 1) * rows)
            m = m_buf[sl, :]
            e = jnp.exp2(lg_buf[j, sl, :] - jnp.concatenate([m] * (tq // LANES), axis=1))
            l_buf[sl, :] = l_buf[sl, :] + (e[:, :LANES] + e[:, LANES:] if tq == 2 * LANES else e)
            acc_buf[sl, :] = acc_buf[sl, :] + jnp.dot(e.astype(BF16), v, preferred_element_type=F32)
        return 0

    lax.fori_loop(0, n_blk, pv_block, 0)
    out = acc_buf[...] / jnp.sum(l_buf[...], axis=-1, keepdims=True)
    lane = _lane_iota((tq, LANES))
    for g in range(n_pair):
        o_ref[:, g * LANES:(g + 1) * LANES] = jnp.where(
            lane < HALF, out[g * tq:(g + 1) * tq], out[rows + g * tq:rows + (g + 1) * tq]).astype(BF16)


def _dsa_attention(iq, misc, qa, ik0, ik1, ka0, ka1, va, batch, seq):
    n = qa.shape[0]
    tq = min(Q_TILE, seq)
    assert tq % LANES == 0 and tq // LANES in (1, 2) and seq <= 2 ** 15
    n_q = seq // tq
    topk = min(TOPK_MAX, seq // 4)
    rows = A_HEADS * tq
    qrow = lambda w: pl.BlockSpec((tq, w), lambda b, i: (b * n_q + i, 0))
    krow = pl.BlockSpec((seq, LANES), lambda b, i: (b, 0))
    return pl.pallas_call(
        functools.partial(_dsa_kernel, tq, topk),
        grid=(batch, n_q),
        in_specs=[qrow(iq.shape[1]), qrow(LANES), qrow(qa.shape[1])] + [krow] * 5,
        out_specs=qrow(qa.shape[1]),
        out_shape=jax.ShapeDtypeStruct((n, qa.shape[1]), BF16),
        scratch_shapes=[pltpu.VMEM((seq, tq), jnp.int32), pltpu.VMEM((seq, tq), jnp.int16),
                        pltpu.VMEM((seq, tq), jnp.int16), pltpu.VMEM((n_q, rows, tq), F32),
                        pltpu.VMEM((rows, LANES), F32), pltpu.VMEM((rows, LANES), F32),
                        pltpu.VMEM((rows, LANES), F32)],
        compiler_params=_cparams(("parallel", "arbitrary")),
    )(iq, misc, qa, ik0, ik1, ka0, ka1, va)


def _mla_kernel(tq, kb, q_ref, k_ref, v_ref, o_ref, lg_buf, m_buf, l_buf, acc_buf):
    i = pl.program_id(2)
    per_tile = tq // kb
    q = [q_ref[:, hh * LANES:(hh + 1) * LANES] for hh in range(2)]
    fold = lambda x, op: op(x[:, :LANES], x[:, LANES:]) if kb == 2 * LANES else x

    def rows_of(j):
        return pl.ds(pl.multiple_of(j * kb, kb), kb)

    def first_row(d):
        return 0 if d is None else d * kb

    def logits(js, diags):
        streams = [(u, hh) for u in range(len(js)) for hh in range(2)]
        lg = {}

        def stage_dot(s):
            u, hh = s
            lg[s] = lax.dot_general(q[hh][first_row(diags[u]):], k_ref[rows_of(js[u]), hh * LANES:(hh + 1) * LANES],
                                    _NT, preferred_element_type=F32)

        def stage_store(s):
            u, hh = s
            r0 = first_row(diags[u])
            x = lg[s]
            if diags[u] is not None:
                key_idx = lax.broadcasted_iota(jnp.int32, x.shape, 1) + diags[u] * kb
                query_idx = lax.broadcasted_iota(jnp.int32, x.shape, 0) + r0
                x = jnp.where(lax.shift_right_logical(key_idx, 6) <= lax.shift_right_logical(query_idx, 6),
                              x, -jnp.inf)
            lg_buf[hh, js[u], r0:, :] = x
            m_buf[hh, r0:, :] = jnp.maximum(m_buf[hh, r0:, :], fold(x, jnp.maximum))

        for t in range(len(streams) + 1):
            if t < len(streams):
                stage_dot(streams[t])
            if t >= 1:
                stage_store(streams[t - 1])

    def weighted(js, diags):
        streams = [(u, hh) for u in range(len(js)) for hh in range(2)]
        e = {}

        def stage_exp(s):
            u, hh = s
            r0 = first_row(diags[u])
            x = jnp.exp2(lg_buf[hh, js[u], r0:, :] - jnp.concatenate([m_buf[hh, r0:, :]] * (kb // LANES), axis=1))
            l_buf[hh, r0:, :] = l_buf[hh, r0:, :] + fold(x, jnp.add)
            e[s] = x.astype(BF16)

        def stage_pv(s):
            u, hh = s
            r0 = first_row(diags[u])
            acc_buf[hh, r0:, :] = acc_buf[hh, r0:, :] + jnp.dot(e[s], v_ref[rows_of(js[u]), :],
                                                              preferred_element_type=F32)

        for t in range(len(streams) + 1):
            if t < len(streams):
                stage_exp(streams[t])
            if t >= 1:
                stage_pv(streams[t - 1])

    diag = list(range(per_tile))
    n_full = i * per_tile
    full = lambda s: [s * per_tile + u for u in range(per_tile)]
    none = [None] * per_tile

    m_buf[...] = jnp.full(m_buf.shape, -jnp.inf, F32)
    logits([n_full + d for d in diag], diag)
    lax.fori_loop(0, i, lambda s, c: (logits(full(s), none), c)[1], 0)
    for hh in range(2):
        m_buf[hh] = jnp.broadcast_to(jnp.max(m_buf[hh], axis=-1, keepdims=True), (tq, LANES))
    l_buf[...] = jnp.zeros(l_buf.shape, F32)
    acc_buf[...] = jnp.zeros(acc_buf.shape, F32)
    weighted([n_full + d for d in diag], diag)
    lax.fori_loop(0, i, lambda s, c: (weighted(full(s), none), c)[1], 0)
    lane = _lane_iota((tq, LANES))
    outs = [acc_buf[hh] / jnp.sum(l_buf[hh], axis=-1, keepdims=True) for hh in range(2)]
    o_ref[...] = jnp.where(lane < HALF, outs[0], outs[1]).astype(BF16)


def _mla_attention(qb, kf, vb, batch, seq):
    n = qb.shape[0]
    tq = min(MLA_Q_TILE, seq)
    kb = min(MLA_KEY_BLOCK, tq)
    assert kb % LANES == 0 and kb // LANES in (1, 2) and kb % CHUNK == 0
    n_q = seq // tq
    n_pair = B_HEADS // 2
    return pl.pallas_call(
        functools.partial(_mla_kernel, tq, kb),
        grid=(batch, n_pair, n_q),
        in_specs=[pl.BlockSpec((tq, 2 * LANES), lambda b, p, i: (b * n_q + i, p)),
                  pl.BlockSpec((seq, 2 * LANES), lambda b, p, i: (b, p)),
                  pl.BlockSpec((seq, LANES), lambda b, p, i: (b, p))],
        out_specs=pl.BlockSpec((tq, LANES), lambda b, p, i: (b * n_q + i, p)),
        out_shape=jax.ShapeDtypeStruct((n, B_HEADS * B_V), BF16),
        scratch_shapes=[pltpu.VMEM((2, seq // kb, tq, kb), F32), pltpu.VMEM((2, tq, LANES), F32),
                        pltpu.VMEM((2, tq, LANES), F32), pltpu.VMEM((2, tq, LANES), F32)],
        compiler_params=_cparams(("parallel", "parallel", "arbitrary")),
    )(qb, kf, vb)


def _odd_proj_kernel(x_ref, mod_ref, w_ref, o_ref):
    sh = mod_ref[0, 0:1, :]
    sc = mod_ref[0, 1:2, :]
    h = (x_ref[...] * (1.0 + sc) + sh).astype(BF16)
    p = jnp.dot(h, w_ref[...], preferred_element_type=F32)
    dq = C_HEADS * C_HEAD_DIM
    o_ref[:, 0:dq] = (p[:, 0:dq] * (C_HEAD_DIM ** -0.5)).astype(BF16)
    o_ref[:, dq:] = p[:, dq:].astype(BF16)


def _odd_proj(x, mod_l, w_qkv, batch, seq):
    n, d = x.shape
    tm = min(PROJ_TILE, seq)
    nt = seq // tm
    w = w_qkv.astype(BF16)
    return pl.pallas_call(
        _odd_proj_kernel,
        grid=(batch, nt),
        in_specs=[pl.BlockSpec((tm, d), lambda b, t: (b * nt + t, 0)),
                  pl.BlockSpec((1, 6, d), lambda b, t: (b, 0, 0)), _const_spec(w.shape)],
        out_specs=pl.BlockSpec((tm, w.shape[1]), lambda b, t: (b * nt + t, 0)),
        out_shape=jax.ShapeDtypeStruct((n, w.shape[1]), BF16),
        compiler_params=_cparams(("parallel", "parallel")),
    )(x, mod_l, w)


def _stick_kernel(tq, kb, q_ref, k_ref, v_ref, o_ref, acc_ref):
    i = pl.program_id(2)
    per_tile = tq // kb
    q2 = q_ref[...]
    lane = _lane_iota((tq, LANES))
    tri = jnp.where(lax.broadcasted_iota(jnp.int32, (kb, kb), 0) >= lax.broadcasted_iota(jnp.int32, (kb, kb), 1),
                    1.0, 0.0).astype(BF16)
    row = lax.broadcasted_iota(jnp.int32, (tq, kb), 0)
    col = lax.broadcasted_iota(jnp.int32, (tq, kb), 1)
    q_heads = (jnp.where(lane < HALF, q2, jnp.zeros_like(q2)), jnp.where(lane < HALF, jnp.zeros_like(q2), q2))
    acc_ref[...] = jnp.zeros(acc_ref.shape, F32)

    def blocks(js, carries, diags):
        streams = [(u, hh) for u in range(len(js)) for hh in range(2)]
        kv = []
        for j in js:
            start = pl.multiple_of(j * kb, kb)
            kv.append((k_ref[pl.ds(start, kb), :], v_ref[pl.ds(start, kb), :]))
        before = [None if d is None else (col + d * kb < row) for d in diags]
        z, inc = {}, {}
        carry = list(carries)

        r0 = [0 if d is None else d * kb for d in diags]

        def stage_z(s):
            z[s] = lax.dot_general(q_heads[s[1]][r0[s[0]]:], kv[s[0]][0], _NT, preferred_element_type=F32)

        def stage_sum(s):
            sp = jnp.where(z[s] > SOFTPLUS_LINEAR, z[s], jnp.log(1.0 + jnp.exp(z[s])))
            if before[s[0]] is not None:
                sp = jnp.where(before[s[0]][r0[s[0]]:], sp, 0.0)
            inc[s] = jnp.dot(sp.astype(BF16), tri, preferred_element_type=F32)

        def stage_out(s):
            u, hh = s
            a = jnp.exp(z[s] - inc[s] - carry[hh][r0[u]:])
            if before[u] is not None:
                a = jnp.where(before[u][r0[u]:], a, 0.0)
            acc_ref[hh, r0[u]:, :] = acc_ref[hh, r0[u]:, :] + jnp.dot(a.astype(BF16), kv[u][1],
                                                                      preferred_element_type=F32)
            tot = carry[hh][r0[u]:] + inc[s][:, 0:1]
            carry[hh] = tot if r0[u] == 0 else jnp.concatenate([carry[hh][:r0[u]], tot], axis=0)

        for t in range(len(streams) + 2):
            if t < len(streams):
                stage_z(streams[t])
            if 0 <= t - 1 < len(streams):
                stage_sum(streams[t - 1])
            if 0 <= t - 2 < len(streams):
                stage_out(streams[t - 2])
        return tuple(carry)

    carries = (jnp.zeros((tq, 1), F32), jnp.zeros((tq, 1), F32))
    diag = list(reversed(range(per_tile)))
    carries = blocks([i * per_tile + d for d in diag], carries, diag)
    n_full = i * per_tile
    lax.fori_loop(0, i, lambda s, cs: blocks([n_full - 1 - s * per_tile - u for u in range(per_tile)], cs,
                                             [None] * per_tile), carries)
    o_ref[...] = jnp.where(lane < HALF, acc_ref[0], acc_ref[1]).astype(BF16)


def _stick_attention(qkv, batch, seq):
    n = qkv.shape[0]
    tq = min(STICK_Q_TILE, seq)
    kb = min(STICK_KEY_BLOCK, tq)
    n_q = seq // tq
    n_pair = C_HEADS // 2
    return pl.pallas_call(
        functools.partial(_stick_kernel, tq, kb),
        grid=(batch, n_pair, n_q),
        in_specs=[pl.BlockSpec((tq, LANES), lambda b, p, i: (b * n_q + i, p)),
                  pl.BlockSpec((seq, LANES), lambda b, p, i: (b, n_pair + p)),
                  pl.BlockSpec((seq, LANES), lambda b, p, i: (b, 2 * n_pair + p))],
        out_specs=pl.BlockSpec((tq, LANES), lambda b, p, i: (b * n_q + i, p)),
        out_shape=jax.ShapeDtypeStruct((n, C_HEADS * C_HEAD_DIM), BF16),
        scratch_shapes=[pltpu.VMEM((2, tq, LANES), F32)],
        compiler_params=_cparams(("parallel", "parallel", "arbitrary")),
    )(qkv, qkv, qkv)


_HALO = 8


def _mix_ffn_kernel(tm, n_in, *refs):
    x_ref, mod_ref = refs[0], refs[1]
    ins = refs[2:2 + n_in]
    wos = refs[2 + n_in:2 + 2 * n_in]
    mg_ref, mb_ref, wup_ref, cw_ref, cb_ref, wdn_ref, g_ref, b_ref, o_ref, a_buf, tail_buf = refs[2 + 2 * n_in:]
    t = pl.program_id(1)
    sh = mod_ref[0, 3:4, :]
    sc = mod_ref[0, 4:5, :]
    gate = mod_ref[0, 5:6, :]

    @pl.when(t == 0)
    def _():
        tail_buf[...] = jnp.zeros_like(tail_buf)

    def up(hh, ci):
        cs = slice(ci * FF_CHUNK, (ci + 1) * FF_CHUNK)
        gs = slice(D_FF + ci * FF_CHUNK, D_FF + (ci + 1) * FF_CHUNK)
        return (jnp.dot(hh, wup_ref[:, cs], preferred_element_type=F32),
                jnp.dot(hh, wup_ref[:, gs], preferred_element_type=F32))

    halves = [slice(0, tm // 2), slice(tm // 2, tm)]
    mixes = []
    for rs in halves:
        mix = jnp.dot(ins[0][rs, :], wos[0][...], preferred_element_type=F32)
        for a_ref, w_ref in zip(ins[1:], wos[1:]):
            mix = mix + jnp.dot(a_ref[rs, :], w_ref[...], preferred_element_type=F32)
        mixes.append(mix)
    xs, hs, first = [], [], []
    for rs, mix in zip(halves, mixes):
        xh = _layer_norm(DEEPNORM_ALPHA * x_ref[rs, :] + (1.0 + mod_ref[0, 2:3, :]) * mix, mg_ref[...], mb_ref[...])
        hh = (xh * (1.0 + sc) + sh).astype(BF16)
        xs.append(xh)
        hs.append(hh)
        first.append(up(hh, 0))
    x = jnp.concatenate(xs, axis=0)
    h = jnp.concatenate(hs, axis=0)
    nxt = tuple(jnp.concatenate([f[k] for f in first], axis=0) for k in range(2))

    n_chunk = D_FF // FF_CHUNK
    y = jnp.zeros((tm, x.shape[1]), F32)
    for ci in range(n_chunk):
        cs = slice(ci * FF_CHUNK, (ci + 1) * FF_CHUNK)
        a, gt = nxt
        if ci + 1 < n_chunk:
            nxt = up(h, ci + 1)
        a_buf[0:_HALO, :] = tail_buf[:, cs]
        a_buf[_HALO:, :] = a
        tail_buf[:, cs] = a[tm - _HALO:, :]
        conv = (a_buf[_HALO - 2:_HALO - 2 + tm, :] * cw_ref[0:1, cs]
                + a_buf[_HALO - 1:_HALO - 1 + tm, :] * cw_ref[1:2, cs]
                + a * cw_ref[2:3, cs] + cb_ref[:, cs])
        u = conv / (1.0 + jnp.exp(-conv)) * gt
        y = y + jnp.dot(u.astype(BF16), wdn_ref[cs, :], preferred_element_type=F32)
    o_ref[...] = _layer_norm(DEEPNORM_ALPHA * x + (1.0 + gate) * y, g_ref[...], b_ref[...])


def _mix_ffn(x, mod_l, acts, w_out, mix_g, mix_b, w_up, conv_w, conv_b, w_down, ffn_g, ffn_b, batch, seq):
    n, d = x.shape
    tm = min(TOKEN_TILE, seq)
    nt = seq // tm
    row = lambda w: pl.BlockSpec((tm, w), lambda b, t: (b * nt + t, 0))
    single = lambda shape: pl.BlockSpec(shape, lambda b, t: (0,) * len(shape), pipeline_mode=pl.Buffered(1))
    wos, off = [], 0
    for a in acts:
        wos.append(w_out[off:off + a.shape[1]].astype(BF16))
        off += a.shape[1]
    vec = lambda v: v.reshape(1, -1)
    return pl.pallas_call(
        functools.partial(_mix_ffn_kernel, tm, len(acts)),
        grid=(batch, nt),
        in_specs=[row(d), pl.BlockSpec((1, 6, d), lambda b, t: (b, 0, 0))]
                 + [row(a.shape[1]) for a in acts] + [single(w.shape) for w in wos]
                 + [_const_spec((1, d))] * 2
                 + [single((d, 2 * D_FF)), _const_spec((CONV_WIDTH, D_FF)), _const_spec((1, D_FF)), single((D_FF, d)),
                    _const_spec((1, d)), _const_spec((1, d))],
        out_specs=row(d),
        out_shape=jax.ShapeDtypeStruct((n, d), F32),
        scratch_shapes=[pltpu.VMEM((_HALO + tm, FF_CHUNK), F32), pltpu.VMEM((_HALO, D_FF), F32)],
        compiler_params=_cparams(("parallel", "arbitrary")),
    )(x, mod_l, *acts, *wos, vec(mix_g), vec(mix_b), w_up.astype(BF16), conv_w, vec(conv_b), w_down.astype(BF16),
      vec(ffn_g), vec(ffn_b))


def kernel(x, c, positions, mod_w, mod_b, ln_mix_g, ln_mix_b, ln_ffn_g, ln_ffn_b, ev_w_in, ev_idx_k_g, ev_idx_k_b, ev_q_norm_g, ev_kv_norm_g, ev_w_uq, ev_w_ukv, ev_w_out, od_w_qkv, od_w_out, ffn_w_up, ffn_conv_w, ffn_conv_b, ffn_w_down):
    batch, seq, d = x.shape
    assert d == D_MODEL and seq % min(Q_TILE, seq) == 0 and seq % CHUNK == 0
    n = batch * seq
    depth = mod_w.shape[0]
    mod = _modulation(c, mod_w, mod_b).reshape(depth, batch, 6, d)
    tables = _rope_tables(positions)
    xs = x.reshape(n, d)
    for l in range(depth):
        i = l // 2
        if l % 2 == 0:
            weights = _even_weights(ev_w_in[i], ev_idx_k_g[i], ev_idx_k_b[i], ev_q_norm_g[i],
                                    ev_kv_norm_g[i], ev_w_uq[i], ev_w_ukv[i])
            qa, iq, ka0, ka1, va, ik0, ik1, misc, qb, kf, vb = _even_proj(xs, mod[l], tables, weights, batch, seq)
            out_a = _dsa_attention(iq, misc, qa, ik0, ik1, ka0, ka1, va, batch, seq)
            out_b = _mla_attention(qb, kf, vb, batch, seq)
            acts, w_out = [out_a, out_b], ev_w_out[i]
        else:
            qkv = _odd_proj(xs, mod[l], od_w_qkv[i], batch, seq)
            acts, w_out = [_stick_attention(qkv, batch, seq)], od_w_out[i]
        xs = _mix_ffn(xs, mod[l], acts, w_out, ln_mix_g[l], ln_mix_b[l], ffn_w_up[l], ffn_conv_w[l], ffn_conv_b[l],
                      ffn_w_down[l], ln_ffn_g[l], ln_ffn_b[l], batch, seq)
    return xs.reshape(batch, seq, d)
```

```python
import functools

import numpy as np
import jax
import jax.numpy as jnp
from jax import lax
from jax.experimental import pallas as pl
from jax.experimental.pallas import tpu as pltpu

F32 = jnp.float32
BF16 = jnp.bfloat16

D_MODEL = 1024
DEPTH = 4
CHUNK = 64
ROPE_THETA = 10000.0
LN_EPS = 1e-5
RMS_EPS = 1e-6

A_HEADS = 8
A_HEAD_DIM = 64
IDX_HEADS = 4
IDX_DIM = 64
TOPK_MAX = 256

B_HEADS = 8
B_NOPE = 64
B_ROPE = 32
B_V = 64
B_Q_RANK = 384
B_KV_RANK = 256

C_HEADS = 16
C_HEAD_DIM = D_MODEL // C_HEADS

D_FF = 2816
CONV_WIDTH = 3

EVEN_IN_SIZES = (A_HEADS * A_HEAD_DIM, A_HEAD_DIM, A_HEAD_DIM, IDX_HEADS * IDX_DIM, IDX_DIM, IDX_HEADS,
                 B_Q_RANK, B_KV_RANK, B_ROPE)
DEEPNORM_ALPHA = (2 * DEPTH) ** 0.25
LOG2E = 1.4426950408889634

LANES = 128
HALF = 64
TOKEN_TILE = 512
PROJ_TILE = 1024
Q_TILE = 256
STICK_Q_TILE = 2048
STICK_KEY_BLOCK = 256
MLA_Q_TILE = 2048
MLA_KEY_BLOCK = 256
SOFTPLUS_LINEAR = 30.0
FF_CHUNK = 256
VMEM_LIMIT = 56 * 2 ** 20

_P_QA = 0
_P_IQ = _P_QA + A_HEADS * A_HEAD_DIM
_P_CQ = _P_IQ + IDX_HEADS * IDX_DIM
_P_CKV = _P_CQ + B_Q_RANK
_P_GK = _P_CKV + B_KV_RANK
_P_GV = _P_GK + LANES
_P_GI = _P_GV + LANES
_P_G5 = _P_GI + LANES
_P_END = _P_G5 + LANES
_KR_LO = HALF
_KR_HI = HALF + B_ROPE
_IW_LO = _KR_HI

_NT = (((1,), (1,)), ((), ()))


def _cparams(sem):
    return pltpu.CompilerParams(dimension_semantics=sem, vmem_limit_bytes=VMEM_LIMIT)


def _const_spec(shape):
    nd = len(shape)
    return pl.BlockSpec(shape, lambda *_: (0,) * nd)


def _lane_iota(shape):
    return lax.broadcasted_iota(jnp.int32, shape, len(shape) - 1)


def _layer_norm(v, g, b):
    mu = jnp.mean(v, axis=-1, keepdims=True)
    d = v - mu
    var = jnp.mean(d * d, axis=-1, keepdims=True)
    return d * lax.rsqrt(var + LN_EPS) * g + b


def _mod_kernel(c_ref, w_ref, b_ref, o_ref):
    c = c_ref[...]
    ca = c / (1.0 + jnp.exp(-c))
    o_ref[0] = jnp.dot(ca, w_ref[0], preferred_element_type=F32,
                       precision=lax.Precision.HIGHEST) + b_ref[0]


def _modulation(c, mod_w, mod_b):
    depth, d, d6 = mod_w.shape
    b = c.shape[0]
    nj = d6 // d
    return pl.pallas_call(
        _mod_kernel,
        grid=(depth, nj),
        in_specs=[pl.BlockSpec((b, d), lambda l, j: (0, 0)),
                  pl.BlockSpec((1, d, d), lambda l, j: (l, 0, j)),
                  pl.BlockSpec((1, 1, d), lambda l, j: (l, 0, j))],
        out_specs=pl.BlockSpec((1, b, d), lambda l, j: (l, 0, j)),
        out_shape=jax.ShapeDtypeStruct((depth, b, d6), F32),
        compiler_params=_cparams(("parallel", "parallel")),
    )(c, mod_w, mod_b.reshape(depth, 1, d6))


def _rope_kernel(pos_ref, f_ref, ga_ref, gb_ref, ca_ref, sa_ref, cb_ref, sb_ref):
    ang = pos_ref[...] * f_ref[...]
    c, s = jnp.cos(ang), jnp.sin(ang)
    na, nb = A_HEAD_DIM // 2, B_ROPE // 2
    ones = jnp.ones((ang.shape[0], _KR_LO), F32)
    ca_ref[...] = jnp.concatenate([c[:, :na]] * (LANES // na), axis=1)
    sa_ref[...] = jnp.concatenate([s[:, :na]] * (LANES // na), axis=1) * ga_ref[...]
    cb, sb = c[:, na:na + nb], s[:, na:na + nb]
    cb_ref[...] = jnp.concatenate([ones, cb, cb, ones[:, :LANES - _KR_HI]], axis=1)
    sb_ref[...] = jnp.concatenate([0.0 * ones, sb, sb, 0.0 * ones[:, :LANES - _KR_HI]], axis=1) * gb_ref[...]


def _rope_tables(positions):
    n = positions.size
    lane = np.arange(LANES)
    inv_a = ROPE_THETA ** (-jnp.arange(0, A_HEAD_DIM, 2, dtype=F32) / A_HEAD_DIM)
    inv_b = ROPE_THETA ** (-jnp.arange(0, B_ROPE, 2, dtype=F32) / B_ROPE)
    freqs = jnp.concatenate([inv_a, inv_b, jnp.zeros((LANES - inv_a.size - inv_b.size,), F32)])
    ga = jnp.asarray(np.where(lane % A_HEAD_DIM < A_HEAD_DIM // 2, -1.0, 1.0), F32)
    in_rope = (lane >= _KR_LO) & (lane < _KR_HI)
    gb = jnp.asarray(np.where(in_rope, np.where(lane < _KR_LO + B_ROPE // 2, -1.0, 1.0), 0.0), F32)
    tm = 2048 if n % 2048 == 0 else TOKEN_TILE
    vec = lambda v: v.reshape(1, LANES).astype(F32)
    tab = jax.ShapeDtypeStruct((n, LANES), F32)
    return pl.pallas_call(
        _rope_kernel,
        grid=(n // tm,),
        in_specs=[pl.BlockSpec((tm, 1), lambda i: (i, 0))] + [_const_spec((1, LANES))] * 3,
        out_specs=[pl.BlockSpec((tm, LANES), lambda i: (i, 0))] * 4,
        out_shape=[tab] * 4,
        compiler_params=_cparams(("parallel",)),
    )(positions.reshape(n, 1).astype(F32), vec(freqs), vec(ga), vec(gb))


def _rope_a(x, c, s):
    lane = _lane_iota(x.shape)
    partner = jnp.where(lane % A_HEAD_DIM < A_HEAD_DIM // 2,
                        pltpu.roll(x, LANES - A_HEAD_DIM // 2, 1), pltpu.roll(x, A_HEAD_DIM // 2, 1))
    return x * c + partner * s


def _rope_b(x, c, s):
    lane = _lane_iota(x.shape)
    partner = jnp.where(lane < _KR_LO + B_ROPE // 2,
                        pltpu.roll(x, LANES - B_ROPE // 2, 1), pltpu.roll(x, B_ROPE // 2, 1))
    return x * c + partner * s


def _even_proj_kernel(x_ref, mod_ref, win_ref, ca_ref, sa_ref, cb_ref, sb_ref, ikg_ref, ikb_ref,
                      qg_ref, kvg_ref, wuq_ref, wkn_ref, wv_ref,
                      qa_ref, iq_ref, ka0_ref, ka1_ref, va_ref, ik0_ref, ik1_ref, misc_ref,
                      qb_ref, kf_ref, vb_ref):
    sh = mod_ref[0, 0:1, :]
    sc = mod_ref[0, 1:2, :]
    h = (x_ref[...] * (1.0 + sc) + sh).astype(BF16)
    ca, sa, cb, sb = ca_ref[...], sa_ref[...], cb_ref[...], sb_ref[...]
    lane = _lane_iota(ca.shape)
    low = lane < HALF
    kr = []

    def rms(v, g_ref):
        return (v * lax.rsqrt(jnp.mean(v * v, axis=-1, keepdims=True) + RMS_EPS) * g_ref[...]).astype(BF16)

    def misc_group(p):
        g5 = _rope_b(p, cb, sb)
        misc_ref[...] = g5 * (IDX_HEADS ** -0.5)
        kr.append(jnp.where((lane >= _KR_LO) & (lane < _KR_HI), g5, 0.0))

    def kv_latent(p):
        ckvn = rms(p, kvg_ref)
        kn = jnp.dot(ckvn, wkn_ref[...], preferred_element_type=F32)
        for hd in range(B_HEADS):
            sl = slice(hd * LANES, (hd + 1) * LANES)
            kf_ref[:, sl] = (kn[:, sl] + kr[0]).astype(BF16)
        vb_ref[...] = jnp.dot(ckvn, wv_ref[...], preferred_element_type=F32).astype(BF16)

    def q_latent(p):
        qb = jnp.dot(rms(p, qg_ref), wuq_ref[...], preferred_element_type=F32)
        scale_b = (B_NOPE + B_ROPE) ** -0.5 * LOG2E
        for hd in range(B_HEADS):
            sl = slice(hd * LANES, (hd + 1) * LANES)
            qb_ref[:, sl] = (_rope_b(qb[:, sl], cb, sb) * scale_b).astype(BF16)

    def roped_heads(o_ref, scale):
        def store(p):
            for g in range(p.shape[1] // LANES):
                sl = slice(g * LANES, (g + 1) * LANES)
                o_ref[:, sl] = (_rope_a(p[:, sl], ca, sa) * scale).astype(BF16)
        return store

    def dsa_key(p):
        ka = _rope_a(p, ca, sa)
        ka0_ref[...] = jnp.where(low, ka, 0.0).astype(BF16)
        ka1_ref[...] = jnp.where(low, 0.0, ka).astype(BF16)

    def dsa_value(p):
        va_ref[...] = p.astype(BF16)

    def indexer_key(p):
        mu = jnp.sum(jnp.where(low, p, 0.0), axis=-1, keepdims=True) * (1.0 / IDX_DIM)
        d = p - mu
        var = jnp.sum(jnp.where(low, d * d, 0.0), axis=-1, keepdims=True) * (1.0 / IDX_DIM)
        ik = _rope_a(d * lax.rsqrt(var + LN_EPS) * ikg_ref[...] + ikb_ref[...], ca, sa)
        ik0_ref[...] = jnp.where(low, ik, 0.0).astype(BF16)
        ik1_ref[...] = jnp.where(low, 0.0, ik).astype(BF16)

    both = lambda first, second: lambda p: (first(p[:, :LANES]), second(p[:, LANES:]))
    stages = [(_P_GI, 2 * LANES, both(indexer_key, misc_group)), (_P_CKV, B_KV_RANK, kv_latent),
              (_P_CQ, B_Q_RANK, q_latent),
              (_P_QA, A_HEADS * A_HEAD_DIM, roped_heads(qa_ref, A_HEAD_DIM ** -0.5 * LOG2E)),
              (_P_IQ, IDX_HEADS * IDX_DIM, roped_heads(iq_ref, IDX_DIM ** -0.5)),
              (_P_GK, 2 * LANES, both(dsa_key, dsa_value))]
    project = lambda off, width: jnp.dot(h, win_ref[:, off:off + width], preferred_element_type=F32)
    nxt = project(*stages[0][:2])
    for s, (_, _, epilogue) in enumerate(stages):
        cur = nxt
        if s + 1 < len(stages):
            nxt = project(*stages[s + 1][:2])
        epilogue(cur)


def _even_weights(w_in, idx_k_g, idx_k_b, q_norm_g, kv_norm_g, w_uq, w_ukv):
    d = w_in.shape[0]
    qa, ka, va, iq, ik, iw, cq, ckv, kr = jnp.split(w_in, np.cumsum(EVEN_IN_SIZES)[:-1].tolist(), axis=1)
    z = lambda n: jnp.zeros((d, n), w_in.dtype)
    win = jnp.concatenate([qa, iq, cq, ckv, ka, ka, va, va, ik, ik,
                           z(_KR_LO), kr, iw, z(LANES - _IW_LO - IDX_HEADS)], axis=1).astype(BF16)
    assert win.shape[1] == _P_END
    pad_head = B_NOPE + B_ROPE
    wuq = jnp.pad(w_uq.reshape(B_Q_RANK, B_HEADS, pad_head), ((0, 0), (0, 0), (0, LANES - pad_head)))
    wuq = wuq.reshape(B_Q_RANK, B_HEADS * LANES).astype(BF16)
    wkv = w_ukv.reshape(B_KV_RANK, B_HEADS, B_NOPE + B_V)
    wkn = jnp.pad(wkv[:, :, :B_NOPE], ((0, 0), (0, 0), (0, LANES - B_NOPE)))
    wkn = wkn.reshape(B_KV_RANK, B_HEADS * LANES).astype(BF16)
    wv = wkv[:, :, B_NOPE:].reshape(B_KV_RANK, B_HEADS * B_V).astype(BF16)
    two = lambda v: jnp.concatenate([v, v]).reshape(1, LANES).astype(F32)
    return (win, two(idx_k_g), two(idx_k_b), q_norm_g.reshape(1, -1).astype(F32),
            kv_norm_g.reshape(1, -1).astype(F32), wuq, wkn, wv)


def _even_proj(x, mod_l, tables, weights, batch, seq):
    n, d = x.shape
    tm = min(PROJ_TILE, seq)
    nt = seq // tm
    win, ikg, ikb, qg, kvg, wuq, wkn, wv = weights
    row = lambda w: pl.BlockSpec((tm, w), lambda b, t: (b * nt + t, 0))
    bf = lambda w: jax.ShapeDtypeStruct((n, w), BF16)
    out_widths = [A_HEADS * A_HEAD_DIM, IDX_HEADS * IDX_DIM, LANES, LANES, LANES, LANES, LANES]
    out_shape = [bf(w) for w in out_widths] + [jax.ShapeDtypeStruct((n, LANES), F32)] + \
                [bf(B_HEADS * LANES), bf(B_HEADS * LANES), bf(B_HEADS * B_V)]
    out_specs = [row(w) for w in out_widths] + [row(LANES)] + \
                [row(B_HEADS * LANES), row(B_HEADS * LANES), row(B_HEADS * B_V)]
    return pl.pallas_call(
        _even_proj_kernel,
        grid=(batch, nt),
        in_specs=[row(d), pl.BlockSpec((1, 6, d), lambda b, t: (b, 0, 0)), _const_spec(win.shape)]
                 + [row(LANES)] * 4
                 + [_const_spec(a.shape) for a in (ikg, ikb, qg, kvg, wuq, wkn, wv)],
        out_specs=out_specs,
        out_shape=out_shape,
        compiler_params=_cparams(("parallel", "parallel")),
    )(x, mod_l, win, *tables, ikg, ikb, qg, kvg, wuq, wkn, wv)


def _dsa_kernel(tq, topk, iq_ref, misc_ref, qa_ref, ik0_ref, ik1_ref, ka0_ref, ka1_ref, va_ref,
                o_ref, key_buf, hi_buf, lo_buf, lg_buf, m_buf, l_buf, acc_buf):
    i = pl.program_id(1)
    n_blk = i + 1
    int_min = jnp.int32(-2 ** 31)
    min16 = jnp.int16(-2 ** 15)
    one16, zero16 = jnp.int16(1), jnp.int16(0)
    kf = jnp.float32(topk)
    n_pair = A_HEADS // 2
    rows = n_pair * tq

    def rows_of(j):
        return pl.ds(pl.multiple_of(j * tq, tq), tq)

    iw_t = misc_ref[...].T
    iq = iq_ref[...]

    def score_block(j, diagonal):
        rels = [lax.dot_general(ik_ref[rows_of(j), :], iq[:, pair * LANES:(pair + 1) * LANES], _NT,
                                preferred_element_type=F32)
                for pair in range(IDX_HEADS // 2) for ik_ref in (ik0_ref, ik1_ref)]
        score = jnp.zeros((tq, tq), F32)
        for hd, rel in enumerate(rels):
            score = score + jnp.maximum(rel, 0.0) * iw_t[_IW_LO + hd:_IW_LO + hd + 1, :]
        score = jnp.where(score == 0.0, 0.0, score)
        bits = lax.bitcast_convert_type(score, jnp.int32)
        key = bits ^ (lax.shift_right_arithmetic(bits, 31) & jnp.int32(0x7FFFFFFF))
        if diagonal:
            key_chunk = lax.shift_right_logical(lax.broadcasted_iota(jnp.int32, (tq, tq), 0), 6)
            query_chunk = lax.shift_right_logical(lax.broadcasted_iota(jnp.int32, (tq, tq), 1), 6)
            key = jnp.where(key_chunk <= query_chunk, key, int_min)
        key_buf[rows_of(j), :] = key
        hi_buf[rows_of(j), :] = lax.shift_right_arithmetic(key, 16).astype(jnp.int16)

    lax.fori_loop(0, i, lambda j, c: (score_block(j, False), c)[1], 0)
    score_block(i, True)

    def count(pred, buf):
        def body(j, acc):
            hit = jnp.where(pred(buf[rows_of(j), :]), one16, zero16)
            for r in range(tq // 16):
                acc = acc + hit[r * 16:(r + 1) * 16]
            return acc
        acc = lax.fori_loop(0, n_blk, body, jnp.zeros((16, tq), jnp.int16))
        return jnp.sum(acc.astype(jnp.int32), axis=0, keepdims=True).astype(F32)

    def search(buf, base):
        def bit(b, cu):
            cand = cu | lax.shift_left(jnp.int32(1), 15 - b)
            image = (cand ^ jnp.int32(0x8000)).astype(jnp.int16)
            return jnp.where(base + count(lambda x: x >= image, buf) >= kf, cand, cu)

        return lax.fori_loop(0, 16, bit, jnp.zeros((1, tq), jnp.int32))

    hi_cu = search(hi_buf, 0.0)
    hi16 = (hi_cu ^ jnp.int32(0x8000)).astype(jnp.int16)

    def low_prep(j, _):
        low = ((key_buf[rows_of(j), :] & jnp.int32(0xFFFF)) ^ jnp.int32(0x8000)).astype(jnp.int16)
        lo_buf[rows_of(j), :] = jnp.where(hi_buf[rows_of(j), :] == hi16, low, min16)
        return 0

    lax.fori_loop(0, n_blk, low_prep, 0)
    n_above = count(lambda x: x > hi16, hi_buf)
    lo_cu = search(lo_buf, n_above)
    lo16 = (lo_cu ^ jnp.int32(0x8000)).astype(jnp.int16)
    thr = lax.shift_left(hi_cu ^ jnp.int32(0x8000), 16) | lo_cu
    need = kf - (n_above + count(lambda x: x > lo16, lo_buf))
    tri = jnp.where(lax.broadcasted_iota(jnp.int32, (tq, tq), 0) >= lax.broadcasted_iota(jnp.int32, (tq, tq), 1),
                    1.0, 0.0).astype(BF16)

    qa = qa_ref[...]
    q_stack = jnp.concatenate([qa[:, g * LANES:(g + 1) * LANES] for g in range(n_pair)], axis=0)
    m_buf[...] = jnp.full(m_buf.shape, -jnp.inf, F32)

    def logits_block(j, ties_before):
        k = key_buf[rows_of(j), :]
        tied = jnp.where(k == thr, 1.0, 0.0)
        rank = jnp.dot(tri, tied.astype(BF16), preferred_element_type=F32) + ties_before
        keep = jnp.where(k == int_min, 0.0,
                         jnp.where(k > thr, 1.0, jnp.where(rank <= need, tied, 0.0)))
        bias = jnp.where(keep.T > 0.5, 0.0, -jnp.inf)
        bias = jnp.concatenate([bias] * n_pair, axis=0)
        lgs = [lax.dot_general(q_stack, ka_ref[rows_of(j), :], _NT, preferred_element_type=F32)
               for ka_ref in (ka0_ref, ka1_ref)]
        for half, lg in enumerate(lgs):
            sl = slice(half * rows, (half + 1) * rows)
            lg = lg + bias
            lg_buf[j, sl, :] = lg
            folded = jnp.maximum(lg[:, :LANES], lg[:, LANES:]) if tq == 2 * LANES else lg
            m_buf[sl, :] = jnp.maximum(m_buf[sl, :], folded)
        return rank[tq - 1:tq, :]

    lax.fori_loop(0, n_blk, logits_block, jnp.zeros((1, tq), F32))
    m_buf[...] = jnp.broadcast_to(jnp.max(m_buf[...], axis=-1, keepdims=True), m_buf.shape)
    l_buf[...] = jnp.zeros(l_buf.shape, F32)
    acc_buf[...] = jnp.zeros(acc_buf.shape, F32)

    def pv_block(j, _):
        v = va_ref[rows_of(j), :]
        for half in range(2):
            sl = slice(half * rows, (half + 1) * rows)
            m = m_buf[sl, :]
            e = jnp.exp2(lg_buf[j, sl, :] - jnp.concatenate([m] * (tq // LANES), axis=1))
            l_buf[sl, :] = l_buf[sl, :] + (e[:, :LANES] + e[:, LANES:] if tq == 2 * LANES else e)
            acc_buf[sl, :] = acc_buf[sl, :] + jnp.dot(e.astype(BF16), v, preferred_element_type=F32)
        return 0

    lax.fori_loop(0, n_blk, pv_block, 0)
    out = acc_buf[...] / jnp.sum(l_buf[...], axis=-1, keepdims=True)
    lane = _lane_iota((tq, LANES))
    for g in range(n_pair):
        o_ref[:, g * LANES:(g + 1) * LANES] = jnp.where(
            lane < HALF, out[g * tq:(g + 1) * tq], out[rows + g * tq:rows + (g + 1) * tq]).astype(BF16)


def _dsa_attention(iq, misc, qa, ik0, ik1, ka0, ka1, va, batch, seq):
    n = qa.shape[0]
    tq = min(Q_TILE, seq)
    assert tq % LANES == 0 and tq // LANES in (1, 2) and seq <= 2 ** 15
    n_q = seq // tq
    topk = min(TOPK_MAX, seq // 4)
    rows = A_HEADS * tq
    qrow = lambda w: pl.BlockSpec((tq, w), lambda b, i: (b * n_q + i, 0))
    krow = pl.BlockSpec((seq, LANES), lambda b, i: (b, 0))
    return pl.pallas_call(
        functools.partial(_dsa_kernel, tq, topk),
        grid=(batch, n_q),
        in_specs=[qrow(iq.shape[1]), qrow(LANES), qrow(qa.shape[1])] + [krow] * 5,
        out_specs=qrow(qa.shape[1]),
        out_shape=jax.ShapeDtypeStruct((n, qa.shape[1]), BF16),
        scratch_shapes=[pltpu.VMEM((seq, tq), jnp.int32), pltpu.VMEM((seq, tq), jnp.int16),
                        pltpu.VMEM((seq, tq), jnp.int16), pltpu.VMEM((n_q, rows, tq), F32),
                        pltpu.VMEM((rows, LANES), F32), pltpu.VMEM((rows, LANES), F32),
                        pltpu.VMEM((rows, LANES), F32)],
        compiler_params=_cparams(("parallel", "arbitrary")),
    )(iq, misc, qa, ik0, ik1, ka0, ka1, va)


def _mla_kernel(tq, kb, q_ref, k_ref, v_ref, o_ref, lg_buf, m_buf, l_buf, acc_buf):
    i = pl.program_id(2)
    per_tile = tq // kb
    q = [q_ref[:, hh * LANES:(hh + 1) * LANES] for hh in range(2)]
    fold = lambda x, op: op(x[:, :LANES], x[:, LANES:]) if kb == 2 * LANES else x

    def rows_of(j):
        return pl.ds(pl.multiple_of(j * kb, kb), kb)

    def first_row(d):
        return 0 if d is None else d * kb

    def logits(js, diags):
        streams = [(u, hh) for u in range(len(js)) for hh in range(2)]
        lg = {}

        def stage_dot(s):
            u, hh = s
            lg[s] = lax.dot_general(q[hh][first_row(diags[u]):], k_ref[rows_of(js[u]), hh * LANES:(hh + 1) * LANES],
                                    _NT, preferred_element_type=F32)

        def stage_store(s):
            u, hh = s
            r0 = first_row(diags[u])
            x = lg[s]
            if diags[u] is not None:
                key_idx = lax.broadcasted_iota(jnp.int32, x.shape, 1) + diags[u] * kb
                query_idx = lax.broadcasted_iota(jnp.int32, x.shape, 0) + r0
                x = jnp.where(lax.shift_right_logical(key_idx, 6) <= lax.shift_right_logical(query_idx, 6),
                              x, -jnp.inf)
            lg_buf[hh, js[u], r0:, :] = x
            m_buf[hh, r0:, :] = jnp.maximum(m_buf[hh, r0:, :], fold(x, jnp.maximum))

        for t in range(len(streams) + 1):
            if t < len(streams):
                stage_dot(streams[t])
            if t >= 1:
                stage_store(streams[t - 1])

    def weighted(js, diags):
        streams = [(u, hh) for u in range(len(js)) for hh in range(2)]
        e = {}

        def stage_exp(s):
            u, hh = s
            r0 = first_row(diags[u])
            x = jnp.exp2(lg_buf[hh, js[u], r0:, :] - jnp.concatenate([m_buf[hh, r0:, :]] * (kb // LANES), axis=1))
            l_buf[hh, r0:, :] = l_buf[hh, r0:, :] + fold(x, jnp.add)
            e[s] = x.astype(BF16)

        def stage_pv(s):
            u, hh = s
            r0 = first_row(diags[u])
            acc_buf[hh, r0:, :] = acc_buf[hh, r0:, :] + jnp.dot(e[s], v_ref[rows_of(js[u]), :],
                                                              preferred_element_type=F32)

        for t in range(len(streams) + 1):
            if t < len(streams):
                stage_exp(streams[t])
            if t >= 1:
                stage_pv(streams[t - 1])

    diag = list(range(per_tile))
    n_full = i * per_tile
    full = lambda s: [s * per_tile + u for u in range(per_tile)]
    none = [None] * per_tile

    m_buf[...] = jnp.full(m_buf.shape, -jnp.inf, F32)
    logits([n_full + d for d in diag], diag)
    lax.fori_loop(0, i, lambda s, c: (logits(full(s), none), c)[1], 0)
    for hh in range(2):
        m_buf[hh] = jnp.broadcast_to(jnp.max(m_buf[hh], axis=-1, keepdims=True), (tq, LANES))
    l_buf[...] = jnp.zeros(l_buf.shape, F32)
    acc_buf[...] = jnp.zeros(acc_buf.shape, F32)
    weighted([n_full + d for d in diag], diag)
    lax.fori_loop(0, i, lambda s, c: (weighted(full(s), none), c)[1], 0)
    lane = _lane_iota((tq, LANES))
    outs = [acc_buf[hh] / jnp.sum(l_buf[hh], axis=-1, keepdims=True) for hh in range(2)]
    o_ref[...] = jnp.where(lane < HALF, outs[0], outs[1]).astype(BF16)


def _mla_attention(qb, kf, vb, batch, seq):
    n = qb.shape[0]
    tq = min(MLA_Q_TILE, seq)
    kb = min(MLA_KEY_BLOCK, tq)
    assert kb % LANES == 0 and kb // LANES in (1, 2) and kb % CHUNK == 0
    n_q = seq // tq
    n_pair = B_HEADS // 2
    return pl.pallas_call(
        functools.partial(_mla_kernel, tq, kb),
        grid=(batch, n_pair, n_q),
        in_specs=[pl.BlockSpec((tq, 2 * LANES), lambda b, p, i: (b * n_q + i, p)),
                  pl.BlockSpec((seq, 2 * LANES), lambda b, p, i: (b, p)),
                  pl.BlockSpec((seq, LANES), lambda b, p, i: (b, p))],
        out_specs=pl.BlockSpec((tq, LANES), lambda b, p, i: (b * n_q + i, p)),
        out_shape=jax.ShapeDtypeStruct((n, B_HEADS * B_V), BF16),
        scratch_shapes=[pltpu.VMEM((2, seq // kb, tq, kb), F32), pltpu.VMEM((2, tq, LANES), F32),
                        pltpu.VMEM((2, tq, LANES), F32), pltpu.VMEM((2, tq, LANES), F32)],
        compiler_params=_cparams(("parallel", "parallel", "arbitrary")),
    )(qb, kf, vb)


def _odd_proj_kernel(x_ref, mod_ref, w_ref, o_ref):
    sh = mod_ref[0, 0:1, :]
    sc = mod_ref[0, 1:2, :]
    h = (x_ref[...] * (1.0 + sc) + sh).astype(BF16)
    p = jnp.dot(h, w_ref[...], preferred_element_type=F32)
    dq = C_HEADS * C_HEAD_DIM
    o_ref[:, 0:dq] = (p[:, 0:dq] * (C_HEAD_DIM ** -0.5)).astype(BF16)
    o_ref[:, dq:] = p[:, dq:].astype(BF16)


def _odd_proj(x, mod_l, w_qkv, batch, seq):
    n, d = x.shape
    tm = min(PROJ_TILE, seq)
    nt = seq // tm
    w = w_qkv.astype(BF16)
    return pl.pallas_call(
        _odd_proj_kernel,
        grid=(batch, nt),
        in_specs=[pl.BlockSpec((tm, d), lambda b, t: (b * nt + t, 0)),
                  pl.BlockSpec((1, 6, d), lambda b, t: (b, 0, 0)), _const_spec(w.shape)],
        out_specs=pl.BlockSpec((tm, w.shape[1]), lambda b, t: (b * nt + t, 0)),
        out_shape=jax.ShapeDtypeStruct((n, w.shape[1]), BF16),
        compiler_params=_cparams(("parallel", "parallel")),
    )(x, mod_l, w)


def _stick_kernel(tq, kb, q_ref, k_ref, v_ref, o_ref, acc_ref):
    i = pl.program_id(2)
    per_tile = tq // kb
    q2 = q_ref[...]
    lane = _lane_iota((tq, LANES))
    tri = jnp.where(lax.broadcasted_iota(jnp.int32, (kb, kb), 0) >= lax.broadcasted_iota(jnp.int32, (kb, kb), 1),
                    1.0, 0.0).astype(BF16)
    row = lax.broadcasted_iota(jnp.int32, (tq, kb), 0)
    col = lax.broadcasted_iota(jnp.int32, (tq, kb), 1)
    q_heads = (jnp.where(lane < HALF, q2, jnp.zeros_like(q2)), jnp.where(lane < HALF, jnp.zeros_like(q2), q2))
    acc_ref[...] = jnp.zeros(acc_ref.shape, F32)

    def blocks(js, carries, diags):
        streams = [(u, hh) for u in range(len(js)) for hh in range(2)]
        kv = []
        for j in js:
            start = pl.multiple_of(j * kb, kb)
            kv.append((k_ref[pl.ds(start, kb), :], v_ref[pl.ds(start, kb), :]))
        before = [None if d is None else (col + d * kb < row) for d in diags]
        z, inc = {}, {}
        carry = list(carries)

        r0 = [0 if d is None else d * kb for d in diags]

        def stage_z(s):
            z[s] = lax.dot_general(q_heads[s[1]][r0[s[0]]:], kv[s[0]][0], _NT, preferred_element_type=F32)

        def stage_sum(s):
            sp = jnp.where(z[s] > SOFTPLUS_LINEAR, z[s], jnp.log(1.0 + jnp.exp(z[s])))
            if before[s[0]] is not None:
                sp = jnp.where(before[s[0]][r0[s[0]]:], sp, 0.0)
            inc[s] = jnp.dot(sp.astype(BF16), tri, preferred_element_type=F32)

        def stage_out(s):
            u, hh = s
            a = jnp.exp(z[s] - inc[s] - carry[hh][r0[u]:])
            if before[u] is not None:
                a = jnp.where(before[u][r0[u]:], a, 0.0)
            acc_ref[hh, r0[u]:, :] = acc_ref[hh, r0[u]:, :] + jnp.dot(a.astype(BF16), kv[u][1],
                                                                      preferred_element_type=F32)
            tot = carry[hh][r0[u]:] + inc[s][:, 0:1]
            carry[hh] = tot if r0[u] == 0 else jnp.concatenate([carry[hh][:r0[u]], tot], axis=0)

        for t in range(len(streams) + 2):
            if t < len(streams):
                stage_z(streams[t])
            if 0 <= t - 1 < len(streams):
                stage_sum(streams[t - 1])
            if 0 <= t - 2 < len(streams):
                stage_out(streams[t - 2])
        return tuple(carry)

    carries = (jnp.zeros((tq, 1), F32), jnp.zeros((tq, 1), F32))
    diag = list(reversed(range(per_tile)))
    carries = blocks([i * per_tile + d for d in diag], carries, diag)
    n_full = i * per_tile
    lax.fori_loop(0, i, lambda s, cs: blocks([n_full - 1 - s * per_tile - u for u in range(per_tile)], cs,
                                             [None] * per_tile), carries)
    o_ref[...] = jnp.where(lane < HALF, acc_ref[0], acc_ref[1]).astype(BF16)


def _stick_attention(qkv, batch, seq):
    n = qkv.shape[0]
    tq = min(STICK_Q_TILE, seq)
    kb = min(STICK_KEY_BLOCK, tq)
    n_q = seq // tq
    n_pair = C_HEADS // 2
    return pl.pallas_call(
        functools.partial(_stick_kernel, tq, kb),
        grid=(batch, n_pair, n_q),
        in_specs=[pl.BlockSpec((tq, LANES), lambda b, p, i: (b * n_q + i, p)),
                  pl.BlockSpec((seq, LANES), lambda b, p, i: (b, n_pair + p)),
                  pl.BlockSpec((seq, LANES), lambda b, p, i: (b, 2 * n_pair + p))],
        out_specs=pl.BlockSpec((tq, LANES), lambda b, p, i: (b * n_q + i, p)),
        out_shape=jax.ShapeDtypeStruct((n, C_HEADS * C_HEAD_DIM), BF16),
        scratch_shapes=[pltpu.VMEM((2, tq, LANES), F32)],
        compiler_params=_cparams(("parallel", "parallel", "arbitrary")),
    )(qkv, qkv, qkv)


_HALO = 8


def _mix_ffn_kernel(tm, n_in, *refs):
    x_ref, mod_ref = refs[0], refs[1]
    ins = refs[2:2 + n_in]
    wos = refs[2 + n_in:2 + 2 * n_in]
    mg_ref, mb_ref, wup_ref, cw_ref, cb_ref, wdn_ref, g_ref, b_ref, o_ref, a_buf, tail_buf = refs[2 + 2 * n_in:]
    t = pl.program_id(1)
    sh = mod_ref[0, 3:4, :]
    sc = mod_ref[0, 4:5, :]
    gate = mod_ref[0, 5:6, :]

    @pl.when(t == 0)
    def _():
        tail_buf[...] = jnp.zeros_like(tail_buf)

    def up(hh, ci):
        cs = slice(ci * FF_CHUNK, (ci + 1) * FF_CHUNK)
        gs = slice(D_FF + ci * FF_CHUNK, D_FF + (ci + 1) * FF_CHUNK)
        return (jnp.dot(hh, wup_ref[:, cs], preferred_element_type=F32),
                jnp.dot(hh, wup_ref[:, gs], preferred_element_type=F32))

    halves = [slice(0, tm // 2), slice(tm // 2, tm)]
    mixes = []
    for rs in halves:
        mix = jnp.dot(ins[0][rs, :], wos[0][...], preferred_element_type=F32)
        for a_ref, w_ref in zip(ins[1:], wos[1:]):
            mix = mix + jnp.dot(a_ref[rs, :], w_ref[...], preferred_element_type=F32)
        mixes.append(mix)
    xs, hs, first = [], [], []
    for rs, mix in zip(halves, mixes):
        xh = _layer_norm(DEEPNORM_ALPHA * x_ref[rs, :] + (1.0 + mod_ref[0, 2:3, :]) * mix, mg_ref[...], mb_ref[...])
        hh = (xh * (1.0 + sc) + sh).astype(BF16)
        xs.append(xh)
        hs.append(hh)
        first.append(up(hh, 0))
    x = jnp.concatenate(xs, axis=0)
    h = jnp.concatenate(hs, axis=0)
    nxt = tuple(jnp.concatenate([f[k] for f in first], axis=0) for k in range(2))

    n_chunk = D_FF // FF_CHUNK
    y = jnp.zeros((tm, x.shape[1]), F32)
    for ci in range(n_chunk):
        cs = slice(ci * FF_CHUNK, (ci + 1) * FF_CHUNK)
        a, gt = nxt
        if ci + 1 < n_chunk:
            nxt = up(h, ci + 1)
        a_buf[0:_HALO, :] = tail_buf[:, cs]
        a_buf[_HALO:, :] = a
        tail_buf[:, cs] = a[tm - _HALO:, :]
        conv = (a_buf[_HALO - 2:_HALO - 2 + tm, :] * cw_ref[0:1, cs]
                + a_buf[_HALO - 1:_HALO - 1 + tm, :] * cw_ref[1:2, cs]
                + a * cw_ref[2:3, cs] + cb_ref[:, cs])
        u = conv / (1.0 + jnp.exp(-conv)) * gt
        y = y + jnp.dot(u.astype(BF16), wdn_ref[cs, :], preferred_element_type=F32)
    o_ref[...] = _layer_norm(DEEPNORM_ALPHA * x + (1.0 + gate) * y, g_ref[...], b_ref[...])


def _mix_ffn(x, mod_l, acts, w_out, mix_g, mix_b, w_up, conv_w, conv_b, w_down, ffn_g, ffn_b, batch, seq):
    n, d = x.shape
    tm = min(TOKEN_TILE, seq)
    nt = seq // tm
    row = lambda w: pl.BlockSpec((tm, w), lambda b, t: (b * nt + t, 0))
    single = lambda shape: pl.BlockSpec(shape, lambda b, t: (0,) * len(shape), pipeline_mode=pl.Buffered(1))
    wos, off = [], 0
    for a in acts:
        wos.append(w_out[off:off + a.shape[1]].astype(BF16))
        off += a.shape[1]
    vec = lambda v: v.reshape(1, -1)
    return pl.pallas_call(
        functools.partial(_mix_ffn_kernel, tm, len(acts)),
        grid=(batch, nt),
        in_specs=[row(d), pl.BlockSpec((1, 6, d), lambda b, t: (b, 0, 0))]
                 + [row(a.shape[1]) for a in acts] + [single(w.shape) for w in wos]
                 + [_const_spec((1, d))] * 2
                 + [single((d, 2 * D_FF)), _const_spec((CONV_WIDTH, D_FF)), _const_spec((1, D_FF)), single((D_FF, d)),
                    _const_spec((1, d)), _const_spec((1, d))],
        out_specs=row(d),
        out_shape=jax.ShapeDtypeStruct((n, d), F32),
        scratch_shapes=[pltpu.VMEM((_HALO + tm, FF_CHUNK), F32), pltpu.VMEM((_HALO, D_FF), F32)],
        compiler_params=_cparams(("parallel", "arbitrary")),
    )(x, mod_l, *acts, *wos, vec(mix_g), vec(mix_b), w_up.astype(BF16), conv_w, vec(conv_b), w_down.astype(BF16),
      vec(ffn_g), vec(ffn_b))


def kernel(x, c, positions, mod_w, mod_b, ln_mix_g, ln_mix_b, ln_ffn_g, ln_ffn_b, ev_w_in, ev_idx_k_g, ev_idx_k_b, ev_q_norm_g, ev_kv_norm_g, ev_w_uq, ev_w_ukv, ev_w_out, od_w_qkv, od_w_out, ffn_w_up, ffn_conv_w, ffn_conv_b, ffn_w_down):
    batch, seq, d = x.shape
    assert d == D_MODEL and seq % min(Q_TILE, seq) == 0 and seq % CHUNK == 0
    n = batch * seq
    depth = mod_w.shape[0]
    mod = _modulation(c, mod_w, mod_b).reshape(depth, batch, 6, d)
    tables = _rope_tables(positions)
    xs = x.reshape(n, d)
    for l in range(depth):
        i = l // 2
        if l % 2 == 0:
            weights = _even_weights(ev_w_in[i], ev_idx_k_g[i], ev_idx_k_b[i], ev_q_norm_g[i],
                                    ev_kv_norm_g[i], ev_w_uq[i], ev_w_ukv[i])
            qa, iq, ka0, ka1, va, ik0, ik1, misc, qb, kf, vb = _even_proj(xs, mod[l], tables, weights, batch, seq)
            out_a = _dsa_attention(iq, misc, qa, ik0, ik1, ka0, ka1, va, batch, seq)
            out_b = _mla_attention(qb, kf, vb, batch, seq)
            acts, w_out = [out_a, out_b], ev_w_out[i]
        else:
            qkv = _odd_proj(xs, mod[l], od_w_qkv[i], batch, seq)
            acts, w_out = [_stick_attention(qkv, batch, seq)], od_w_out[i]
        xs = _mix_ffn(xs, mod[l], acts, w_out, ln_mix_g[l], ln_mix_b[l], ffn_w_up[l], ffn_conv_w[l], ffn_conv_b[l],
                      ffn_w_down[l], ln_ffn_g[l], ln_ffn_b[l], batch, seq)
    return xs.reshape(batch, seq, d)
```

```python
import functools

import numpy as np
import jax
import jax.numpy as jnp
from jax import lax
from jax.experimental import pallas as pl
from jax.experimental.pallas import tpu as pltpu

F32 = jnp.float32
BF16 = jnp.bfloat16

D_MODEL = 1024
DEPTH = 4
CHUNK = 64
ROPE_THETA = 10000.0
LN_EPS = 1e-5
RMS_EPS = 1e-6

A_HEADS = 8
A_HEAD_DIM = 64
IDX_HEADS = 4
IDX_DIM = 64
TOPK_MAX = 256

B_HEADS = 8
B_NOPE = 64
B_ROPE = 32
B_V = 64
B_Q_RANK = 384
B_KV_RANK = 256

C_HEADS = 16
C_HEAD_DIM = D_MODEL // C_HEADS

D_FF = 2816
CONV_WIDTH = 3

EVEN_IN_SIZES = (A_HEADS * A_HEAD_DIM, A_HEAD_DIM, A_HEAD_DIM, IDX_HEADS * IDX_DIM, IDX_DIM, IDX_HEADS,
                 B_Q_RANK, B_KV_RANK, B_ROPE)
DEEPNORM_ALPHA = (2 * DEPTH) ** 0.25
LOG2E = 1.4426950408889634

LANES = 128
HALF = 64
TOKEN_TILE = 512
PROJ_TILE = 1024
Q_TILE = 256
STICK_Q_TILE = 2048
STICK_KEY_BLOCK = 256
MLA_Q_TILE = 2048
MLA_KEY_BLOCK = 256
SOFTPLUS_LINEAR = 30.0
FF_CHUNK = 256
VMEM_LIMIT = 56 * 2 ** 20

_P_QA = 0
_P_IQ = _P_QA + A_HEADS * A_HEAD_DIM
_P_CQ = _P_IQ + IDX_HEADS * IDX_DIM
_P_CKV = _P_CQ + B_Q_RANK
_P_GK = _P_CKV + B_KV_RANK
_P_GV = _P_GK + LANES
_P_GI = _P_GV + LANES
_P_G5 = _P_GI + LANES
_P_END = _P_G5 + LANES
_KR_LO = HALF
_KR_HI = HALF + B_ROPE
_IW_LO = _KR_HI

_NT = (((1,), (1,)), ((), ()))


def _cparams(sem):
    return pltpu.CompilerParams(dimension_semantics=sem, vmem_limit_bytes=VMEM_LIMIT)


def _const_spec(shape):
    nd = len(shape)
    return pl.BlockSpec(shape, lambda *_: (0,) * nd)


def _lane_iota(shape):
    return lax.broadcasted_iota(jnp.int32, shape, len(shape) - 1)


def _layer_norm(v, g, b):
    mu = jnp.mean(v, axis=-1, keepdims=True)
    d = v - mu
    var = jnp.mean(d * d, axis=-1, keepdims=True)
    return d * lax.rsqrt(var + LN_EPS) * g + b


def _mod_kernel(c_ref, w_ref, b_ref, o_ref):
    c = c_ref[...]
    ca = c / (1.0 + jnp.exp(-c))
    o_ref[0] = jnp.dot(ca, w_ref[0], preferred_element_type=F32,
                       precision=lax.Precision.HIGHEST) + b_ref[0]


def _modulation(c, mod_w, mod_b):
    depth, d, d6 = mod_w.shape
    b = c.shape[0]
    nj = d6 // d
    return pl.pallas_call(
        _mod_kernel,
        grid=(depth, nj),
        in_specs=[pl.BlockSpec((b, d), lambda l, j: (0, 0)),
                  pl.BlockSpec((1, d, d), lambda l, j: (l, 0, j)),
                  pl.BlockSpec((1, 1, d), lambda l, j: (l, 0, j))],
        out_specs=pl.BlockSpec((1, b, d), lambda l, j: (l, 0, j)),
        out_shape=jax.ShapeDtypeStruct((depth, b, d6), F32),
        compiler_params=_cparams(("parallel", "parallel")),
    )(c, mod_w, mod_b.reshape(depth, 1, d6))


def _rope_kernel(pos_ref, f_ref, ga_ref, gb_ref, ca_ref, sa_ref, cb_ref, sb_ref):
    ang = pos_ref[...] * f_ref[...]
    c, s = jnp.cos(ang), jnp.sin(ang)
    na, nb = A_HEAD_DIM // 2, B_ROPE // 2
    ones = jnp.ones((ang.shape[0], _KR_LO), F32)
    ca_ref[...] = jnp.concatenate([c[:, :na]] * (LANES // na), axis=1)
    sa_ref[...] = jnp.concatenate([s[:, :na]] * (LANES // na), axis=1) * ga_ref[...]
    cb, sb = c[:, na:na + nb], s[:, na:na + nb]
    cb_ref[...] = jnp.concatenate([ones, cb, cb, ones[:, :LANES - _KR_HI]], axis=1)
    sb_ref[...] = jnp.concatenate([0.0 * ones, sb, sb, 0.0 * ones[:, :LANES - _KR_HI]], axis=1) * gb_ref[...]


def _rope_tables(positions):
    n = positions.size
    lane = np.arange(LANES)
    inv_a = ROPE_THETA ** (-jnp.arange(0, A_HEAD_DIM, 2, dtype=F32) / A_HEAD_DIM)
    inv_b = ROPE_THETA ** (-jnp.arange(0, B_ROPE, 2, dtype=F32) / B_ROPE)
    freqs = jnp.concatenate([inv_a, inv_b, jnp.zeros((LANES - inv_a.size - inv_b.size,), F32)])
    ga = jnp.asarray(np.where(lane % A_HEAD_DIM < A_HEAD_DIM // 2, -1.0, 1.0), F32)
    in_rope = (lane >= _KR_LO) & (lane < _KR_HI)
    gb = jnp.asarray(np.where(in_rope, np.where(lane < _KR_LO + B_ROPE // 2, -1.0, 1.0), 0.0), F32)
    tm = 2048 if n % 2048 == 0 else TOKEN_TILE
    vec = lambda v: v.reshape(1, LANES).astype(F32)
    tab = jax.ShapeDtypeStruct((n, LANES), F32)
    return pl.pallas_call(
        _rope_kernel,
        grid=(n // tm,),
        in_specs=[pl.BlockSpec((tm, 1), lambda i: (i, 0))] + [_const_spec((1, LANES))] * 3,
        out_specs=[pl.BlockSpec((tm, LANES), lambda i: (i, 0))] * 4,
        out_shape=[tab] * 4,
        compiler_params=_cparams(("parallel",)),
    )(positions.reshape(n, 1).astype(F32), vec(freqs), vec(ga), vec(gb))


def _rope_a(x, c, s):
    lane = _lane_iota(x.shape)
    partner = jnp.where(lane % A_HEAD_DIM < A_HEAD_DIM // 2,
                        pltpu.roll(x, LANES - A_HEAD_DIM // 2, 1), pltpu.roll(x, A_HEAD_DIM // 2, 1))
    return x * c + partner * s


def _rope_b(x, c, s):
    lane = _lane_iota(x.shape)
    partner = jnp.where(lane < _KR_LO + B_ROPE // 2,
                        pltpu.roll(x, LANES - B_ROPE // 2, 1), pltpu.roll(x, B_ROPE // 2, 1))
    return x * c + partner * s


def _even_proj_kernel(x_ref, mod_ref, win_ref, ca_ref, sa_ref, cb_ref, sb_ref, ikg_ref, ikb_ref,
                      qg_ref, kvg_ref, wuq_ref, wkn_ref, wv_ref,
                      qa_ref, iq_ref, ka0_ref, ka1_ref, va_ref, ik0_ref, ik1_ref, misc_ref,
                      qb_ref, kf_ref, vb_ref):
    sh = mod_ref[0, 0:1, :]
    sc = mod_ref[0, 1:2, :]
    h = (x_ref[...] * (1.0 + sc) + sh).astype(BF16)
    ca, sa, cb, sb = ca_ref[...], sa_ref[...], cb_ref[...], sb_ref[...]
    lane = _lane_iota(ca.shape)
    low = lane < HALF
    kr = []

    def rms(v, g_ref):
        return (v * lax.rsqrt(jnp.mean(v * v, axis=-1, keepdims=True) + RMS_EPS) * g_ref[...]).astype(BF16)

    def misc_group(p):
        g5 = _rope_b(p, cb, sb)
        misc_ref[...] = g5 * (IDX_HEADS ** -0.5)
        kr.append(jnp.where((lane >= _KR_LO) & (lane < _KR_HI), g5, 0.0))

    def kv_latent(p):
        ckvn = rms(p, kvg_ref)
        kn = jnp.dot(ckvn, wkn_ref[...], preferred_element_type=F32)
        for hd in range(B_HEADS):
            sl = slice(hd * LANES, (hd + 1) * LANES)
            kf_ref[:, sl] = (kn[:, sl] + kr[0]).astype(BF16)
        vb_ref[...] = jnp.dot(ckvn, wv_ref[...], preferred_element_type=F32).astype(BF16)

    def q_latent(p):
        qb = jnp.dot(rms(p, qg_ref), wuq_ref[...], preferred_element_type=F32)
        scale_b = (B_NOPE + B_ROPE) ** -0.5 * LOG2E
        for hd in range(B_HEADS):
            sl = slice(hd * LANES, (hd + 1) * LANES)
            qb_ref[:, sl] = (_rope_b(qb[:, sl], cb, sb) * scale_b).astype(BF16)

    def roped_heads(o_ref, scale):
        def store(p):
            for g in range(p.shape[1] // LANES):
                sl = slice(g * LANES, (g + 1) * LANES)
                o_ref[:, sl] = (_rope_a(p[:, sl], ca, sa) * scale).astype(BF16)
        return store

    def dsa_key(p):
        ka = _rope_a(p, ca, sa)
        ka0_ref[...] = jnp.where(low, ka, 0.0).astype(BF16)
        ka1_ref[...] = jnp.where(low, 0.0, ka).astype(BF16)

    def dsa_value(p):
        va_ref[...] = p.astype(BF16)

    def indexer_key(p):
        mu = jnp.sum(jnp.where(low, p, 0.0), axis=-1, keepdims=True) * (1.0 / IDX_DIM)
        d = p - mu
        var = jnp.sum(jnp.where(low, d * d, 0.0), axis=-1, keepdims=True) * (1.0 / IDX_DIM)
        ik = _rope_a(d * lax.rsqrt(var + LN_EPS) * ikg_ref[...] + ikb_ref[...], ca, sa)
        ik0_ref[...] = jnp.where(low, ik, 0.0).astype(BF16)
        ik1_ref[...] = jnp.where(low, 0.0, ik).astype(BF16)

    both = lambda first, second: lambda p: (first(p[:, :LANES]), second(p[:, LANES:]))
    stages = [(_P_GI, 2 * LANES, both(indexer_key, misc_group)), (_P_CKV, B_KV_RANK, kv_latent),
              (_P_CQ, B_Q_RANK, q_latent),
              (_P_QA, A_HEADS * A_HEAD_DIM, roped_heads(qa_ref, A_HEAD_DIM ** -0.5 * LOG2E)),
              (_P_IQ, IDX_HEADS * IDX_DIM, roped_heads(iq_ref, IDX_DIM ** -0.5)),
              (_P_GK, 2 * LANES, both(dsa_key, dsa_value))]
    project = lambda off, width: jnp.dot(h, win_ref[:, off:off + width], preferred_element_type=F32)
    nxt = project(*stages[0][:2])
    for s, (_, _, epilogue) in enumerate(stages):
        cur = nxt
        if s + 1 < len(stages):
            nxt = project(*stages[s + 1][:2])
        epilogue(cur)


def _even_weights(w_in, idx_k_g, idx_k_b, q_norm_g, kv_norm_g, w_uq, w_ukv):
    d = w_in.shape[0]
    qa, ka, va, iq, ik, iw, cq, ckv, kr = jnp.split(w_in, np.cumsum(EVEN_IN_SIZES)[:-1].tolist(), axis=1)
    z = lambda n: jnp.zeros((d, n), w_in.dtype)
    win = jnp.concatenate([qa, iq, cq, ckv, ka, ka, va, va, ik, ik,
                           z(_KR_LO), kr, iw, z(LANES - _IW_LO - IDX_HEADS)], axis=1).astype(BF16)
    assert win.shape[1] == _P_END
    pad_head = B_NOPE + B_ROPE
    wuq = jnp.pad(w_uq.reshape(B_Q_RANK, B_HEADS, pad_head), ((0, 0), (0, 0), (0, LANES - pad_head)))
    wuq = wuq.reshape(B_Q_RANK, B_HEADS * LANES).astype(BF16)
    wkv = w_ukv.reshape(B_KV_RANK, B_HEADS, B_NOPE + B_V)
    wkn = jnp.pad(wkv[:, :, :B_NOPE], ((0, 0), (0, 0), (0, LANES - B_NOPE)))
    wkn = wkn.reshape(B_KV_RANK, B_HEADS * LANES).astype(BF16)
    wv = wkv[:, :, B_NOPE:].reshape(B_KV_RANK, B_HEADS * B_V).astype(BF16)
    two = lambda v: jnp.concatenate([v, v]).reshape(1, LANES).astype(F32)
    return (win, two(idx_k_g), two(idx_k_b), q_norm_g.reshape(1, -1).astype(F32),
            kv_norm_g.reshape(1, -1).astype(F32), wuq, wkn, wv)


def _even_proj(x, mod_l, tables, weights, batch, seq):
    n, d = x.shape
    tm = min(PROJ_TILE, seq)
    nt = seq // tm
    win, ikg, ikb, qg, kvg, wuq, wkn, wv = weights
    row = lambda w: pl.BlockSpec((tm, w), lambda b, t: (b * nt + t, 0))
    bf = lambda w: jax.ShapeDtypeStruct((n, w), BF16)
    out_widths = [A_HEADS * A_HEAD_DIM, IDX_HEADS * IDX_DIM, LANES, LANES, LANES, LANES, LANES]
    out_shape = [bf(w) for w in out_widths] + [jax.ShapeDtypeStruct((n, LANES), F32)] + \
                [bf(B_HEADS * LANES), bf(B_HEADS * LANES), bf(B_HEADS * B_V)]
    out_specs = [row(w) for w in out_widths] + [row(LANES)] + \
                [row(B_HEADS * LANES), row(B_HEADS * LANES), row(B_HEADS * B_V)]
    return pl.pallas_call(
        _even_proj_kernel,
        grid=(batch, nt),
        in_specs=[row(d), pl.BlockSpec((1, 6, d), lambda b, t: (b, 0, 0)), _const_spec(win.shape)]
                 + [row(LANES)] * 4
                 + [_const_spec(a.shape) for a in (ikg, ikb, qg, kvg, wuq, wkn, wv)],
        out_specs=out_specs,
        out_shape=out_shape,
        compiler_params=_cparams(("parallel", "parallel")),
    )(x, mod_l, win, *tables, ikg, ikb, qg, kvg, wuq, wkn, wv)


def _dsa_kernel(tq, topk, iq_ref, misc_ref, qa_ref, ik0_ref, ik1_ref, ka0_ref, ka1_ref, va_ref,
                o_ref, key_buf, hi_buf, lo_buf, lg_buf, m_buf, l_buf, acc_buf):
    i = pl.program_id(1)
    n_blk = i + 1
    int_min = jnp.int32(-2 ** 31)
    min16 = jnp.int16(-2 ** 15)
    one16, zero16 = jnp.int16(1), jnp.int16(0)
    kf = jnp.float32(topk)
    n_pair = A_HEADS // 2
    rows = n_pair * tq

    def rows_of(j):
        return pl.ds(pl.multiple_of(j * tq, tq), tq)

    iw_t = misc_ref[...].T
    iq = iq_ref[...]

    def score_block(j, diagonal):
        rels = [lax.dot_general(ik_ref[rows_of(j), :], iq[:, pair * LANES:(pair + 1) * LANES], _NT,
                                preferred_element_type=F32)
                for pair in range(IDX_HEADS // 2) for ik_ref in (ik0_ref, ik1_ref)]
        score = jnp.zeros((tq, tq), F32)
        for hd, rel in enumerate(rels):
            score = score + jnp.maximum(rel, 0.0) * iw_t[_IW_LO + hd:_IW_LO + hd + 1, :]
        score = jnp.where(score == 0.0, 0.0, score)
        bits = lax.bitcast_convert_type(score, jnp.int32)
        key = bits ^ (lax.shift_right_arithmetic(bits, 31) & jnp.int32(0x7FFFFFFF))
        if diagonal:
            key_chunk = lax.shift_right_logical(lax.broadcasted_iota(jnp.int32, (tq, tq), 0), 6)
            query_chunk = lax.shift_right_logical(lax.broadcasted_iota(jnp.int32, (tq, tq), 1), 6)
            key = jnp.where(key_chunk <= query_chunk, key, int_min)
        key_buf[rows_of(j), :] = key
        hi_buf[rows_of(j), :] = lax.shift_right_arithmetic(key, 16).astype(jnp.int16)

    lax.fori_loop(0, i, lambda j, c: (score_block(j, False), c)[1], 0)
    score_block(i, True)

    def count(pred, buf):
        def body(j, acc):
            hit = jnp.where(pred(buf[rows_of(j), :]), one16, zero16)
            for r in range(tq // 16):
                acc = acc + hit[r * 16:(r + 1) * 16]
            return acc
        acc = lax.fori_loop(0, n_blk, body, jnp.zeros((16, tq), jnp.int16))
        return jnp.sum(acc.astype(jnp.int32), axis=0, keepdims=True).astype(F32)

    def search(buf, base):
        def bit(b, cu):
            cand = cu | lax.shift_left(jnp.int32(1), 15 - b)
            image = (cand ^ jnp.int32(0x8000)).astype(jnp.int16)
            return jnp.where(base + count(lambda x: x >= image, buf) >= kf, cand, cu)

        return lax.fori_loop(0, 16, bit, jnp.zeros((1, tq), jnp.int32))

    hi_cu = search(hi_buf, 0.0)
    hi16 = (hi_cu ^ jnp.int32(0x8000)).astype(jnp.int16)

    def low_prep(j, _):
        low = ((key_buf[rows_of(j), :] & jnp.int32(0xFFFF)) ^ jnp.int32(0x8000)).astype(jnp.int16)
        lo_buf[rows_of(j), :] = jnp.where(hi_buf[rows_of(j), :] == hi16, low, min16)
        return 0

    lax.fori_loop(0, n_blk, low_prep, 0)
    n_above = count(lambda x: x > hi16, hi_buf)
    lo_cu = search(lo_buf, n_above)
    lo16 = (lo_cu ^ jnp.int32(0x8000)).astype(jnp.int16)
    thr = lax.shift_left(hi_cu ^ jnp.int32(0x8000), 16) | lo_cu
    need = kf - (n_above + count(lambda x: x > lo16, lo_buf))
    tri = jnp.where(lax.broadcasted_iota(jnp.int32, (tq, tq), 0) >= lax.broadcasted_iota(jnp.int32, (tq, tq), 1),
                    1.0, 0.0).astype(BF16)

    qa = qa_ref[...]
    q_stack = jnp.concatenate([qa[:, g * LANES:(g + 1) * LANES] for g in range(n_pair)], axis=0)

    def logits_block(j, ties_before, first):
        k = key_buf[rows_of(j), :]
        tied = jnp.where(k == thr, 1.0, 0.0)
        rank = jnp.dot(tri, tied.astype(BF16), preferred_element_type=F32) + ties_before
        keep = jnp.where(k == int_min, 0.0,
                         jnp.where(k > thr, 1.0, jnp.where(rank <= need, tied, 0.0)))
        bias = jnp.where(keep.T > 0.5, 0.0, -jnp.inf)
        bias = jnp.concatenate([bias] * n_pair, axis=0)
        lgs = [lax.dot_general(q_stack, ka_ref[rows_of(j), :], _NT, preferred_element_type=F32)
               for ka_ref in (ka0_ref, ka1_ref)]
        for half, lg in enumerate(lgs):
            sl = slice(half * rows, (half + 1) * rows)
            lg = lg + bias
            lg_buf[j, sl, :] = lg
            folded = jnp.maximum(lg[:, :LANES], lg[:, LANES:]) if tq == 2 * LANES else lg
            m_buf[sl, :] = folded if first else jnp.maximum(m_buf[sl, :], folded)
        return rank[tq - 1:tq, :]

    ties = logits_block(0, jnp.zeros((1, tq), F32), True)
    lax.fori_loop(1, n_blk, lambda j, t: logits_block(j, t, False), ties)
    m_buf[...] = jnp.broadcast_to(jnp.max(m_buf[...], axis=-1, keepdims=True), m_buf.shape)

    def pv_block(j, first):
        v = va_ref[rows_of(j), :]
        for half in range(2):
            sl = slice(half * rows, (half + 1) * rows)
            m = m_buf[sl, :]
            e = jnp.exp2(lg_buf[j, sl, :] - jnp.concatenate([m] * (tq // LANES), axis=1))
            row_sum = e[:, :LANES] + e[:, LANES:] if tq == 2 * LANES else e
            pv = jnp.dot(e.astype(BF16), v, preferred_element_type=F32)
            l_buf[sl, :] = row_sum if first else l_buf[sl, :] + row_sum
            acc_buf[sl, :] = pv if first else acc_buf[sl, :] + pv

    pv_block(0, True)
    lax.fori_loop(1, n_blk, lambda j, c: (pv_block(j, False), c)[1], 0)
    out = acc_buf[...] / jnp.sum(l_buf[...], axis=-1, keepdims=True)
    lane = _lane_iota((tq, LANES))
    for g in range(n_pair):
        o_ref[:, g * LANES:(g + 1) * LANES] = jnp.where(
            lane < HALF, out[g * tq:(g + 1) * tq], out[rows + g * tq:rows + (g + 1) * tq]).astype(BF16)


def _dsa_attention(iq, misc, qa, ik0, ik1, ka0, ka1, va, batch, seq):
    n = qa.shape[0]
    tq = min(Q_TILE, seq)
    assert tq % LANES == 0 and tq // LANES in (1, 2) and seq <= 2 ** 15
    n_q = seq // tq
    topk = min(TOPK_MAX, seq // 4)
    rows = A_HEADS * tq
    qrow = lambda w: pl.BlockSpec((tq, w), lambda b, i: (b * n_q + i, 0))
    krow = pl.BlockSpec((seq, LANES), lambda b, i: (b, 0))
    return pl.pallas_call(
        functools.partial(_dsa_kernel, tq, topk),
        grid=(batch, n_q),
        in_specs=[qrow(iq.shape[1]), qrow(LANES), qrow(qa.shape[1])] + [krow] * 5,
        out_specs=qrow(qa.shape[1]),
        out_shape=jax.ShapeDtypeStruct((n, qa.shape[1]), BF16),
        scratch_shapes=[pltpu.VMEM((seq, tq), jnp.int32), pltpu.VMEM((seq, tq), jnp.int16),
                        pltpu.VMEM((seq, tq), jnp.int16), pltpu.VMEM((n_q, rows, tq), F32),
                        pltpu.VMEM((rows, LANES), F32), pltpu.VMEM((rows, LANES), F32),
                        pltpu.VMEM((rows, LANES), F32)],
        compiler_params=_cparams(("parallel", "arbitrary")),
    )(iq, misc, qa, ik0, ik1, ka0, ka1, va)


def _mla_kernel(tq, kb, q_ref, k_ref, v_ref, o_ref, lg_buf, m_buf, l_buf, acc_buf):
    i = pl.program_id(2)
    per_tile = tq // kb
    q = [q_ref[:, hh * LANES:(hh + 1) * LANES] for hh in range(2)]
    fold = lambda x, op: op(x[:, :LANES], x[:, LANES:]) if kb == 2 * LANES else x

    def rows_of(j):
        return pl.ds(pl.multiple_of(j * kb, kb), kb)

    def first_row(d):
        return 0 if d is None else d * kb

    def logits(js, diags):
        streams = [(u, hh) for u in range(len(js)) for hh in range(2)]
        lg = {}

        def stage_dot(s):
            u, hh = s
            lg[s] = lax.dot_general(q[hh][first_row(diags[u]):], k_ref[rows_of(js[u]), hh * LANES:(hh + 1) * LANES],
                                    _NT, preferred_element_type=F32)

        def stage_store(s):
            u, hh = s
            r0 = first_row(diags[u])
            x = lg[s]
            if diags[u] is not None:
                key_idx = lax.broadcasted_iota(jnp.int32, x.shape, 1) + diags[u] * kb
                query_idx = lax.broadcasted_iota(jnp.int32, x.shape, 0) + r0
                x = jnp.where(lax.shift_right_logical(key_idx, 6) <= lax.shift_right_logical(query_idx, 6),
                              x, -jnp.inf)
            lg_buf[hh, js[u], r0:, :] = x
            m_buf[hh, r0:, :] = jnp.maximum(m_buf[hh, r0:, :], fold(x, jnp.maximum))

        for t in range(len(streams) + 1):
            if t < len(streams):
                stage_dot(streams[t])
            if t >= 1:
                stage_store(streams[t - 1])

    def weighted(js, diags):
        streams = [(u, hh) for u in range(len(js)) for hh in range(2)]
        e = {}

        def stage_exp(s):
            u, hh = s
            r0 = first_row(diags[u])
            x = jnp.exp2(lg_buf[hh, js[u], r0:, :] - jnp.concatenate([m_buf[hh, r0:, :]] * (kb // LANES), axis=1))
            l_buf[hh, r0:, :] = l_buf[hh, r0:, :] + fold(x, jnp.add)
            e[s] = x.astype(BF16)

        def stage_pv(s):
            u, hh = s
            r0 = first_row(diags[u])
            acc_buf[hh, r0:, :] = acc_buf[hh, r0:, :] + jnp.dot(e[s], v_ref[rows_of(js[u]), :],
                                                              preferred_element_type=F32)

        for t in range(len(streams) + 1):
            if t < len(streams):
                stage_exp(streams[t])
            if t >= 1:
                stage_pv(streams[t - 1])

    diag = list(range(per_tile))
    n_full = i * per_tile
    full = lambda s: [s * per_tile + u for u in range(per_tile)]
    none = [None] * per_tile

    m_buf[...] = jnp.full(m_buf.shape, -jnp.inf, F32)
    logits([n_full + d for d in diag], diag)
    lax.fori_loop(0, i, lambda s, c: (logits(full(s), none), c)[1], 0)
    for hh in range(2):
        m_buf[hh] = jnp.broadcast_to(jnp.max(m_buf[hh], axis=-1, keepdims=True), (tq, LANES))
    l_buf[...] = jnp.zeros(l_buf.shape, F32)
    acc_buf[...] = jnp.zeros(acc_buf.shape, F32)
    weighted([n_full + d for d in diag], diag)
    lax.fori_loop(0, i, lambda s, c: (weighted(full(s), none), c)[1], 0)
    lane = _lane_iota((tq, LANES))
    outs = [acc_buf[hh] / jnp.sum(l_buf[hh], axis=-1, keepdims=True) for hh in range(2)]
    o_ref[...] = jnp.where(lane < HALF, outs[0], outs[1]).astype(BF16)


def _mla_attention(qb, kf, vb, batch, seq):
    n = qb.shape[0]
    tq = min(MLA_Q_TILE, seq)
    kb = min(MLA_KEY_BLOCK, tq)
    assert kb % LANES == 0 and kb // LANES in (1, 2) and kb % CHUNK == 0
    n_q = seq // tq
    n_pair = B_HEADS // 2
    return pl.pallas_call(
        functools.partial(_mla_kernel, tq, kb),
        grid=(batch, n_pair, n_q),
        in_specs=[pl.BlockSpec((tq, 2 * LANES), lambda b, p, i: (b * n_q + i, p)),
                  pl.BlockSpec((seq, 2 * LANES), lambda b, p, i: (b, p)),
                  pl.BlockSpec((seq, LANES), lambda b, p, i: (b, p))],
        out_specs=pl.BlockSpec((tq, LANES), lambda b, p, i: (b * n_q + i, p)),
        out_shape=jax.ShapeDtypeStruct((n, B_HEADS * B_V), BF16),
        scratch_shapes=[pltpu.VMEM((2, seq // kb, tq, kb), F32), pltpu.VMEM((2, tq, LANES), F32),
                        pltpu.VMEM((2, tq, LANES), F32), pltpu.VMEM((2, tq, LANES), F32)],
        compiler_params=_cparams(("parallel", "parallel", "arbitrary")),
    )(qb, kf, vb)


def _odd_proj_kernel(x_ref, mod_ref, w_ref, o_ref):
    sh = mod_ref[0, 0:1, :]
    sc = mod_ref[0, 1:2, :]
    h = (x_ref[...] * (1.0 + sc) + sh).astype(BF16)
    p = jnp.dot(h, w_ref[...], preferred_element_type=F32)
    dq = C_HEADS * C_HEAD_DIM
    o_ref[:, 0:dq] = (p[:, 0:dq] * (C_HEAD_DIM ** -0.5)).astype(BF16)
    o_ref[:, dq:] = p[:, dq:].astype(BF16)


def _odd_proj(x, mod_l, w_qkv, batch, seq):
    n, d = x.shape
    tm = min(PROJ_TILE, seq)
    nt = seq // tm
    w = w_qkv.astype(BF16)
    return pl.pallas_call(
        _odd_proj_kernel,
        grid=(batch, nt),
        in_specs=[pl.BlockSpec((tm, d), lambda b, t: (b * nt + t, 0)),
                  pl.BlockSpec((1, 6, d), lambda b, t: (b, 0, 0)), _const_spec(w.shape)],
        out_specs=pl.BlockSpec((tm, w.shape[1]), lambda b, t: (b * nt + t, 0)),
        out_shape=jax.ShapeDtypeStruct((n, w.shape[1]), BF16),
        compiler_params=_cparams(("parallel", "parallel")),
    )(x, mod_l, w)


def _stick_kernel(tq, kb, q_ref, k_ref, v_ref, o_ref, acc_ref):
    i = pl.program_id(2)
    per_tile = tq // kb
    q2 = q_ref[...]
    lane = _lane_iota((tq, LANES))
    tri = jnp.where(lax.broadcasted_iota(jnp.int32, (kb, kb), 0) >= lax.broadcasted_iota(jnp.int32, (kb, kb), 1),
                    1.0, 0.0).astype(BF16)
    row = lax.broadcasted_iota(jnp.int32, (tq, kb), 0)
    col = lax.broadcasted_iota(jnp.int32, (tq, kb), 1)
    q_heads = (jnp.where(lane < HALF, q2, jnp.zeros_like(q2)), jnp.where(lane < HALF, jnp.zeros_like(q2), q2))
    acc_ref[...] = jnp.zeros(acc_ref.shape, F32)

    def blocks(js, carries, diags):
        streams = [(u, hh) for u in range(len(js)) for hh in range(2)]
        kv = []
        for j in js:
            start = pl.multiple_of(j * kb, kb)
            kv.append((k_ref[pl.ds(start, kb), :], v_ref[pl.ds(start, kb), :]))
        before = [None if d is None else (col + d * kb < row) for d in diags]
        z, inc = {}, {}
        carry = list(carries)

        r0 = [0 if d is None else d * kb for d in diags]

        def stage_z(s):
            z[s] = lax.dot_general(q_heads[s[1]][r0[s[0]]:], kv[s[0]][0], _NT, preferred_element_type=F32)

        def stage_sum(s):
            sp = jnp.where(z[s] > SOFTPLUS_LINEAR, z[s], jnp.log(1.0 + jnp.exp(z[s])))
            if before[s[0]] is not None:
                sp = jnp.where(before[s[0]][r0[s[0]]:], sp, 0.0)
            inc[s] = jnp.dot(sp.astype(BF16), tri, preferred_element_type=F32)

        def stage_out(s):
            u, hh = s
            a = jnp.exp(z[s] - inc[s] - carry[hh][r0[u]:])
            if before[u] is not None:
                a = jnp.where(before[u][r0[u]:], a, 0.0)
            acc_ref[hh, r0[u]:, :] = acc_ref[hh, r0[u]:, :] + jnp.dot(a.astype(BF16), kv[u][1],
                                                                      preferred_element_type=F32)
            tot = carry[hh][r0[u]:] + inc[s][:, 0:1]
            carry[hh] = tot if r0[u] == 0 else jnp.concatenate([carry[hh][:r0[u]], tot], axis=0)

        for t in range(len(streams) + 2):
            if t < len(streams):
                stage_z(streams[t])
            if 0 <= t - 1 < len(streams):
                stage_sum(streams[t - 1])
            if 0 <= t - 2 < len(streams):
                stage_out(streams[t - 2])
        return tuple(carry)

    carries = (jnp.zeros((tq, 1), F32), jnp.zeros((tq, 1), F32))
    diag = list(reversed(range(per_tile)))
    carries = blocks([i * per_tile + d for d in diag], carries, diag)
    n_full = i * per_tile
    lax.fori_loop(0, i, lambda s, cs: blocks([n_full - 1 - s * per_tile - u for u in range(per_tile)], cs,
                                             [None] * per_tile), carries)
    o_ref[...] = jnp.where(lane < HALF, acc_ref[0], acc_ref[1]).astype(BF16)


def _stick_attention(qkv, batch, seq):
    n = qkv.shape[0]
    tq = min(STICK_Q_TILE, seq)
    kb = min(STICK_KEY_BLOCK, tq)
    n_q = seq // tq
    n_pair = C_HEADS // 2
    return pl.pallas_call(
        functools.partial(_stick_kernel, tq, kb),
        grid=(batch, n_pair, n_q),
        in_specs=[pl.BlockSpec((tq, LANES), lambda b, p, i: (b * n_q + i, p)),
                  pl.BlockSpec((seq, LANES), lambda b, p, i: (b, n_pair + p)),
                  pl.BlockSpec((seq, LANES), lambda b, p, i: (b, 2 * n_pair + p))],
        out_specs=pl.BlockSpec((tq, LANES), lambda b, p, i: (b * n_q + i, p)),
        out_shape=jax.ShapeDtypeStruct((n, C_HEADS * C_HEAD_DIM), BF16),
        scratch_shapes=[pltpu.VMEM((2, tq, LANES), F32)],
        compiler_params=_cparams(("parallel", "parallel", "arbitrary")),
    )(qkv, qkv, qkv)


_HALO = 8


def _mix_ffn_kernel(tm, n_in, *refs):
    x_ref, mod_ref = refs[0], refs[1]
    ins = refs[2:2 + n_in]
    wos = refs[2 + n_in:2 + 2 * n_in]
    mg_ref, mb_ref, wup_ref, cw_ref, cb_ref, wdn_ref, g_ref, b_ref, o_ref, a_buf, tail_buf = refs[2 + 2 * n_in:]
    t = pl.program_id(1)
    sh = mod_ref[0, 3:4, :]
    sc = mod_ref[0, 4:5, :]
    gate = mod_ref[0, 5:6, :]

    @pl.when(t == 0)
    def _():
        tail_buf[...] = jnp.zeros_like(tail_buf)

    def up(hh, ci):
        cs = slice(ci * FF_CHUNK, (ci + 1) * FF_CHUNK)
        gs = slice(D_FF + ci * FF_CHUNK, D_FF + (ci + 1) * FF_CHUNK)
        return (jnp.dot(hh, wup_ref[:, cs], preferred_element_type=F32),
                jnp.dot(hh, wup_ref[:, gs], preferred_element_type=F32))

    halves = [slice(0, tm // 2), slice(tm // 2, tm)]
    mixes = []
    for rs in halves:
        mix = jnp.dot(ins[0][rs, :], wos[0][...], preferred_element_type=F32)
        for a_ref, w_ref in zip(ins[1:], wos[1:]):
            mix = mix + jnp.dot(a_ref[rs, :], w_ref[...], preferred_element_type=F32)
        mixes.append(mix)
    xs, hs, first = [], [], []
    for rs, mix in zip(halves, mixes):
        xh = _layer_norm(DEEPNORM_ALPHA * x_ref[rs, :] + (1.0 + mod_ref[0, 2:3, :]) * mix, mg_ref[...], mb_ref[...])
        hh = (xh * (1.0 + sc) + sh).astype(BF16)
        xs.append(xh)
        hs.append(hh)
        first.append(up(hh, 0))
    x = jnp.concatenate(xs, axis=0)
    h = jnp.concatenate(hs, axis=0)
    nxt = tuple(jnp.concatenate([f[k] for f in first], axis=0) for k in range(2))

    n_chunk = D_FF // FF_CHUNK
    y = jnp.zeros((tm, x.shape[1]), F32)
    for ci in range(n_chunk):
        cs = slice(ci * FF_CHUNK, (ci + 1) * FF_CHUNK)
        a, gt = nxt
        if ci + 1 < n_chunk:
            nxt = up(h, ci + 1)
        a_buf[0:_HALO, :] = tail_buf[:, cs]
        a_buf[_HALO:, :] = a
        tail_buf[:, cs] = a[tm - _HALO:, :]
        conv = (a_buf[_HALO - 2:_HALO - 2 + tm, :] * cw_ref[0:1, cs]
                + a_buf[_HALO - 1:_HALO - 1 + tm, :] * cw_ref[1:2, cs]
                + a * cw_ref[2:3, cs] + cb_ref[:, cs])
        u = conv / (1.0 + jnp.exp(-conv)) * gt
        y = y + jnp.dot(u.astype(BF16), wdn_ref[cs, :], preferred_element_type=F32)
    o_ref[...] = _layer_norm(DEEPNORM_ALPHA * x + (1.0 + gate) * y, g_ref[...], b_ref[...])


def _mix_ffn(x, mod_l, acts, w_out, mix_g, mix_b, w_up, conv_w, conv_b, w_down, ffn_g, ffn_b, batch, seq):
    n, d = x.shape
    tm = min(TOKEN_TILE, seq)
    nt = seq // tm
    row = lambda w: pl.BlockSpec((tm, w), lambda b, t: (b * nt + t, 0))
    single = lambda shape: pl.BlockSpec(shape, lambda b, t: (0,) * len(shape), pipeline_mode=pl.Buffered(1))
    wos, off = [], 0
    for a in acts:
        wos.append(w_out[off:off + a.shape[1]].astype(BF16))
        off += a.shape[1]
    vec = lambda v: v.reshape(1, -1)
    return pl.pallas_call(
        functools.partial(_mix_ffn_kernel, tm, len(acts)),
        grid=(batch, nt),
        in_specs=[row(d), pl.BlockSpec((1, 6, d), lambda b, t: (b, 0, 0))]
                 + [row(a.shape[1]) for a in acts] + [single(w.shape) for w in wos]
                 + [_const_spec((1, d))] * 2
                 + [single((d, 2 * D_FF)), _const_spec((CONV_WIDTH, D_FF)), _const_spec((1, D_FF)), single((D_FF, d)),
                    _const_spec((1, d)), _const_spec((1, d))],
        out_specs=row(d),
        out_shape=jax.ShapeDtypeStruct((n, d), F32),
        scratch_shapes=[pltpu.VMEM((_HALO + tm, FF_CHUNK), F32), pltpu.VMEM((_HALO, D_FF), F32)],
        compiler_params=_cparams(("parallel", "arbitrary")),
    )(x, mod_l, *acts, *wos, vec(mix_g), vec(mix_b), w_up.astype(BF16), conv_w, vec(conv_b), w_down.astype(BF16),
      vec(ffn_g), vec(ffn_b))


def kernel(x, c, positions, mod_w, mod_b, ln_mix_g, ln_mix_b, ln_ffn_g, ln_ffn_b, ev_w_in, ev_idx_k_g, ev_idx_k_b, ev_q_norm_g, ev_kv_norm_g, ev_w_uq, ev_w_ukv, ev_w_out, od_w_qkv, od_w_out, ffn_w_up, ffn_conv_w, ffn_conv_b, ffn_w_down):
    batch, seq, d = x.shape
    assert d == D_MODEL and seq % min(Q_TILE, seq) == 0 and seq % CHUNK == 0
    n = batch * seq
    depth = mod_w.shape[0]
    mod = _modulation(c, mod_w, mod_b).reshape(depth, batch, 6, d)
    tables = _rope_tables(positions)
    xs = x.reshape(n, d)
    for l in range(depth):
        i = l // 2
        if l % 2 == 0:
            weights = _even_weights(ev_w_in[i], ev_idx_k_g[i], ev_idx_k_b[i], ev_q_norm_g[i],
                                    ev_kv_norm_g[i], ev_w_uq[i], ev_w_ukv[i])
            qa, iq, ka0, ka1, va, ik0, ik1, misc, qb, kf, vb = _even_proj(xs, mod[l], tables, weights, batch, seq)
            out_a = _dsa_attention(iq, misc, qa, ik0, ik1, ka0, ka1, va, batch, seq)
            out_b = _mla_attention(qb, kf, vb, batch, seq)
            acts, w_out = [out_a, out_b], ev_w_out[i]
        else:
            qkv = _odd_proj(xs, mod[l], od_w_qkv[i], batch, seq)
            acts, w_out = [_stick_attention(qkv, batch, seq)], od_w_out[i]
        xs = _mix_ffn(xs, mod[l], acts, w_out, ln_mix_g[l], ln_mix_b[l], ffn_w_up[l], ffn_conv_w[l], ffn_conv_b[l],
                      ffn_w_down[l], ln_ffn_g[l], ln_ffn_b[l], batch, seq)
    return xs.reshape(batch, seq, d)
```
